```python
import math
import jax, jax.numpy as jnp
from jax import lax
import numpy as np


D_MODEL = 2048
BATCH = 1
SEQ = 8192
DEPTH = 4

D_MIX = D_MODEL
HEAD_DIM = 128
A_HEADS = 6
A_WIDTH = A_HEADS * HEAD_DIM
IDX_HEADS = 8
IDX_DIM = 64
TOPK_MAX = 256
M_HEADS = 6
M_DH = HEAD_DIM
M_WIDTH = M_HEADS * M_DH
M_CHUNK = 64
CONV_W = 4
C_HEADS = 4
C_QK = 64
C_DV = 2 * C_QK
C_WIDTH = C_HEADS * C_DV
D_FF = 4 * D_MODEL
Q_BLOCK = 128
ALPHA = (2.0 * DEPTH) ** 0.25
BETA = (8.0 * DEPTH) ** -0.25
EPS = 1e-5

SIZES = (A_WIDTH, A_WIDTH, A_WIDTH, IDX_HEADS * IDX_DIM, IDX_DIM, IDX_HEADS,
         M_WIDTH, M_WIDTH, M_WIDTH, M_WIDTH, M_HEADS, M_HEADS,
         C_HEADS * 2 * C_QK, C_HEADS * 2 * C_QK, C_WIDTH)
VALUE_SEGMENTS = (2, 8, 14)
D_IN = sum(SIZES)
SPLIT_POINTS = tuple(int(p) for p in np.cumsum(SIZES)[:-1])

kernel_name = "hybrid_dsa_mlstm_diffattn_deepnorm"


def alibi_slopes(n):
    return 2.0 ** (-8.0 * jnp.arange(1, n + 1, dtype=jnp.float32) / n)


def layer_norm(x, g, b):
    xf = x.astype(jnp.float32)
    mu = jnp.mean(xf, axis=-1, keepdims=True)
    var = jnp.mean(jnp.square(xf - mu), axis=-1, keepdims=True)
    return ((xf - mu) * lax.rsqrt(var + EPS) * g + b).astype(x.dtype)


def head_rmsnorm(h, g):
    hf = h.astype(jnp.float32)
    return (hf * lax.rsqrt(jnp.mean(hf * hf, axis=-1, keepdims=True) + EPS) * g).astype(h.dtype)


def causal_conv(x, w):
    return lax.conv_general_dilated(x, w.astype(x.dtype), window_strides=(1,),
                                    padding=[(CONV_W - 1, 0)],
                                    dimension_numbers=('NWC', 'WIO', 'NWC'),
                                    feature_group_count=x.shape[-1])


def dsa_attention(q, k, v, q_idx, k_idx, w_idx):
    B, S, H, D = q.shape
    nb = S // Q_BLOCK
    k_sel = min(TOPK_MAX, S // 4)
    slopes = alibi_slopes(H)
    scale = D ** -0.5
    pos_k = jnp.arange(S)

    def block(args):
        qb, qib, wb, t0 = args
        pos_q = t0 + jnp.arange(Q_BLOCK)
        sc = jax.nn.relu(jnp.einsum('bqhd,bsd->bqhs', qib, k_idx).astype(jnp.float32))
        idx_score = jnp.einsum('bqhs,bqh->bqs', sc, wb.astype(jnp.float32))
        causal = pos_k[None, :] <= pos_q[:, None]
        idx_score = jnp.where(causal[None], idx_score, -jnp.inf)
        _, sel = lax.top_k(idx_score, k_sel)
        valid = sel <= pos_q[None, :, None]
        k_g = jax.vmap(lambda kk, ii: kk[ii])(k, sel)
        v_g = jax.vmap(lambda vv, ii: vv[ii])(v, sel)
        logits = jnp.einsum('bqhd,bqkhd->bhqk', qb, k_g).astype(jnp.float32) * scale
        dist = (pos_q[None, :, None] - sel).astype(jnp.float32)
        logits = logits - slopes[None, :, None, None] * dist[:, None]
        logits = jnp.where(valid[:, None], logits, -jnp.inf)
        p = jax.nn.softmax(logits, axis=-1)
        return jnp.einsum('bhqk,bqkhd->bqhd', p.astype(v.dtype), v_g)

    qs = q.reshape(B, nb, Q_BLOCK, H, D).transpose(1, 0, 2, 3, 4)
    qis = q_idx.reshape(B, nb, Q_BLOCK, IDX_HEADS, IDX_DIM).transpose(1, 0, 2, 3, 4)
    ws = w_idx.reshape(B, nb, Q_BLOCK, IDX_HEADS).transpose(1, 0, 2, 3)
    t0s = jnp.arange(nb) * Q_BLOCK
    out = lax.map(block, (qs, qis, ws, t0s))
    return out.transpose(1, 0, 2, 3, 4).reshape(B, S, H, D)


def mlstm_chunkwise(q, k, v, i_pre, f_pre):
    B, S, H, D = q.shape
    L = M_CHUNK
    nc = S // L
    f32 = jnp.float32

    def chunks(t):
        return t.astype(f32).reshape(B, nc, L, H, D).transpose(1, 0, 3, 2, 4)

    def gchunks(t):
        return t.reshape(B, nc, L, H).transpose(1, 0, 3, 2)

    log_f = gchunks(jax.nn.log_sigmoid(f_pre.astype(f32)))
    log_i = gchunks(i_pre.astype(f32))
    tril = jnp.tril(jnp.ones((L, L), dtype=bool))

    def step(carry, inp):
        C, n, m = carry
        qj, kj, vj, lfj, lij = inp
        a = jnp.cumsum(lfj, axis=-1)
        g = a[..., -1]
        d_intra = a[..., :, None] - a[..., None, :] + lij[..., None, :]
        d_intra = jnp.where(tril, d_intra, -jnp.inf)
        inter = a + m[..., None]
        m_row = jnp.maximum(inter, jnp.max(d_intra, axis=-1))
        w_intra = jnp.exp(d_intra - m_row[..., None])
        w_inter = jnp.exp(inter - m_row)
        qk = jnp.einsum('bhld,bhsd->bhls', qj, kj) * w_intra
        num = jnp.einsum('bhls,bhse->bhle', qk, vj) + w_inter[..., None] * jnp.einsum('bhld,bhde->bhle', qj, C)
        den = jnp.sum(qk, axis=-1) + w_inter * jnp.einsum('bhld,bhd->bhl', qj, n)
        h = num / jnp.maximum(jnp.abs(den), jnp.exp(-m_row))[..., None]
        decay = g[..., None] - a + lij
        m_new = jnp.maximum(g + m, jnp.max(decay, axis=-1))
        ws = jnp.exp(decay - m_new[..., None])
        wc = jnp.exp(g + m - m_new)
        C_new = wc[..., None, None] * C + jnp.einsum('bhs,bhsd,bhse->bhde', ws, kj, vj)
        n_new = wc[..., None] * n + jnp.einsum('bhs,bhsd->bhd', ws, kj)
        return (C_new, n_new, m_new), h

    init = (jnp.zeros((B, H, D, D), f32), jnp.zeros((B, H, D), f32), jnp.zeros((B, H), f32))
    _, hs = lax.scan(step, init, (chunks(q), chunks(k), chunks(v), log_f, log_i))
    return hs.transpose(1, 0, 3, 2, 4).reshape(B, S, H, D).astype(v.dtype)


def diff_attention(q, k, v, lam):
    B, S, H, _, d = q.shape
    nb = S // Q_BLOCK
    slopes = alibi_slopes(H)
    scale = d ** -0.5
    pos_k = jnp.arange(S)

    def block(args):
        qb, t0 = args
        pos_q = t0 + jnp.arange(Q_BLOCK)
        s = jnp.einsum('bqhmd,bshmd->bmhqs', qb, k).astype(jnp.float32) * scale
        dist = (pos_q[:, None] - pos_k[None, :]).astype(jnp.float32)
        s = jnp.where(dist >= 0, s - slopes[:, None, None] * dist, -jnp.inf)
        p = jax.nn.softmax(s, axis=-1)
        attn = p[:, 0] - lam * p[:, 1]
        return jnp.einsum('bhqs,bshe->bqhe', attn.astype(v.dtype), v)

    qs = q.reshape(B, nb, Q_BLOCK, H, 2, d).transpose(1, 0, 2, 3, 4, 5)
    t0s = jnp.arange(nb) * Q_BLOCK
    out = lax.map(block, (qs, t0s))
    return out.transpose(1, 0, 2, 3, 4).reshape(B, S, H, v.shape[-1])


def token_mixers(xn, layer_idx, w_in, conv_w, b_i, b_f, m_norm_g, lq1, lk1, lq2, lk2, c_norm_g):
    B, S, _ = xn.shape
    proj = jnp.einsum('bsd,de->bse', xn, w_in)
    (q_a, k_a, v_a, q_i, k_i, w_i, q_m, k_m, v_m, o_m, i_m, f_m,
     q_c, k_c, v_c) = jnp.split(proj, SPLIT_POINTS, axis=-1)

    y_a = dsa_attention(q_a.reshape(B, S, A_HEADS, HEAD_DIM), k_a.reshape(B, S, A_HEADS, HEAD_DIM),
                        v_a.reshape(B, S, A_HEADS, HEAD_DIM), q_i.reshape(B, S, IDX_HEADS, IDX_DIM),
                        k_i, w_i).reshape(B, S, A_WIDTH)

    qk_m = jax.nn.silu(causal_conv(jnp.concatenate([q_m, k_m], axis=-1), conv_w))
    q_m, k_m = jnp.split(qk_m, 2, axis=-1)
    h_m = mlstm_chunkwise(q_m.reshape(B, S, M_HEADS, M_DH),
                          k_m.reshape(B, S, M_HEADS, M_DH) * (M_DH ** -0.5),
                          v_m.reshape(B, S, M_HEADS, M_DH), i_m + b_i, f_m + b_f)
    y_m = jax.nn.sigmoid(o_m) * head_rmsnorm(h_m, m_norm_g.reshape(M_HEADS, M_DH)).reshape(B, S, M_WIDTH)

    lam_init = 0.8 - 0.6 * math.exp(-0.3 * layer_idx)
    lam = (jnp.exp(jnp.sum(lq1.astype(jnp.float32) * lk1)) - jnp.exp(jnp.sum(lq2.astype(jnp.float32) * lk2))
           + lam_init)
    h_c = diff_attention(q_c.reshape(B, S, C_HEADS, 2, C_QK), k_c.reshape(B, S, C_HEADS, 2, C_QK),
                         v_c.reshape(B, S, C_HEADS, C_DV), lam)
    y_c = (head_rmsnorm(h_c, c_norm_g.reshape(C_HEADS, C_DV)) * (1.0 - lam_init)).reshape(B, S, C_WIDTH)

    return jnp.concatenate([y_a, y_m, y_c], axis=-1)


def setup_inputs(seed: int = 0) -> dict:
    key = jax.random.key(seed)
    ks = jax.random.split(key, 20)
    f32 = jnp.float32
    col_scale = jnp.concatenate([jnp.full((s,), BETA if i in VALUE_SEGMENTS else 1.0, f32)
                                 for i, s in enumerate(SIZES)])
    x = jax.random.normal(ks[0], (BATCH, SEQ, D_MODEL), f32)
    w_in = jax.random.normal(ks[1], (DEPTH, D_MODEL, D_IN), f32) * (D_MODEL ** -0.5) * col_scale
    conv_m = jax.random.normal(ks[2], (DEPTH, CONV_W, 1, 2 * M_WIDTH), f32) * (CONV_W ** -0.5)
    b_i = 0.01 * jax.random.normal(ks[3], (DEPTH, M_HEADS), f32)
    b_f = jnp.linspace(3.0, 6.0, M_HEADS, dtype=f32)[None] + 0.01 * jax.random.normal(ks[4], (DEPTH, M_HEADS), f32)
    m_norm_g = 1.0 + 0.01 * jax.random.normal(ks[5], (DEPTH, M_WIDTH), f32)
    lam_q1 = 0.1 * jax.random.normal(ks[6], (DEPTH, C_QK), f32)
    lam_k1 = 0.1 * jax.random.normal(ks[7], (DEPTH, C_QK), f32)
    lam_q2 = 0.1 * jax.random.normal(ks[8], (DEPTH, C_QK), f32)
    lam_k2 = 0.1 * jax.random.normal(ks[9], (DEPTH, C_QK), f32)
    c_norm_g = 1.0 + 0.01 * jax.random.normal(ks[10], (DEPTH, C_WIDTH), f32)
    w_out = jax.random.normal(ks[11], (DEPTH, D_MIX, D_MODEL), f32) * (D_MIX ** -0.5) * BETA
    ln1_g = 1.0 + 0.01 * jax.random.normal(ks[12], (DEPTH, D_MODEL), f32)
    ln1_b = 0.01 * jax.random.normal(ks[13], (DEPTH, D_MODEL), f32)
    w_up = jax.random.normal(ks[14], (DEPTH, D_MODEL, D_FF), f32) * (D_MODEL ** -0.5) * BETA
    w_down = jax.random.normal(ks[15], (DEPTH, D_FF, D_MODEL), f32) * (D_FF ** -0.5) * BETA
    ln2_g = 1.0 + 0.01 * jax.random.normal(ks[16], (DEPTH, D_MODEL), f32)
    ln2_b = 0.01 * jax.random.normal(ks[17], (DEPTH, D_MODEL), f32)
    return {"x": x, "w_in": w_in, "conv_m": conv_m, "b_i": b_i, "b_f": b_f, "m_norm_g": m_norm_g,
            "lam_q1": lam_q1, "lam_k1": lam_k1, "lam_q2": lam_q2, "lam_k2": lam_k2,
            "c_norm_g": c_norm_g, "w_out": w_out, "ln1_g": ln1_g, "ln1_b": ln1_b,
            "w_up": w_up, "w_down": w_down, "ln2_g": ln2_g, "ln2_b": ln2_b}


def reference(x, w_in, conv_m, b_i, b_f, m_norm_g, lam_q1, lam_k1, lam_q2, lam_k2,
              c_norm_g, w_out, ln1_g, ln1_b, w_up, w_down, ln2_g, ln2_b):
    for l in range(DEPTH):
        mixed = token_mixers(x, l, w_in[l], conv_m[l], b_i[l], b_f[l], m_norm_g[l],
                             lam_q1[l], lam_k1[l], lam_q2[l], lam_k2[l], c_norm_g[l])
        x = layer_norm(ALPHA * x + jnp.einsum('bse,ed->bsd', mixed, w_out[l]), ln1_g[l], ln1_b[l])
        h = jnp.square(jax.nn.relu(jnp.einsum('bsd,df->bsf', x, w_up[l])))
        x = layer_norm(ALPHA * x + jnp.einsum('bsf,fd->bsd', h, w_down[l]), ln2_g[l], ln2_b[l])
    return x
```

```python
import functools
import math

import jax
import jax.numpy as jnp
from jax import lax
from jax.experimental import pallas as pl
from jax.experimental.pallas import tpu as pltpu

F32 = jnp.float32
BF16 = jnp.bfloat16
I32 = jnp.int32

D_MODEL = 2048
DEPTH = 4
HEAD_DIM = 128
A_HEADS = 6
A_WIDTH = A_HEADS * HEAD_DIM
IDX_HEADS = 8
IDX_DIM = 64
TOPK_MAX = 256
M_HEADS = 6
M_WIDTH = M_HEADS * HEAD_DIM
CONV_W = 4
C_HEADS = 4
C_QK = 64
C_WIDTH = C_HEADS * 2 * C_QK
D_FF = 4 * D_MODEL
ALPHA = (2.0 * DEPTH) ** 0.25
EPS = 1e-5

SIZES = (A_WIDTH, A_WIDTH, A_WIDTH, IDX_HEADS * IDX_DIM, IDX_DIM, IDX_HEADS,
         M_WIDTH, M_WIDTH, M_WIDTH, M_WIDTH, M_HEADS, M_HEADS,
         C_WIDTH, C_WIDTH, C_WIDTH)
_OFFS = tuple(int(sum(SIZES[:i])) for i in range(len(SIZES)))

LANES = 128
SUBLANES = 8
VMEM_LIMIT_BYTES = 56 * 1024 * 1024
INT_MIN = -2 ** 31
NEG_INF = float("-inf")

PB_QA, PB_KA, PB_VA, PB_VM = 0, 768, 1536, 2304
PB_QC, PB_KC, PB_VC, PB_QI = 3072, 3584, 4096, 4608
PB_KK = 5120
PB_WIDTH = 5376
PF_QM, PF_KM, PF_OM = 0, 768, 1536
PF_WI, PF_GI, PF_GF = 2304, 2432, 2560
PF_WIDTH = 3072


def _cparams(*sem):
    return pltpu.CompilerParams(dimension_semantics=sem, vmem_limit_bytes=VMEM_LIMIT_BYTES)


def _alibi_slopes(n):
    return [2.0 ** (-8.0 * (h + 1) / n) for h in range(n)]


def _mm_kernel(x_ref, w_ref, o_ref, xb_ref):
    @pl.when(pl.program_id(1) == 0)
    def _():
        xb_ref[...] = x_ref[...].astype(BF16)

    o_ref[...] = jnp.dot(xb_ref[...], w_ref[...], preferred_element_type=F32).astype(o_ref.dtype)


def _matmul(x, w, out_dtype, tm, tn, name):
    m, k = x.shape
    n = w.shape[1]
    return pl.pallas_call(
        _mm_kernel,
        grid=(m // tm, n // tn),
        in_specs=[pl.BlockSpec((tm, k), lambda i, j: (i, 0)),
                  pl.BlockSpec((k, tn), lambda i, j: (0, j))],
        out_specs=pl.BlockSpec((tm, tn), lambda i, j: (i, j)),
        out_shape=jax.ShapeDtypeStruct((m, n), out_dtype),
        scratch_shapes=[pltpu.VMEM((tm, k), BF16)],
        compiler_params=_cparams("arbitrary", "arbitrary"),
        name=name,
    )(x, w)


def _idx_kernel(qi_ref, kk_ref, wi_ref, mask_ref, q8_ref, w8_ref, keys_ref, *, tq, tk, nk, ksel, jbits):
    i = pl.program_id(0)
    nck = ((i + 1) * tq + tk - 1) // tk

    lane = lax.broadcasted_iota(I32, (tq, LANES), 1)
    lo = jnp.where(lane < IDX_DIM, 1.0, 0.0).astype(F32)
    hi = 1.0 - lo
    for h in range(IDX_HEADS):
        qp = qi_ref[:, (h // 2) * LANES:(h // 2 + 1) * LANES].astype(F32)
        q8_ref[h * tq:(h + 1) * tq, :] = (qp * (lo if h % 2 == 0 else hi)).astype(BF16)
        w8_ref[h * tq:(h + 1) * tq, :] = wi_ref[:, h:h + 1]

    row = i * tq + lax.broadcasted_iota(I32, (tq, tk), 0)
    col0 = lax.broadcasted_iota(I32, (tq, tk), 1)

    def score_chunk(c, carry):
        kc = kk_ref[pl.ds(pl.multiple_of(c * tk, tk), tk), :]
        y = lax.dot_general(q8_ref[...], kc, (((1,), (1,)), ((), ())), preferred_element_type=F32)
        z = jnp.maximum(y, 0.0) * w8_ref[...]
        sc = z[0:tq]
        for h in range(1, IDX_HEADS):
            sc = sc + z[h * tq:(h + 1) * tq]
        bits = lax.bitcast_convert_type(sc, I32)
        key = jnp.where(bits < 0, bits ^ jnp.int32(0x7FFFFFFF), bits)
        key = jnp.where(bits == jnp.int32(INT_MIN), 0, key)
        key = jnp.where(c * tk + col0 <= row, key, jnp.int32(INT_MIN))
        keys_ref[c] = key
        return carry

    lax.fori_loop(0, nck, score_chunk, 0)

    def lane_fold(m):
        part = m[:, 0:LANES]
        for g in range(1, tk // LANES):
            part = part + m[:, g * LANES:(g + 1) * LANES]
        return part

    def count(pred):
        def body(c, acc):
            return acc + lane_fold(jnp.where(pred(keys_ref[c], c), 1.0, 0.0).astype(F32))
        acc = lax.fori_loop(0, nck, body, jnp.zeros((tq, LANES), F32))
        return jnp.sum(acc, axis=1, keepdims=True)

    def bit_body(b, u):
        cand_u = u | lax.shift_left(jnp.int32(1), 31 - b)
        cand = cand_u ^ jnp.int32(INT_MIN)
        cnt = count(lambda k, c: k >= cand)
        return jnp.where(cnt >= ksel, cand_u, u)

    u = lax.fori_loop(0, 32, bit_body, jnp.zeros((tq, 1), I32))
    thr = u ^ jnp.int32(INT_MIN)
    n_gt = count(lambda k, c: k > thr)
    n_ge = count(lambda k, c: k >= thr)
    need = ksel - n_gt
    excess = jnp.logical_and(thr > jnp.int32(INT_MIN), n_ge > ksel)
    any_excess = jnp.max(jnp.where(excess, 1.0, 0.0)) > 0.0
    j_all = jnp.full((tq, 1), 2 ** jbits - 1, I32)

    def tie_search():
        def jbody(b, jc):
            cand = jc + lax.shift_left(jnp.int32(1), jbits - 1 - b)
            cnt = count(lambda k, c: jnp.logical_and(k == thr, c * tk + col0 < cand))
            return jnp.where(cnt <= need, cand, jc)
        return lax.fori_loop(0, jbits, jbody, jnp.zeros((tq, 1), I32))

    jcut = lax.cond(any_excess, tie_search, lambda: j_all)

    for c in range(nk):
        @pl.when(c < nck)
        def _():
            key = keys_ref[c]
            tie = jnp.logical_and(key == thr, c * tk + col0 < jcut)
            sel = jnp.logical_and(key > jnp.int32(INT_MIN), jnp.logical_or(key > thr, tie))
            mask_ref[:, c * tk:(c + 1) * tk] = jnp.where(sel, 1.0, 0.0).astype(BF16)

        @pl.when(c >= nck)
        def _():
            mask_ref[:, c * tk:(c + 1) * tk] = jnp.zeros((tq, tk), BF16)


def _dsa_mask(pb, pf, ksel, tq, tk):
    s = pb.shape[0]
    nk = s // tk
    jbits = int(s).bit_length()
    kern = functools.partial(_idx_kernel, tq=tq, tk=tk, nk=nk, ksel=ksel, jbits=jbits)
    return pl.pallas_call(
        kern,
        grid=(s // tq,),
        in_specs=[pl.BlockSpec((tq, IDX_HEADS * IDX_DIM), lambda i: (i, PB_QI // (IDX_HEADS * IDX_DIM))),
                  pl.BlockSpec((s, LANES), lambda i: (0, PB_KK // LANES)),
                  pl.BlockSpec((tq, LANES), lambda i: (i, PF_WI // LANES))],
        out_specs=pl.BlockSpec((tq, s), lambda i: (i, 0)),
        out_shape=jax.ShapeDtypeStruct((s, s), BF16),
        scratch_shapes=[pltpu.VMEM((IDX_HEADS * tq, LANES), BF16),
                        pltpu.VMEM((IDX_HEADS * tq, 1), F32),
                        pltpu.VMEM((nk, tq, tk), I32)],
        compiler_params=_cparams("arbitrary"),
        name="dsa_index_mask",
    )(pb, pb, pf)


def _dsa_attn_kernel(q_ref, k_ref, v_ref, mask_ref, o_ref, m_ref, l_ref, acc_ref, *, tq, tk):
    i = pl.program_id(0)
    j = pl.program_id(1)
    slopes = _alibi_slopes(A_HEADS)
    scale = HEAD_DIM ** -0.5

    @pl.when(j == 0)
    def _():
        m_ref[...] = jnp.full(m_ref.shape, NEG_INF, F32)
        l_ref[...] = jnp.zeros(l_ref.shape, F32)
        acc_ref[...] = jnp.zeros(acc_ref.shape, F32)

    @pl.when(j * tk < (i + 1) * tq)
    def _():
        sel = mask_ref[...] > 0
        kpos = (j * tk - i * tq + lax.broadcasted_iota(I32, (1, tk), 1)).astype(F32)
        for h in range(A_HEADS):
            hs = slice(h * HEAD_DIM, (h + 1) * HEAD_DIM)
            s = lax.dot_general(q_ref[:, hs], k_ref[:, hs], (((1,), (1,)), ((), ())),
                                preferred_element_type=F32)
            s = s * scale + slopes[h] * kpos
            s = jnp.where(sel, s, NEG_INF)
            m_prev = m_ref[h]
            m_cur = jnp.maximum(m_prev, jnp.max(s, axis=1, keepdims=True))
            m_safe = jnp.where(m_cur == NEG_INF, 0.0, m_cur)
            p = jnp.exp(s - m_safe)
            a = jnp.exp(m_prev - m_safe)
            l_ref[h] = a * l_ref[h] + jnp.sum(p, axis=1, keepdims=True)
            acc_ref[:, hs] = a * acc_ref[:, hs] + jnp.dot(p.astype(BF16), v_ref[:, hs],
                                                         preferred_element_type=F32)
            m_ref[h] = m_cur

    @pl.when(j == pl.num_programs(1) - 1)
    def _():
        for h in range(A_HEADS):
            hs = slice(h * HEAD_DIM, (h + 1) * HEAD_DIM)
            o_ref[:, hs] = (acc_ref[:, hs] / l_ref[h]).astype(o_ref.dtype)


def _last_kv(i, tq, tk):
    return ((i + 1) * tq - 1) // tk


def _dsa_attention(pb, mask, tq, tk):
    s = pb.shape[0]
    kern = functools.partial(_dsa_attn_kernel, tq=tq, tk=tk)
    kv = lambda i, j: jnp.minimum(j, _last_kv(i, tq, tk))
    return pl.pallas_call(
        kern,
        grid=(s // tq, s // tk),
        in_specs=[pl.BlockSpec((tq, A_WIDTH), lambda i, j: (i, PB_QA // A_WIDTH)),
                  pl.BlockSpec((tk, A_WIDTH), lambda i, j: (kv(i, j), PB_KA // A_WIDTH)),
                  pl.BlockSpec((tk, A_WIDTH), lambda i, j: (kv(i, j), PB_VA // A_WIDTH)),
                  pl.BlockSpec((tq, tk), lambda i, j: (i, kv(i, j)))],
        out_specs=pl.BlockSpec((tq, A_WIDTH), lambda i, j: (i, 0)),
        out_shape=jax.ShapeDtypeStruct((s, A_WIDTH), BF16),
        scratch_shapes=[pltpu.VMEM((A_HEADS, tq, 1), F32),
                        pltpu.VMEM((A_HEADS, tq, 1), F32),
                        pltpu.VMEM((tq, A_WIDTH), F32)],
        compiler_params=_cparams("arbitrary", "arbitrary"),
        name="dsa_attention",
    )(pb, pb, pb, mask)


def _diff_attn_kernel(q_ref, k_ref, v_ref, lam_ref, g_ref, o_ref, m_ref, l_ref, acc_ref,
                      *, tq, tk, lam_init):
    i = pl.program_id(0)
    j = pl.program_id(1)
    slopes = _alibi_slopes(C_HEADS)
    scale = C_QK ** -0.5

    @pl.when(j == 0)
    def _():
        m_ref[...] = jnp.full(m_ref.shape, NEG_INF, F32)
        l_ref[...] = jnp.zeros(l_ref.shape, F32)
        acc_ref[...] = jnp.zeros(acc_ref.shape, F32)

    @pl.when(j * tk < (i + 1) * tq)
    def _():
        lane = lax.broadcasted_iota(I32, (tq, LANES), 1)
        half = [jnp.where(lane < C_QK, scale, 0.0).astype(F32),
                jnp.where(lane < C_QK, 0.0, scale).astype(F32)]
        rel = j * tk - i * tq + lax.broadcasted_iota(I32, (tq, tk), 1) - lax.broadcasted_iota(I32, (tq, tk), 0)
        causal = rel <= 0
        kpos = (j * tk - i * tq + lax.broadcasted_iota(I32, (1, tk), 1)).astype(F32)
        for h in range(C_HEADS):
            hs = slice(h * LANES, (h + 1) * LANES)
            qf = q_ref[:, hs].astype(F32)
            kh = k_ref[:, hs]
            vh = v_ref[:, hs]
            for mp in range(2):
                idx = 2 * h + mp
                s = lax.dot_general((qf * half[mp]).astype(BF16), kh, (((1,), (1,)), ((), ())),
                                    preferred_element_type=F32)
                s = jnp.where(causal, s + slopes[h] * kpos, NEG_INF)
                m_prev = m_ref[idx]
                m_cur = jnp.maximum(m_prev, jnp.max(s, axis=1, keepdims=True))
                p = jnp.exp(s - m_cur)
                a = jnp.exp(m_prev - m_cur)
                l_ref[idx] = a * l_ref[idx] + jnp.sum(p, axis=1, keepdims=True)
                acc_ref[mp, :, hs] = a * acc_ref[mp, :, hs] + jnp.dot(p.astype(BF16), vh,
                                                                     preferred_element_type=F32)
                m_ref[idx] = m_cur

    @pl.when(j == pl.num_programs(1) - 1)
    def _():
        lp = lam_ref[...]
        lam = (jnp.exp(jnp.sum(lp[0:1] * lp[1:2], axis=1, keepdims=True))
               - jnp.exp(jnp.sum(lp[2:3] * lp[3:4], axis=1, keepdims=True)) + lam_init)
        for h in range(C_HEADS):
            hs = slice(h * LANES, (h + 1) * LANES)
            o = acc_ref[0, :, hs] / l_ref[2 * h] - lam * (acc_ref[1, :, hs] / l_ref[2 * h + 1])
            ms = jnp.mean(o * o, axis=1, keepdims=True)
            o_ref[:, hs] = (o * lax.rsqrt(ms + EPS) * g_ref[:, hs] * (1.0 - lam_init)).astype(o_ref.dtype)


def _diff_attention(pb, lam_params, g, lam_init, tq, tk):
    s = pb.shape[0]
    kern = functools.partial(_diff_attn_kernel, tq=tq, tk=tk, lam_init=lam_init)
    kv = lambda i, j: jnp.minimum(j, _last_kv(i, tq, tk))
    return pl.pallas_call(
        kern,
        grid=(s // tq, s // tk),
        in_specs=[pl.BlockSpec((tq, C_WIDTH), lambda i, j: (i, PB_QC // C_WIDTH)),
                  pl.BlockSpec((tk, C_WIDTH), lambda i, j: (kv(i, j), PB_KC // C_WIDTH)),
                  pl.BlockSpec((tk, C_WIDTH), lambda i, j: (kv(i, j), PB_VC // C_WIDTH)),
                  pl.BlockSpec((4, C_QK), lambda i, j: (0, 0)),
                  pl.BlockSpec((1, C_WIDTH), lambda i, j: (0, 0))],
        out_specs=pl.BlockSpec((tq, C_WIDTH), lambda i, j: (i, 0)),
        out_shape=jax.ShapeDtypeStruct((s, C_WIDTH), BF16),
        scratch_shapes=[pltpu.VMEM((2 * C_HEADS, tq, 1), F32),
                        pltpu.VMEM((2 * C_HEADS, tq, 1), F32),
                        pltpu.VMEM((2, tq, C_WIDTH), F32)],
        compiler_params=_cparams("arbitrary", "arbitrary"),
        name="diff_attention",
    )(pb, pb, pb, lam_params, g)


def _split3(x):
    x1 = x.astype(BF16)
    r1 = x - x1.astype(F32)
    x2 = r1.astype(BF16)
    x3 = (r1 - x2.astype(F32)).astype(BF16)
    return x1, x2, x3


def _mlstm_kernel(q_ref, k_ref, qh_ref, kh_ref, v_ref, o_ref, gi_ref, gf_ref, cw_ref, bi_ref, bf_ref,
                  g_ref, y_ref, xq_ref, xk_ref, c_ref, n_ref, m_ref, *, L):
    c = pl.program_id(0)

    @pl.when(c == 0)
    def _():
        c_ref[...] = jnp.zeros(c_ref.shape, F32)
        n_ref[...] = jnp.zeros(n_ref.shape, F32)
        m_ref[...] = jnp.zeros(m_ref.shape, F32)

    first = (c > 0).astype(F32)
    xq_ref[0:SUBLANES, :] = qh_ref[...] * first
    xq_ref[SUBLANES:SUBLANES + L, :] = q_ref[...]
    xk_ref[0:SUBLANES, :] = kh_ref[...] * first
    xk_ref[SUBLANES:SUBLANES + L, :] = k_ref[...]
    qc = jnp.zeros((L, M_WIDTH), F32)
    kc = jnp.zeros((L, M_WIDTH), F32)
    for t in range(CONV_W):
        off = SUBLANES - (CONV_W - 1) + t
        qc = qc + xq_ref[off:off + L, :] * cw_ref[t:t + 1, 0:M_WIDTH]
        kc = kc + xk_ref[off:off + L, :] * cw_ref[t:t + 1, M_WIDTH:2 * M_WIDTH]
    qc = qc * jax.nn.sigmoid(qc)
    kc = kc * jax.nn.sigmoid(kc) * (HEAD_DIM ** -0.5)

    li = gi_ref[...] + bi_ref[...]
    fp = gf_ref[...] + bf_ref[...]
    lf = jnp.minimum(fp, 0.0) - jnp.log(1.0 + jnp.exp(-jnp.abs(fp)))
    r_i = lax.broadcasted_iota(I32, (L, L), 0)
    c_i = lax.broadcasted_iota(I32, (L, L), 1)
    tril = c_i <= r_i
    tril_b = jnp.where(tril, 1.0, 0.0).astype(BF16)
    a = jnp.zeros((L, LANES), F32)
    for part in _split3(lf):
        a = a + jnp.dot(tril_b, part, preferred_element_type=F32)
    b = li - a
    g_tot = a[L - 1:L, :]
    b_t = b.T

    m_all = m_ref[...]
    m_new_all = m_all
    for h in range(M_HEADS):
        hs = slice(h * HEAD_DIM, (h + 1) * HEAD_DIM)
        qh = qc[:, hs]
        kh = kc[:, hs]
        vh = v_ref[:, hs]
        qb = qh.astype(BF16)
        kb = kh.astype(BF16)
        m_prev = m_all[:, h:h + 1]
        d = jnp.where(tril, b_t[h:h + 1, :], NEG_INF)
        mm = jnp.maximum(m_prev, jnp.max(d, axis=1, keepdims=True))
        w_intra = jnp.exp(d - mm)
        w_inter = jnp.exp(m_prev - mm)
        qk = lax.dot_general(qb, kb, (((1,), (1,)), ((), ())), preferred_element_type=F32) * w_intra
        c_h = c_ref[h]
        num = (jnp.dot(qk.astype(BF16), vh, preferred_element_type=F32)
               + w_inter * jnp.dot(qb, c_h.astype(BF16), preferred_element_type=F32))
        den = (jnp.sum(qk, axis=1, keepdims=True)
               + w_inter * jnp.sum(qh * n_ref[h:h + 1, :], axis=1, keepdims=True))
        m_row = a[:, h:h + 1] + mm
        hh = num / jnp.maximum(jnp.abs(den), jnp.exp(-m_row))
        mm_last = mm[L - 1:L, :]
        ws = jnp.exp(b[:, h:h + 1] - mm_last)
        wc = jnp.exp(m_prev - mm_last)
        c_ref[h] = wc * c_h + lax.dot_general(kb, (ws * vh.astype(F32)).astype(BF16),
                                              (((0,), (0,)), ((), ())), preferred_element_type=F32)
        n_ref[h:h + 1, :] = wc * n_ref[h:h + 1, :] + jnp.sum(ws * kh, axis=0, keepdims=True)
        lane = lax.broadcasted_iota(I32, (1, LANES), 1)
        m_new_all = jnp.where(lane == h, g_tot + mm_last, m_new_all)
        ms = jnp.mean(hh * hh, axis=1, keepdims=True)
        hn = hh * lax.rsqrt(ms + EPS) * g_ref[:, hs]
        y_ref[:, hs] = (jax.nn.sigmoid(o_ref[:, hs]) * hn).astype(y_ref.dtype)
    m_ref[...] = m_new_all


def _mlstm(pb, pf, conv_w, b_i, b_f, g, L):
    s = pb.shape[0]
    kern = functools.partial(_mlstm_kernel, L=L)
    halo = lambda c: jnp.maximum(c * (L // SUBLANES) - 1, 0)
    return pl.pallas_call(
        kern,
        grid=(s // L,),
        in_specs=[pl.BlockSpec((L, M_WIDTH), lambda c: (c, PF_QM // M_WIDTH)),
                  pl.BlockSpec((L, M_WIDTH), lambda c: (c, PF_KM // M_WIDTH)),
                  pl.BlockSpec((SUBLANES, M_WIDTH), lambda c: (halo(c), PF_QM // M_WIDTH)),
                  pl.BlockSpec((SUBLANES, M_WIDTH), lambda c: (halo(c), PF_KM // M_WIDTH)),
                  pl.BlockSpec((L, M_WIDTH), lambda c: (c, PB_VM // M_WIDTH)),
                  pl.BlockSpec((L, M_WIDTH), lambda c: (c, PF_OM // M_WIDTH)),
                  pl.BlockSpec((L, LANES), lambda c: (c, PF_GI // LANES)),
                  pl.BlockSpec((L, LANES), lambda c: (c, PF_GF // LANES)),
                  pl.BlockSpec((CONV_W, 2 * M_WIDTH), lambda c: (0, 0)),
                  pl.BlockSpec((1, LANES), lambda c: (0, 0)),
                  pl.BlockSpec((1, LANES), lambda c: (0, 0)),
                  pl.BlockSpec((1, M_WIDTH), lambda c: (0, 0))],
        out_specs=pl.BlockSpec((L, M_WIDTH), lambda c: (c, 0)),
        out_shape=jax.ShapeDtypeStruct((s, M_WIDTH), BF16),
        scratch_shapes=[pltpu.VMEM((L + SUBLANES, M_WIDTH), F32),
                        pltpu.VMEM((L + SUBLANES, M_WIDTH), F32),
                        pltpu.VMEM((M_HEADS, HEAD_DIM, HEAD_DIM), F32),
                        pltpu.VMEM((SUBLANES, HEAD_DIM), F32),
                        pltpu.VMEM((1, LANES), F32)],
        compiler_params=_cparams("arbitrary"),
        name="mlstm",
    )(pf, pf, pf, pf, pb, pf, pf, pf, conv_w, b_i, b_f, g)


def _layer_norm(z, g, b):
    mu = jnp.mean(z, axis=-1, keepdims=True)
    zc = z - mu
    var = jnp.mean(zc * zc, axis=-1, keepdims=True)
    return zc * lax.rsqrt(var + EPS) * g + b


def _out_kernel(ya_ref, ym_ref, yc_ref, w_ref, x_ref, g_ref, b_ref, o_ref):
    acc = jnp.dot(ya_ref[...], w_ref[0:A_WIDTH, :], preferred_element_type=F32)
    acc = acc + jnp.dot(ym_ref[...], w_ref[A_WIDTH:A_WIDTH + M_WIDTH, :], preferred_element_type=F32)
    acc = acc + jnp.dot(yc_ref[...], w_ref[A_WIDTH + M_WIDTH:D_MODEL, :], preferred_element_type=F32)
    o_ref[...] = _layer_norm(ALPHA * x_ref[...] + acc, g_ref[...], b_ref[...])


def _out_proj(ya, ym, yc, w, x, g, b, tm):
    s = x.shape[0]
    return pl.pallas_call(
        _out_kernel,
        grid=(s // tm,),
        in_specs=[pl.BlockSpec((tm, A_WIDTH), lambda i: (i, 0)),
                  pl.BlockSpec((tm, M_WIDTH), lambda i: (i, 0)),
                  pl.BlockSpec((tm, C_WIDTH), lambda i: (i, 0)),
                  pl.BlockSpec((D_MODEL, D_MODEL), lambda i: (0, 0)),
                  pl.BlockSpec((tm, D_MODEL), lambda i: (i, 0)),
                  pl.BlockSpec((1, D_MODEL), lambda i: (0, 0)),
                  pl.BlockSpec((1, D_MODEL), lambda i: (0, 0))],
        out_specs=pl.BlockSpec((tm, D_MODEL), lambda i: (i, 0)),
        out_shape=jax.ShapeDtypeStruct((s, D_MODEL), F32),
        compiler_params=_cparams("arbitrary"),
        name="out_proj_ln",
    )(ya, ym, yc, w, x, g, b)


def _ffn_kernel(x_ref, wu_ref, wd_ref, g_ref, b_ref, o_ref, xb_ref, acc_ref):
    f = pl.program_id(1)

    @pl.when(f == 0)
    def _():
        xb_ref[...] = x_ref[...].astype(BF16)
        acc_ref[...] = jnp.zeros(acc_ref.shape, F32)

    hdn = jnp.maximum(jnp.dot(xb_ref[...], wu_ref[...], preferred_element_type=F32), 0.0)
    acc_ref[...] += jnp.dot((hdn * hdn).astype(BF16), wd_ref[...], preferred_element_type=F32)

    @pl.when(f == pl.num_programs(1) - 1)
    def _():
        o_ref[...] = _layer_norm(ALPHA * x_ref[...] + acc_ref[...], g_ref[...], b_ref[...])


def _ffn(x, wu, wd, g, b, tm, tf):
    s = x.shape[0]
    return pl.pallas_call(
        _ffn_kernel,
        grid=(s // tm, D_FF // tf),
        in_specs=[pl.BlockSpec((tm, D_MODEL), lambda i, f: (i, 0)),
                  pl.BlockSpec((D_MODEL, tf), lambda i, f: (0, f)),
                  pl.BlockSpec((tf, D_MODEL), lambda i, f: (f, 0)),
                  pl.BlockSpec((1, D_MODEL), lambda i, f: (0, 0)),
                  pl.BlockSpec((1, D_MODEL), lambda i, f: (0, 0))],
        out_specs=pl.BlockSpec((tm, D_MODEL), lambda i, f: (i, 0)),
        out_shape=jax.ShapeDtypeStruct((s, D_MODEL), F32),
        scratch_shapes=[pltpu.VMEM((tm, D_MODEL), BF16),
                        pltpu.VMEM((tm, D_MODEL), F32)],
        compiler_params=_cparams("arbitrary", "arbitrary"),
        name="ffn_ln",
    )(x, wu, wd, g, b)


def _seg(w, i):
    return w[..., _OFFS[i]:_OFFS[i] + SIZES[i]]


def _pad_cols(w, width):
    return jnp.pad(w, [(0, 0)] * (w.ndim - 1) + [(0, width - w.shape[-1])])


def _layout_w_in(w_in):
    (q_a, k_a, v_a, q_i, k_i, w_i, q_m, k_m, v_m, o_m, i_m, f_m, q_c, k_c, v_c) = [
        _seg(w_in, i) for i in range(len(SIZES))]
    wb = jnp.concatenate([q_a, k_a, v_a, v_m, q_c, k_c, v_c, q_i, k_i, k_i], axis=-1)
    wb = _pad_cols(wb, PB_WIDTH).astype(BF16)
    wf = jnp.concatenate([q_m, k_m, o_m, _pad_cols(w_i, LANES), _pad_cols(i_m, LANES),
                          _pad_cols(f_m, LANES)], axis=-1)
    wf = _pad_cols(wf, PF_WIDTH).astype(BF16)
    return wb, wf


def _tile(n, pref):
    t = min(n, pref)
    assert n % t == 0, (n, t)
    return t


def kernel(x, w_in, conv_m, b_i, b_f, m_norm_g, lam_q1, lam_k1, lam_q2, lam_k2, c_norm_g, w_out,
           ln1_g, ln1_b, w_up, w_down, ln2_g, ln2_b):
    batch, s, d = x.shape
    assert batch == 1 and d == D_MODEL
    ksel = min(TOPK_MAX, s // 4)
    wb_all, wf_all = _layout_w_in(w_in)
    w_out_b = w_out.astype(BF16)
    w_up_b = w_up.astype(BF16)
    w_down_b = w_down.astype(BF16)
    conv_w = conv_m.reshape(DEPTH, CONV_W, 2 * M_WIDTH)
    b_i_p = _pad_cols(b_i, LANES).reshape(DEPTH, 1, LANES)
    b_f_p = _pad_cols(b_f, LANES).reshape(DEPTH, 1, LANES)
    lam_p = jnp.stack([lam_q1, lam_k1, lam_q2, lam_k2], axis=1)

    tm_proj = _tile(s, 1024)
    t_attn = _tile(s, 512)
    tq_idx = _tile(s, 128)
    tk_idx = _tile(s, 512)
    l_chunk = _tile(s, 256)
    tm_out = _tile(s, 512)
    tm_ffn = _tile(s, 512)

    h = x.reshape(s, d)
    for l in range(DEPTH):
        pb = _matmul(h, wb_all[l], BF16, tm_proj, 768, "proj_bf16")
        pf = _matmul(h, wf_all[l], F32, tm_proj, 768, "proj_f32")
        mask = _dsa_mask(pb, pf, ksel, tq_idx, tk_idx)
        y_a = _dsa_attention(pb, mask, t_attn, t_attn)
        y_m = _mlstm(pb, pf, conv_w[l], b_i_p[l], b_f_p[l], m_norm_g[l].reshape(1, M_WIDTH), l_chunk)
        lam_init = 0.8 - 0.6 * math.exp(-0.3 * l)
        y_c = _diff_attention(pb, lam_p[l], c_norm_g[l].reshape(1, C_WIDTH), lam_init, t_attn, t_attn)
        h = _out_proj(y_a, y_m, y_c, w_out_b[l], h, ln1_g[l].reshape(1, d), ln1_b[l].reshape(1, d), tm_out)
        h = _ffn(h, w_up_b[l], w_down_b[l], ln2_g[l].reshape(1, d), ln2_b[l].reshape(1, d), tm_ffn, 1024)
    return h.reshape(batch, s, d)
```

```python
import functools
import math

import jax
import jax.numpy as jnp
from jax import lax
from jax.experimental import pallas as pl
from jax.experimental.pallas import tpu as pltpu

F32 = jnp.float32
BF16 = jnp.bfloat16
I32 = jnp.int32

D_MODEL = 2048
DEPTH = 4
HEAD_DIM = 128
A_HEADS = 6
A_WIDTH = A_HEADS * HEAD_DIM
IDX_HEADS = 8
IDX_DIM = 64
TOPK_MAX = 256
M_HEADS = 6
M_WIDTH = M_HEADS * HEAD_DIM
CONV_W = 4
C_HEADS = 4
C_QK = 64
C_WIDTH = C_HEADS * 2 * C_QK
D_FF = 4 * D_MODEL
ALPHA = (2.0 * DEPTH) ** 0.25
EPS = 1e-5

SIZES = (A_WIDTH, A_WIDTH, A_WIDTH, IDX_HEADS * IDX_DIM, IDX_DIM, IDX_HEADS,
         M_WIDTH, M_WIDTH, M_WIDTH, M_WIDTH, M_HEADS, M_HEADS,
         C_WIDTH, C_WIDTH, C_WIDTH)
_OFFS = tuple(int(sum(SIZES[:i])) for i in range(len(SIZES)))

LANES = 128
SUBLANES = 8
VMEM_LIMIT_BYTES = 56 * 1024 * 1024
INT_MIN = -2 ** 31
NEG_INF = float("-inf")
LOG2E = math.log2(math.e)

PB_QA, PB_KA, PB_VA, PB_VM = 0, 768, 1536, 2304
PB_QC, PB_KC, PB_VC, PB_QI = 3072, 3584, 4096, 4608
PB_KK = 5120
PB_WIDTH = 5376
PF_QM, PF_KM, PF_OM = 0, 768, 1536
PF_WI, PF_GI, PF_GF = 2304, 2432, 2560
PF_WIDTH = 3072


def _cparams(*sem):
    return pltpu.CompilerParams(dimension_semantics=sem, vmem_limit_bytes=VMEM_LIMIT_BYTES)


def _alibi_slopes(n):
    return [2.0 ** (-8.0 * (h + 1) / n) for h in range(n)]


def _mm_kernel(x_ref, w_ref, o_ref, xb_ref):
    @pl.when(pl.program_id(1) == 0)
    def _():
        xb_ref[...] = x_ref[...].astype(BF16)

    o_ref[...] = jnp.dot(xb_ref[...], w_ref[...], preferred_element_type=F32).astype(o_ref.dtype)


def _matmul(x, w, out_dtype, tm, tn, name):
    m, k = x.shape
    n = w.shape[1]
    return pl.pallas_call(
        _mm_kernel,
        grid=(m // tm, n // tn),
        in_specs=[pl.BlockSpec((tm, k), lambda i, j: (i, 0)),
                  pl.BlockSpec((k, tn), lambda i, j: (0, j))],
        out_specs=pl.BlockSpec((tm, tn), lambda i, j: (i, j)),
        out_shape=jax.ShapeDtypeStruct((m, n), out_dtype),
        scratch_shapes=[pltpu.VMEM((tm, k), BF16)],
        compiler_params=_cparams("arbitrary", "arbitrary"),
        name=name,
    )(x, w)


def _idx_kernel(qi_ref, kk_ref, wi_ref, mask_ref, q8_ref, w8_ref, keys_ref, *, tq, tk, nk, ksel, jbits):
    i = pl.program_id(0)
    nck = ((i + 1) * tq + tk - 1) // tk

    lane = lax.broadcasted_iota(I32, (tq, LANES), 1)
    lo = jnp.where(lane < IDX_DIM, 1.0, 0.0).astype(F32)
    hi = 1.0 - lo
    for h in range(IDX_HEADS):
        qp = qi_ref[:, (h // 2) * LANES:(h // 2 + 1) * LANES].astype(F32)
        q8_ref[h * tq:(h + 1) * tq, :] = (qp * (lo if h % 2 == 0 else hi)).astype(BF16)
        w8_ref[h * tq:(h + 1) * tq, :] = wi_ref[:, h:h + 1]

    row = i * tq + lax.broadcasted_iota(I32, (tq, tk), 0)
    col0 = lax.broadcasted_iota(I32, (tq, tk), 1)

    def score_chunk(c, carry):
        kc = kk_ref[pl.ds(pl.multiple_of(c * tk, tk), tk), :]
        y = lax.dot_general(q8_ref[...], kc, (((1,), (1,)), ((), ())), preferred_element_type=F32)
        z = jnp.maximum(y, 0.0) * w8_ref[...]
        sc = z[0:tq]
        for h in range(1, IDX_HEADS):
            sc = sc + z[h * tq:(h + 1) * tq]
        bits = lax.bitcast_convert_type(sc, I32)
        key = jnp.where(bits < 0, bits ^ jnp.int32(0x7FFFFFFF), bits)
        key = jnp.where(bits == jnp.int32(INT_MIN), 0, key)
        key = jnp.where(c * tk + col0 <= row, key, jnp.int32(INT_MIN))
        keys_ref[c] = key
        return carry

    lax.fori_loop(0, nck, score_chunk, 0)

    def lane_fold(m):
        part = m[:, 0:LANES]
        for g in range(1, tk // LANES):
            part = part + m[:, g * LANES:(g + 1) * LANES]
        return part

    def count(pred):
        def body(c, acc):
            return acc + lane_fold(jnp.where(pred(keys_ref[c], c), 1.0, 0.0).astype(F32))
        acc = lax.fori_loop(0, nck, body, jnp.zeros((tq, LANES), F32))
        return jnp.sum(acc, axis=1, keepdims=True)

    def bit_body(b, u):
        cand_u = u | lax.shift_left(jnp.int32(1), 31 - b)
        cand = cand_u ^ jnp.int32(INT_MIN)
        cnt = count(lambda k, c: k >= cand)
        return jnp.where(cnt >= ksel, cand_u, u)

    u = lax.fori_loop(0, 32, bit_body, jnp.zeros((tq, 1), I32))
    thr = u ^ jnp.int32(INT_MIN)
    n_gt = count(lambda k, c: k > thr)
    n_ge = count(lambda k, c: k >= thr)
    need = ksel - n_gt
    excess = jnp.logical_and(thr > jnp.int32(INT_MIN), n_ge > ksel)
    any_excess = jnp.max(jnp.where(excess, 1.0, 0.0)) > 0.0
    j_all = jnp.full((tq, 1), 2 ** jbits - 1, I32)

    def tie_search():
        def jbody(b, jc):
            cand = jc + lax.shift_left(jnp.int32(1), jbits - 1 - b)
            cnt = count(lambda k, c: jnp.logical_and(k == thr, c * tk + col0 < cand))
            return jnp.where(cnt <= need, cand, jc)
        return lax.fori_loop(0, jbits, jbody, jnp.zeros((tq, 1), I32))

    jcut = lax.cond(any_excess, tie_search, lambda: j_all)

    for c in range(nk):
        @pl.when(c < nck)
        def _():
            key = keys_ref[c]
            tie = jnp.logical_and(key == thr, c * tk + col0 < jcut)
            sel = jnp.logical_and(key > jnp.int32(INT_MIN), jnp.logical_or(key > thr, tie))
            mask_ref[:, c * tk:(c + 1) * tk] = jnp.where(sel, 1.0, 0.0).astype(BF16)

        @pl.when(c >= nck)
        def _():
            mask_ref[:, c * tk:(c + 1) * tk] = jnp.zeros((tq, tk), BF16)


def _dsa_mask(pb, pf, ksel, tq, tk):
    s = pb.shape[0]
    nk = s // tk
    jbits = int(s).bit_length()
    kern = functools.partial(_idx_kernel, tq=tq, tk=tk, nk=nk, ksel=ksel, jbits=jbits)
    return pl.pallas_call(
        kern,
        grid=(s // tq,),
        in_specs=[pl.BlockSpec((tq, IDX_HEADS * IDX_DIM), lambda i: (i, PB_QI // (IDX_HEADS * IDX_DIM))),
                  pl.BlockSpec((s, LANES), lambda i: (0, PB_KK // LANES)),
                  pl.BlockSpec((tq, LANES), lambda i: (i, PF_WI // LANES))],
        out_specs=pl.BlockSpec((tq, s), lambda i: (i, 0)),
        out_shape=jax.ShapeDtypeStruct((s, s), BF16),
        scratch_shapes=[pltpu.VMEM((IDX_HEADS * tq, LANES), BF16),
                        pltpu.VMEM((IDX_HEADS * tq, 1), F32),
                        pltpu.VMEM((nk, tq, tk), I32)],
        compiler_params=_cparams("arbitrary"),
        name="dsa_index_mask",
    )(pb, pb, pf)


def _dsa_attn_kernel(q_ref, k_ref, v_ref, mask_ref, o_ref, qs_ref, mb_ref, s_ref, m_ref, l_ref, acc_ref,
                     *, tq, tk):
    i = pl.program_id(0)
    j = pl.program_id(1)
    slopes = _alibi_slopes(A_HEADS)
    reps = tk // LANES

    @pl.when(j == 0)
    def _():
        m_ref[...] = jnp.full(m_ref.shape, NEG_INF, F32)
        l_ref[...] = jnp.zeros(l_ref.shape, F32)
        acc_ref[...] = jnp.zeros(acc_ref.shape, F32)
        qs_ref[...] = (q_ref[...].astype(F32) * (HEAD_DIM ** -0.5 * LOG2E)).astype(BF16)

    @pl.when(j * tk < (i + 1) * tq)
    def _():
        mf = mask_ref[...].astype(F32)
        mb_ref[...] = jnp.where(mf > 0.0, mf - 1.0, NEG_INF)
        kpos = (j * tk - i * tq + lax.broadcasted_iota(I32, (1, tk), 1)).astype(F32)

        def scores(h):
            hs = slice(h * HEAD_DIM, (h + 1) * HEAD_DIM)
            s = lax.dot_general(qs_ref[:, hs], k_ref[:, hs], (((1,), (1,)), ((), ())),
                                preferred_element_type=F32)
            s = s + (mb_ref[...] + (slopes[h] * LOG2E) * kpos)
            s_ref[h % 2] = s
            m_prev = m_ref[h]
            m_cur = jnp.maximum(m_prev, jnp.max(s, axis=1, keepdims=True))
            m_ref[h] = m_cur
            m_safe = jnp.where(m_cur == NEG_INF, 0.0, m_cur)
            return m_safe, jnp.exp2(m_prev - m_safe)

        def accumulate(h, m_safe, a):
            hs = slice(h * HEAD_DIM, (h + 1) * HEAD_DIM)
            p = jnp.exp2(s_ref[h % 2] - pltpu.repeat(m_safe, reps, axis=1))
            l_ref[h] = a * l_ref[h] + jnp.sum(p, axis=1, keepdims=True)
            acc_ref[:, hs] = a * acc_ref[:, hs] + jnp.dot(p.astype(BF16), v_ref[:, hs],
                                                         preferred_element_type=F32)

        stats = scores(0)
        for h in range(A_HEADS):
            nxt = scores(h + 1) if h + 1 < A_HEADS else None
            accumulate(h, *stats)
            stats = nxt

    @pl.when(j == pl.num_programs(1) - 1)
    def _():
        for h in range(A_HEADS):
            hs = slice(h * HEAD_DIM, (h + 1) * HEAD_DIM)
            o_ref[:, hs] = (acc_ref[:, hs] / l_ref[h]).astype(o_ref.dtype)


def _last_kv(i, tq, tk):
    return ((i + 1) * tq - 1) // tk


def _dsa_attention(pb, mask, tq, tk):
    s = pb.shape[0]
    kern = functools.partial(_dsa_attn_kernel, tq=tq, tk=tk)
    kv = lambda i, j: jnp.minimum(j, _last_kv(i, tq, tk))
    return pl.pallas_call(
        kern,
        grid=(s // tq, s // tk),
        in_specs=[pl.BlockSpec((tq, A_WIDTH), lambda i, j: (i, PB_QA // A_WIDTH)),
                  pl.BlockSpec((tk, A_WIDTH), lambda i, j: (kv(i, j), PB_KA // A_WIDTH)),
                  pl.BlockSpec((tk, A_WIDTH), lambda i, j: (kv(i, j), PB_VA // A_WIDTH)),
                  pl.BlockSpec((tq, tk), lambda i, j: (i, kv(i, j)))],
        out_specs=pl.BlockSpec((tq, A_WIDTH), lambda i, j: (i, 0)),
        out_shape=jax.ShapeDtypeStruct((s, A_WIDTH), BF16),
        scratch_shapes=[pltpu.VMEM((tq, A_WIDTH), BF16),
                        pltpu.VMEM((tq, tk), F32),
                        pltpu.VMEM((2, tq, tk), F32),
                        pltpu.VMEM((A_HEADS, tq, LANES), F32),
                        pltpu.VMEM((A_HEADS, tq, LANES), F32),
                        pltpu.VMEM((tq, A_WIDTH), F32)],
        compiler_params=_cparams("arbitrary", "arbitrary"),
        name="dsa_attention",
    )(pb, pb, pb, mask)


def _diff_attn_kernel(q_ref, k_ref, v_ref, lam_ref, g_ref, o_ref, qs_ref, s_ref, m_ref, l_ref, acc_ref,
                      *, tq, tk, lam_init):
    i = pl.program_id(0)
    j = pl.program_id(1)
    slopes = _alibi_slopes(C_HEADS)
    reps = tk // LANES

    @pl.when(j == 0)
    def _():
        m_ref[...] = jnp.full(m_ref.shape, NEG_INF, F32)
        l_ref[...] = jnp.zeros(l_ref.shape, F32)
        acc_ref[...] = jnp.zeros(acc_ref.shape, F32)
        lane = lax.broadcasted_iota(I32, (tq, C_WIDTH), 1) % LANES
        qf = q_ref[...].astype(F32) * (C_QK ** -0.5 * LOG2E)
        qs_ref[0] = jnp.where(lane < C_QK, qf, 0.0).astype(BF16)
        qs_ref[1] = jnp.where(lane < C_QK, 0.0, qf).astype(BF16)

    def step(diagonal):
        kpos = (j * tk - i * tq + lax.broadcasted_iota(I32, (1, tk), 1)).astype(F32)
        if diagonal:
            causal = lax.broadcasted_iota(I32, (tq, tk), 1) <= lax.broadcasted_iota(I32, (tq, tk), 0)

        def scores(u):
            h, mp = divmod(u, 2)
            hs = slice(h * LANES, (h + 1) * LANES)
            s = lax.dot_general(qs_ref[mp, :, hs], k_ref[:, hs], (((1,), (1,)), ((), ())),
                                preferred_element_type=F32)
            s = s + (slopes[h] * LOG2E) * kpos
            if diagonal:
                s = jnp.where(causal, s, NEG_INF)
            s_ref[u % 2] = s
            m_prev = m_ref[u]
            m_cur = jnp.maximum(m_prev, jnp.max(s, axis=1, keepdims=True))
            m_ref[u] = m_cur
            return m_cur, jnp.exp2(m_prev - m_cur)

        def accumulate(u, m_cur, a):
            h, mp = divmod(u, 2)
            hs = slice(h * LANES, (h + 1) * LANES)
            p = jnp.exp2(s_ref[u % 2] - pltpu.repeat(m_cur, reps, axis=1))
            l_ref[u] = a * l_ref[u] + jnp.sum(p, axis=1, keepdims=True)
            acc_ref[mp, :, hs] = a * acc_ref[mp, :, hs] + jnp.dot(p.astype(BF16), v_ref[:, hs],
                                                                 preferred_element_type=F32)

        stats = scores(0)
        for u in range(2 * C_HEADS):
            nxt = scores(u + 1) if u + 1 < 2 * C_HEADS else None
            accumulate(u, *stats)
            stats = nxt

    @pl.when(j < i)
    def _():
        step(False)

    @pl.when(j == i)
    def _():
        step(True)

    @pl.when(j == pl.num_programs(1) - 1)
    def _():
        lp = lam_ref[...]
        lam = (jnp.exp(jnp.sum(lp[0:1] * lp[1:2], axis=1, keepdims=True))
               - jnp.exp(jnp.sum(lp[2:3] * lp[3:4], axis=1, keepdims=True)) + lam_init)
        for h in range(C_HEADS):
            hs = slice(h * LANES, (h + 1) * LANES)
            o = acc_ref[0, :, hs] / l_ref[2 * h] - lam * (acc_ref[1, :, hs] / l_ref[2 * h + 1])
            ms = jnp.mean(o * o, axis=1, keepdims=True)
            o_ref[:, hs] = (o * lax.rsqrt(ms + EPS) * g_ref[:, hs] * (1.0 - lam_init)).astype(o_ref.dtype)


def _diff_attention(pb, lam_params, g, lam_init, tq, tk):
    s = pb.shape[0]
    assert tq == tk, "the kernel separates diagonal from off-diagonal blocks"
    kern = functools.partial(_diff_attn_kernel, tq=tq, tk=tk, lam_init=lam_init)
    kv = lambda i, j: jnp.minimum(j, _last_kv(i, tq, tk))
    return pl.pallas_call(
        kern,
        grid=(s // tq, s // tk),
        in_specs=[pl.BlockSpec((tq, C_WIDTH), lambda i, j: (i, PB_QC // C_WIDTH)),
                  pl.BlockSpec((tk, C_WIDTH), lambda i, j: (kv(i, j), PB_KC // C_WIDTH)),
                  pl.BlockSpec((tk, C_WIDTH), lambda i, j: (kv(i, j), PB_VC // C_WIDTH)),
                  pl.BlockSpec((4, C_QK), lambda i, j: (0, 0)),
                  pl.BlockSpec((1, C_WIDTH), lambda i, j: (0, 0))],
        out_specs=pl.BlockSpec((tq, C_WIDTH), lambda i, j: (i, 0)),
        out_shape=jax.ShapeDtypeStruct((s, C_WIDTH), BF16),
        scratch_shapes=[pltpu.VMEM((2, tq, C_WIDTH), BF16),
                        pltpu.VMEM((2, tq, tk), F32),
                        pltpu.VMEM((2 * C_HEADS, tq, LANES), F32),
                        pltpu.VMEM((2 * C_HEADS, tq, LANES), F32),
                        pltpu.VMEM((2, tq, C_WIDTH), F32)],
        compiler_params=_cparams("arbitrary", "arbitrary"),
        name="diff_attention",
    )(pb, pb, pb, lam_params, g)


def _split3(x):
    x1 = x.astype(BF16)
    r1 = x - x1.astype(F32)
    x2 = r1.astype(BF16)
    x3 = (r1 - x2.astype(F32)).astype(BF16)
    return x1, x2, x3


def _mlstm_kernel(q_ref, k_ref, qh_ref, kh_ref, v_ref, o_ref, gi_ref, gf_ref, cw_ref, bi_ref, bf_ref,
                  g_ref, y_ref, xq_ref, xk_ref, c_ref, n_ref, m_ref, *, L):
    c = pl.program_id(0)

    @pl.when(c == 0)
    def _():
        c_ref[...] = jnp.zeros(c_ref.shape, F32)
        n_ref[...] = jnp.zeros(n_ref.shape, F32)
        m_ref[...] = jnp.zeros(m_ref.shape, F32)

    first = (c > 0).astype(F32)
    xq_ref[0:SUBLANES, :] = qh_ref[...] * first
    xq_ref[SUBLANES:SUBLANES + L, :] = q_ref[...]
    xk_ref[0:SUBLANES, :] = kh_ref[...] * first
    xk_ref[SUBLANES:SUBLANES + L, :] = k_ref[...]
    qc = jnp.zeros((L, M_WIDTH), F32)
    kc = jnp.zeros((L, M_WIDTH), F32)
    for t in range(CONV_W):
        off = SUBLANES - (CONV_W - 1) + t
        qc = qc + xq_ref[off:off + L, :] * cw_ref[t:t + 1, 0:M_WIDTH]
        kc = kc + xk_ref[off:off + L, :] * cw_ref[t:t + 1, M_WIDTH:2 * M_WIDTH]
    qc = qc * jax.nn.sigmoid(qc)
    kc = kc * jax.nn.sigmoid(kc) * (HEAD_DIM ** -0.5)

    li = gi_ref[...] + bi_ref[...]
    fp = gf_ref[...] + bf_ref[...]
    lf = jnp.minimum(fp, 0.0) - jnp.log(1.0 + jnp.exp(-jnp.abs(fp)))
    r_i = lax.broadcasted_iota(I32, (L, L), 0)
    c_i = lax.broadcasted_iota(I32, (L, L), 1)
    tril = c_i <= r_i
    tril_b = jnp.where(tril, 1.0, 0.0).astype(BF16)
    a = jnp.zeros((L, LANES), F32)
    for part in _split3(lf):
        a = a + jnp.dot(tril_b, part, preferred_element_type=F32)
    b = li - a
    g_tot = a[L - 1:L, :]
    b_t = b.T

    m_all = m_ref[...]
    m_new_all = m_all
    for h in range(M_HEADS):
        hs = slice(h * HEAD_DIM, (h + 1) * HEAD_DIM)
        qh = qc[:, hs]
        kh = kc[:, hs]
        vh = v_ref[:, hs]
        qb = qh.astype(BF16)
        kb = kh.astype(BF16)
        m_prev = m_all[:, h:h + 1]
        d = jnp.where(tril, b_t[h:h + 1, :], NEG_INF)
        mm = jnp.maximum(m_prev, jnp.max(d, axis=1, keepdims=True))
        w_intra = jnp.exp(d - mm)
        w_inter = jnp.exp(m_prev - mm)
        qk = lax.dot_general(qb, kb, (((1,), (1,)), ((), ())), preferred_element_type=F32) * w_intra
        c_h = c_ref[h]
        num = (jnp.dot(qk.astype(BF16), vh, preferred_element_type=F32)
               + w_inter * jnp.dot(qb, c_h.astype(BF16), preferred_element_type=F32))
        den = (jnp.sum(qk, axis=1, keepdims=True)
               + w_inter * jnp.sum(qh * n_ref[h:h + 1, :], axis=1, keepdims=True))
        m_row = a[:, h:h + 1] + mm
        hh = num / jnp.maximum(jnp.abs(den), jnp.exp(-m_row))
        mm_last = mm[L - 1:L, :]
        ws = jnp.exp(b[:, h:h + 1] - mm_last)
        wc = jnp.exp(m_prev - mm_last)
        c_ref[h] = wc * c_h + lax.dot_general(kb, (ws * vh.astype(F32)).astype(BF16),
                                              (((0,), (0,)), ((), ())), preferred_element_type=F32)
        n_ref[h:h + 1, :] = wc * n_ref[h:h + 1, :] + jnp.sum(ws * kh, axis=0, keepdims=True)
        lane = lax.broadcasted_iota(I32, (1, LANES), 1)
        m_new_all = jnp.where(lane == h, g_tot + mm_last, m_new_all)
        ms = jnp.mean(hh * hh, axis=1, keepdims=True)
        hn = hh * lax.rsqrt(ms + EPS) * g_ref[:, hs]
        y_ref[:, hs] = (jax.nn.sigmoid(o_ref[:, hs]) * hn).astype(y_ref.dtype)
    m_ref[...] = m_new_all


def _mlstm(pb, pf, conv_w, b_i, b_f, g, L):
    s = pb.shape[0]
    kern = functools.partial(_mlstm_kernel, L=L)
    halo = lambda c: jnp.maximum(c * (L // SUBLANES) - 1, 0)
    return pl.pallas_call(
        kern,
        grid=(s // L,),
        in_specs=[pl.BlockSpec((L, M_WIDTH), lambda c: (c, PF_QM // M_WIDTH)),
                  pl.BlockSpec((L, M_WIDTH), lambda c: (c, PF_KM // M_WIDTH)),
                  pl.BlockSpec((SUBLANES, M_WIDTH), lambda c: (halo(c), PF_QM // M_WIDTH)),
                  pl.BlockSpec((SUBLANES, M_WIDTH), lambda c: (halo(c), PF_KM // M_WIDTH)),
                  pl.BlockSpec((L, M_WIDTH), lambda c: (c, PB_VM // M_WIDTH)),
                  pl.BlockSpec((L, M_WIDTH), lambda c: (c, PF_OM // M_WIDTH)),
                  pl.BlockSpec((L, LANES), lambda c: (c, PF_GI // LANES)),
                  pl.BlockSpec((L, LANES), lambda c: (c, PF_GF // LANES)),
                  pl.BlockSpec((CONV_W, 2 * M_WIDTH), lambda c: (0, 0)),
                  pl.BlockSpec((1, LANES), lambda c: (0, 0)),
                  pl.BlockSpec((1, LANES), lambda c: (0, 0)),
                  pl.BlockSpec((1, M_WIDTH), lambda c: (0, 0))],
        out_specs=pl.BlockSpec((L, M_WIDTH), lambda c: (c, 0)),
        out_shape=jax.ShapeDtypeStruct((s, M_WIDTH), BF16),
        scratch_shapes=[pltpu.VMEM((L + SUBLANES, M_WIDTH), F32),
                        pltpu.VMEM((L + SUBLANES, M_WIDTH), F32),
                        pltpu.VMEM((M_HEADS, HEAD_DIM, HEAD_DIM), F32),
                        pltpu.VMEM((SUBLANES, HEAD_DIM), F32),
                        pltpu.VMEM((1, LANES), F32)],
        compiler_params=_cparams("arbitrary"),
        name="mlstm",
    )(pf, pf, pf, pf, pb, pf, pf, pf, conv_w, b_i, b_f, g)


def _layer_norm(z, g, b):
    mu = jnp.mean(z, axis=-1, keepdims=True)
    zc = z - mu
    var = jnp.mean(zc * zc, axis=-1, keepdims=True)
    return zc * lax.rsqrt(var + EPS) * g + b


def _out_kernel(ya_ref, ym_ref, yc_ref, w_ref, x_ref, g_ref, b_ref, o_ref):
    acc = jnp.dot(ya_ref[...], w_ref[0:A_WIDTH, :], preferred_element_type=F32)
    acc = acc + jnp.dot(ym_ref[...], w_ref[A_WIDTH:A_WIDTH + M_WIDTH, :], preferred_element_type=F32)
    acc = acc + jnp.dot(yc_ref[...], w_ref[A_WIDTH + M_WIDTH:D_MODEL, :], preferred_element_type=F32)
    o_ref[...] = _layer_norm(ALPHA * x_ref[...] + acc, g_ref[...], b_ref[...])


def _out_proj(ya, ym, yc, w, x, g, b, tm):
    s = x.shape[0]
    return pl.pallas_call(
        _out_kernel,
        grid=(s // tm,),
        in_specs=[pl.BlockSpec((tm, A_WIDTH), lambda i: (i, 0)),
                  pl.BlockSpec((tm, M_WIDTH), lambda i: (i, 0)),
                  pl.BlockSpec((tm, C_WIDTH), lambda i: (i, 0)),
                  pl.BlockSpec((D_MODEL, D_MODEL), lambda i: (0, 0)),
                  pl.BlockSpec((tm, D_MODEL), lambda i: (i, 0)),
                  pl.BlockSpec((1, D_MODEL), lambda i: (0, 0)),
                  pl.BlockSpec((1, D_MODEL), lambda i: (0, 0))],
        out_specs=pl.BlockSpec((tm, D_MODEL), lambda i: (i, 0)),
        out_shape=jax.ShapeDtypeStruct((s, D_MODEL), F32),
        compiler_params=_cparams("arbitrary"),
        name="out_proj_ln",
    )(ya, ym, yc, w, x, g, b)


def _ffn_kernel(x_ref, wu_ref, wd_ref, g_ref, b_ref, o_ref, xb_ref, acc_ref):
    f = pl.program_id(1)

    @pl.when(f == 0)
    def _():
        xb_ref[...] = x_ref[...].astype(BF16)
        acc_ref[...] = jnp.zeros(acc_ref.shape, F32)

    hdn = jnp.maximum(jnp.dot(xb_ref[...], wu_ref[...], preferred_element_type=F32), 0.0)
    acc_ref[...] += jnp.dot((hdn * hdn).astype(BF16), wd_ref[...], preferred_element_type=F32)

    @pl.when(f == pl.num_programs(1) - 1)
    def _():
        o_ref[...] = _layer_norm(ALPHA * x_ref[...] + acc_ref[...], g_ref[...], b_ref[...])


def _ffn(x, wu, wd, g, b, tm, tf):
    s = x.shape[0]
    return pl.pallas_call(
        _ffn_kernel,
        grid=(s // tm, D_FF // tf),
        in_specs=[pl.BlockSpec((tm, D_MODEL), lambda i, f: (i, 0)),
                  pl.BlockSpec((D_MODEL, tf), lambda i, f: (0, f)),
                  pl.BlockSpec((tf, D_MODEL), lambda i, f: (f, 0)),
                  pl.BlockSpec((1, D_MODEL), lambda i, f: (0, 0)),
                  pl.BlockSpec((1, D_MODEL), lambda i, f: (0, 0))],
        out_specs=pl.BlockSpec((tm, D_MODEL), lambda i, f: (i, 0)),
        out_shape=jax.ShapeDtypeStruct((s, D_MODEL), F32),
        scratch_shapes=[pltpu.VMEM((tm, D_MODEL), BF16),
                        pltpu.VMEM((tm, D_MODEL), F32)],
        compiler_params=_cparams("arbitrary", "arbitrary"),
        name="ffn_ln",
    )(x, wu, wd, g, b)


def _seg(w, i):
    return w[..., _OFFS[i]:_OFFS[i] + SIZES[i]]


def _pad_cols(w, width):
    return jnp.pad(w, [(0, 0)] * (w.ndim - 1) + [(0, width - w.shape[-1])])


def _layout_w_in(w_in):
    (q_a, k_a, v_a, q_i, k_i, w_i, q_m, k_m, v_m, o_m, i_m, f_m, q_c, k_c, v_c) = [
        _seg(w_in, i) for i in range(len(SIZES))]
    wb = jnp.concatenate([q_a, k_a, v_a, v_m, q_c, k_c, v_c, q_i, k_i, k_i], axis=-1)
    wb = _pad_cols(wb, PB_WIDTH).astype(BF16)
    wf = jnp.concatenate([q_m, k_m, o_m, _pad_cols(w_i, LANES), _pad_cols(i_m, LANES),
                          _pad_cols(f_m, LANES)], axis=-1)
    wf = _pad_cols(wf, PF_WIDTH).astype(BF16)
    return wb, wf


def _tile(n, pref):
    t = min(n, pref)
    assert n % t == 0, (n, t)
    return t


def kernel(x, w_in, conv_m, b_i, b_f, m_norm_g, lam_q1, lam_k1, lam_q2, lam_k2, c_norm_g, w_out,
           ln1_g, ln1_b, w_up, w_down, ln2_g, ln2_b):
    batch, s, d = x.shape
    assert batch == 1 and d == D_MODEL
    ksel = min(TOPK_MAX, s // 4)
    wb_all, wf_all = _layout_w_in(w_in)
    w_out_b = w_out.astype(BF16)
    w_up_b = w_up.astype(BF16)
    w_down_b = w_down.astype(BF16)
    conv_w = conv_m.reshape(DEPTH, CONV_W, 2 * M_WIDTH)
    b_i_p = _pad_cols(b_i, LANES).reshape(DEPTH, 1, LANES)
    b_f_p = _pad_cols(b_f, LANES).reshape(DEPTH, 1, LANES)
    lam_p = jnp.stack([lam_q1, lam_k1, lam_q2, lam_k2], axis=1)

    tm_proj = _tile(s, 1024)
    t_attn = _tile(s, 512)
    tq_idx = _tile(s, 128)
    tk_idx = _tile(s, 512)
    l_chunk = _tile(s, 256)
    tm_out = _tile(s, 512)
    tm_ffn = _tile(s, 512)

    h = x.reshape(s, d)
    for l in range(DEPTH):
        pb = _matmul(h, wb_all[l], BF16, tm_proj, 768, "proj_bf16")
        pf = _matmul(h, wf_all[l], F32, tm_proj, 768, "proj_f32")
        mask = _dsa_mask(pb, pf, ksel, tq_idx, tk_idx)
        y_a = _dsa_attention(pb, mask, t_attn, t_attn)
        y_m = _mlstm(pb, pf, conv_w[l], b_i_p[l], b_f_p[l], m_norm_g[l].reshape(1, M_WIDTH), l_chunk)
        lam_init = 0.8 - 0.6 * math.exp(-0.3 * l)
        y_c = _diff_attention(pb, lam_p[l], c_norm_g[l].reshape(1, C_WIDTH), lam_init, t_attn, t_attn)
        h = _out_proj(y_a, y_m, y_c, w_out_b[l], h, ln1_g[l].reshape(1, d), ln1_b[l].reshape(1, d), tm_out)
        h = _ffn(h, w_up_b[l], w_down_b[l], ln2_g[l].reshape(1, d), ln2_b[l].reshape(1, d), tm_ffn, 1024)
    return h.reshape(batch, s, d)
```

```python
import functools
import math

import jax
import jax.numpy as jnp
from jax import lax
from jax.experimental import pallas as pl
from jax.experimental.pallas import tpu as pltpu

F32 = jnp.float32
BF16 = jnp.bfloat16
I32 = jnp.int32

D_MODEL = 2048
DEPTH = 4
HEAD_DIM = 128
A_HEADS = 6
A_WIDTH = A_HEADS * HEAD_DIM
IDX_HEADS = 8
IDX_DIM = 64
TOPK_MAX = 256
M_HEADS = 6
M_WIDTH = M_HEADS * HEAD_DIM
CONV_W = 4
C_HEADS = 4
C_QK = 64
C_WIDTH = C_HEADS * 2 * C_QK
D_FF = 4 * D_MODEL
ALPHA = (2.0 * DEPTH) ** 0.25
EPS = 1e-5

SIZES = (A_WIDTH, A_WIDTH, A_WIDTH, IDX_HEADS * IDX_DIM, IDX_DIM, IDX_HEADS,
         M_WIDTH, M_WIDTH, M_WIDTH, M_WIDTH, M_HEADS, M_HEADS,
         C_WIDTH, C_WIDTH, C_WIDTH)
_OFFS = tuple(int(sum(SIZES[:i])) for i in range(len(SIZES)))

LANES = 128
SUBLANES = 8
VMEM_LIMIT_BYTES = 56 * 1024 * 1024
INT_MIN = -2 ** 31
NEG_INF = float("-inf")
LOG2E = math.log2(math.e)

PB_QA, PB_KA, PB_VA, PB_VM = 0, 768, 1536, 2304
PB_QC, PB_KC, PB_VC, PB_QI = 3072, 3584, 4096, 4608
PB_KK = 5120
PB_WIDTH = 5376
PF_QM, PF_KM, PF_OM = 0, 768, 1536
PF_WI, PF_GI, PF_GF = 2304, 2432, 2560
PF_WIDTH = 3072


def _cparams(*sem):
    return pltpu.CompilerParams(dimension_semantics=sem, vmem_limit_bytes=VMEM_LIMIT_BYTES)


def _alibi_slopes(n):
    return [2.0 ** (-8.0 * (h + 1) / n) for h in range(n)]


def _mm_kernel(x_ref, w_ref, o_ref, xb_ref):
    @pl.when(pl.program_id(1) == 0)
    def _():
        xb_ref[...] = x_ref[...].astype(BF16)

    o_ref[...] = jnp.dot(xb_ref[...], w_ref[...], preferred_element_type=F32).astype(o_ref.dtype)


def _matmul(x, w, out_dtype, tm, tn, name):
    m, k = x.shape
    n = w.shape[1]
    return pl.pallas_call(
        _mm_kernel,
        grid=(m // tm, n // tn),
        in_specs=[pl.BlockSpec((tm, k), lambda i, j: (i, 0)),
                  pl.BlockSpec((k, tn), lambda i, j: (0, j))],
        out_specs=pl.BlockSpec((tm, tn), lambda i, j: (i, j)),
        out_shape=jax.ShapeDtypeStruct((m, n), out_dtype),
        scratch_shapes=[pltpu.VMEM((tm, k), BF16)],
        compiler_params=_cparams("arbitrary", "arbitrary"),
        name=name,
    )(x, w)


def _order_key(x):
    bits = lax.bitcast_convert_type(x, I32)
    key = jnp.where(bits < 0, bits ^ jnp.int32(0x7FFFFFFF), bits)
    return jnp.where(bits == jnp.int32(INT_MIN), 0, key)


def _idx_kernel(qi_ref, kk_ref, wi_ref, mask_ref, q8_ref, w8_ref, keys_ref, gm_ref,
                *, tq, tk, nk, ksel, jbits):
    i = pl.program_id(0)
    nck = ((i + 1) * tq + tk - 1) // tk
    groups = tk // LANES
    rb = min(tq, 128)

    lane = lax.broadcasted_iota(I32, (tq, LANES), 1)
    lo_half = jnp.where(lane < IDX_DIM, 1.0, 0.0).astype(F32)
    hi_half = 1.0 - lo_half
    for h in range(IDX_HEADS):
        qp = qi_ref[:, (h // 2) * LANES:(h // 2 + 1) * LANES].astype(F32)
        q8_ref[h * tq:(h + 1) * tq, :] = (qp * (lo_half if h % 2 == 0 else hi_half)).astype(BF16)
        w8_ref[h * tq:(h + 1) * tq, :] = wi_ref[:, h:h + 1]

    row = i * tq + lax.broadcasted_iota(I32, (tq, tk), 0)
    col0 = lax.broadcasted_iota(I32, (tq, tk), 1)

    def score_chunk(c, gmax):
        kc = kk_ref[pl.ds(pl.multiple_of(c * tk, tk), tk), :]
        y = lax.dot_general(q8_ref[...], kc, (((1,), (1,)), ((), ())), preferred_element_type=F32)
        z = jnp.maximum(y, 0.0) * w8_ref[...]
        sc = z[0:tq]
        for h in range(1, IDX_HEADS):
            sc = sc + z[h * tq:(h + 1) * tq]
        causal = c * tk + col0 <= row
        keys_ref[c] = jnp.where(causal, _order_key(sc), jnp.int32(INT_MIN))
        scm = jnp.where(causal, sc, NEG_INF)
        gmax = list(gmax)
        for g in range(groups):
            gmax[g % 2] = jnp.maximum(gmax[g % 2], scm[:, g * LANES:(g + 1) * LANES])
        return tuple(gmax)

    ninf = jnp.full((tq, LANES), NEG_INF, F32)
    g0, g1 = lax.fori_loop(0, nck, score_chunk, (ninf, ninf))
    gm_ref[0] = jnp.minimum(g0, g1)
    gm_ref[1] = jnp.maximum(g0, g1)

    lane_rb = lax.broadcasted_iota(I32, (rb, LANES), 1)
    col_rb = lax.broadcasted_iota(I32, (rb, tk), 1)

    for r in range(tq // rb):
        rs = slice(r * rb, (r + 1) * rb)
        last = (i * tq + (r + 1) * rb + tk - 1) // tk

        def count(pred, rs=rs, last=last):
            def body(c, acc):
                for g in range(groups):
                    k = keys_ref[c, rs, g * LANES:(g + 1) * LANES]
                    acc = jnp.where(pred(k, c * tk + g * LANES), acc + 1.0, acc)
                return acc
            acc = lax.fori_loop(0, last, body, jnp.zeros((rb, LANES), F32))
            return jnp.sum(acc, axis=1, keepdims=True)

        lo_f = jnp.min(gm_ref[0, rs, :], axis=1, keepdims=True)
        hi_f = jnp.max(gm_ref[1, rs, :], axis=1, keepdims=True)
        n_valid = i * tq + r * rb + lax.broadcasted_iota(I32, (rb, 1), 0) + 1
        few = n_valid <= ksel
        lo0 = jnp.where(jnp.logical_or(few, lo_f == NEG_INF), jnp.int32(INT_MIN), _order_key(lo_f))
        hi0 = jnp.where(few, jnp.int32(INT_MIN), _order_key(hi_f))

        def unresolved(st):
            lo, hi = st
            return jnp.max(jnp.where(lo < hi, 1.0, 0.0)) > 0.0

        def bisect(st, count=count):
            lo, hi = st
            mid = (lo >> 1) + (hi >> 1) + ((lo | hi) & 1)
            cnt = count(lambda k, c0: k >= mid)
            ge = cnt >= ksel
            exact = cnt == ksel
            return (jnp.where(ge, mid, lo), jnp.where(exact, mid, jnp.where(ge, hi, mid - 1)))

        thr, _ = lax.while_loop(unresolved, bisect, (lo0, hi0))
        n_gt = count(lambda k, c0: k > thr)
        n_ge = count(lambda k, c0: k >= thr)
        need = ksel - n_gt
        excess = jnp.logical_and(thr > jnp.int32(INT_MIN), n_ge > ksel)
        any_excess = jnp.max(jnp.where(excess, 1.0, 0.0)) > 0.0

        def tie_search(count=count, thr=thr, need=need):
            def jbody(b, jc):
                cand = jc + lax.shift_left(jnp.int32(1), jbits - 1 - b)
                cnt = count(lambda k, c0: jnp.logical_and(k == thr, c0 + lane_rb < cand))
                return jnp.where(cnt <= need, cand, jc)
            return lax.fori_loop(0, jbits, jbody, jnp.zeros((rb, 1), I32))

        jcut = lax.cond(any_excess, tie_search, lambda: jnp.full((rb, 1), 2 ** jbits - 1, I32))

        for c in range(nk):
            @pl.when(c < last)
            def _(c=c, rs=rs, thr=thr, jcut=jcut):
                key = keys_ref[c, rs, :]
                tie = jnp.logical_and(key == thr, c * tk + col_rb < jcut)
                sel = jnp.logical_and(key > jnp.int32(INT_MIN), jnp.logical_or(key > thr, tie))
                mask_ref[rs, c * tk:(c + 1) * tk] = jnp.where(sel, 1.0, 0.0).astype(BF16)

            @pl.when(c >= last)
            def _(c=c, rs=rs):
                mask_ref[rs, c * tk:(c + 1) * tk] = jnp.zeros((rb, tk), BF16)


def _dsa_mask(pb, pf, ksel, tq, tk):
    s = pb.shape[0]
    nk = s // tk
    jbits = int(s).bit_length()
    assert ksel <= 2 * LANES, "the threshold bracket relies on 2 * LANES disjoint key groups"
    kern = functools.partial(_idx_kernel, tq=tq, tk=tk, nk=nk, ksel=ksel, jbits=jbits)
    return pl.pallas_call(
        kern,
        grid=(s // tq,),
        in_specs=[pl.BlockSpec((tq, IDX_HEADS * IDX_DIM), lambda i: (i, PB_QI // (IDX_HEADS * IDX_DIM))),
                  pl.BlockSpec((s, LANES), lambda i: (0, PB_KK // LANES)),
                  pl.BlockSpec((tq, LANES), lambda i: (i, PF_WI // LANES))],
        out_specs=pl.BlockSpec((tq, s), lambda i: (i, 0)),
        out_shape=jax.ShapeDtypeStruct((s, s), BF16),
        scratch_shapes=[pltpu.VMEM((IDX_HEADS * tq, LANES), BF16),
                        pltpu.VMEM((IDX_HEADS * tq, 1), F32),
                        pltpu.VMEM((nk, tq, tk), I32),
                        pltpu.VMEM((2, tq, LANES), F32)],
        compiler_params=_cparams("arbitrary"),
        name="dsa_index_mask",
    )(pb, pb, pf)


def _dsa_attn_kernel(q_ref, k_ref, v_ref, mask_ref, o_ref, qs_ref, mb_ref, s_ref, m_ref, l_ref, acc_ref,
                     *, tq, tk):
    i = pl.program_id(0)
    j = pl.program_id(1)
    slopes = _alibi_slopes(A_HEADS)
    reps = tk // LANES

    @pl.when(j == 0)
    def _():
        m_ref[...] = jnp.full(m_ref.shape, NEG_INF, F32)
        l_ref[...] = jnp.zeros(l_ref.shape, F32)
        acc_ref[...] = jnp.zeros(acc_ref.shape, F32)
        qs_ref[...] = (q_ref[...].astype(F32) * (HEAD_DIM ** -0.5 * LOG2E)).astype(BF16)

    @pl.when(j * tk < (i + 1) * tq)
    def _():
        mf = mask_ref[...].astype(F32)
        mb_ref[...] = jnp.where(mf > 0.0, mf - 1.0, NEG_INF)
        kpos = (j * tk - i * tq + lax.broadcasted_iota(I32, (1, tk), 1)).astype(F32)

        def scores(h):
            hs = slice(h * HEAD_DIM, (h + 1) * HEAD_DIM)
            s = lax.dot_general(qs_ref[:, hs], k_ref[:, hs], (((1,), (1,)), ((), ())),
                                preferred_element_type=F32)
            s = s + (mb_ref[...] + (slopes[h] * LOG2E) * kpos)
            s_ref[h % 2] = s
            m_prev = m_ref[h]
            m_cur = jnp.maximum(m_prev, jnp.max(s, axis=1, keepdims=True))
            m_ref[h] = m_cur
            m_safe = jnp.where(m_cur == NEG_INF, 0.0, m_cur)
            return m_safe, jnp.exp2(m_prev - m_safe)

        def accumulate(h, m_safe, a):
            hs = slice(h * HEAD_DIM, (h + 1) * HEAD_DIM)
            p = jnp.exp2(s_ref[h % 2] - jnp.tile(m_safe, (1, reps)))
            l_ref[h] = a * l_ref[h] + jnp.sum(p, axis=1, keepdims=True)
            acc_ref[:, hs] = a * acc_ref[:, hs] + jnp.dot(p.astype(BF16), v_ref[:, hs],
                                                         preferred_element_type=F32)

        stats = scores(0)
        for h in range(A_HEADS):
            nxt = scores(h + 1) if h + 1 < A_HEADS else None
            accumulate(h, *stats)
            stats = nxt

    @pl.when(j == pl.num_programs(1) - 1)
    def _():
        for h in range(A_HEADS):
            hs = slice(h * HEAD_DIM, (h + 1) * HEAD_DIM)
            o_ref[:, hs] = (acc_ref[:, hs] / l_ref[h]).astype(o_ref.dtype)


def _last_kv(i, tq, tk):
    return ((i + 1) * tq - 1) // tk


def _dsa_attention(pb, mask, tq, tk):
    s = pb.shape[0]
    kern = functools.partial(_dsa_attn_kernel, tq=tq, tk=tk)
    kv = lambda i, j: jnp.minimum(j, _last_kv(i, tq, tk))
    return pl.pallas_call(
        kern,
        grid=(s // tq, s // tk),
        in_specs=[pl.BlockSpec((tq, A_WIDTH), lambda i, j: (i, PB_QA // A_WIDTH)),
                  pl.BlockSpec((tk, A_WIDTH), lambda i, j: (kv(i, j), PB_KA // A_WIDTH)),
                  pl.BlockSpec((tk, A_WIDTH), lambda i, j: (kv(i, j), PB_VA // A_WIDTH)),
                  pl.BlockSpec((tq, tk), lambda i, j: (i, kv(i, j)))],
        out_specs=pl.BlockSpec((tq, A_WIDTH), lambda i, j: (i, 0)),
        out_shape=jax.ShapeDtypeStruct((s, A_WIDTH), BF16),
        scratch_shapes=[pltpu.VMEM((tq, A_WIDTH), BF16),
                        pltpu.VMEM((tq, tk), F32),
                        pltpu.VMEM((2, tq, tk), F32),
                        pltpu.VMEM((A_HEADS, tq, LANES), F32),
                        pltpu.VMEM((A_HEADS, tq, LANES), F32),
                        pltpu.VMEM((tq, A_WIDTH), F32)],
        compiler_params=_cparams("arbitrary", "arbitrary"),
        name="dsa_attention",
    )(pb, pb, pb, mask)


def _diff_attn_kernel(q_ref, k_ref, v_ref, lam_ref, g_ref, o_ref, qs_ref, s_ref, m_ref, l_ref, acc_ref,
                      *, tq, tk, lam_init):
    i = pl.program_id(0)
    j = pl.program_id(1)
    slopes = _alibi_slopes(C_HEADS)
    reps = tk // LANES

    @pl.when(j == 0)
    def _():
        m_ref[...] = jnp.full(m_ref.shape, NEG_INF, F32)
        l_ref[...] = jnp.zeros(l_ref.shape, F32)
        acc_ref[...] = jnp.zeros(acc_ref.shape, F32)
        lane = lax.broadcasted_iota(I32, (tq, C_WIDTH), 1) % LANES
        qf = q_ref[...].astype(F32) * (C_QK ** -0.5 * LOG2E)
        qs_ref[0] = jnp.where(lane < C_QK, qf, 0.0).astype(BF16)
        qs_ref[1] = jnp.where(lane < C_QK, 0.0, qf).astype(BF16)

    def step(diagonal):
        kpos = (j * tk - i * tq + lax.broadcasted_iota(I32, (1, tk), 1)).astype(F32)
        if diagonal:
            causal = lax.broadcasted_iota(I32, (tq, tk), 1) <= lax.broadcasted_iota(I32, (tq, tk), 0)

        def scores(u):
            h, mp = divmod(u, 2)
            hs = slice(h * LANES, (h + 1) * LANES)
            s = lax.dot_general(qs_ref[mp, :, hs], k_ref[:, hs], (((1,), (1,)), ((), ())),
                                preferred_element_type=F32)
            s = s + (slopes[h] * LOG2E) * kpos
            if diagonal:
                s = jnp.where(causal, s, NEG_INF)
            s_ref[u % 2] = s
            m_prev = m_ref[u]
            m_cur = jnp.maximum(m_prev, jnp.max(s, axis=1, keepdims=True))
            m_ref[u] = m_cur
            return m_cur, jnp.exp2(m_prev - m_cur)

        def accumulate(u, m_cur, a):
            h, mp = divmod(u, 2)
            hs = slice(h * LANES, (h + 1) * LANES)
            p = jnp.exp2(s_ref[u % 2] - jnp.tile(m_cur, (1, reps)))
            l_ref[u] = a * l_ref[u] + jnp.sum(p, axis=1, keepdims=True)
            acc_ref[mp, :, hs] = a * acc_ref[mp, :, hs] + jnp.dot(p.astype(BF16), v_ref[:, hs],
                                                                 preferred_element_type=F32)

        stats = scores(0)
        for u in range(2 * C_HEADS):
            nxt = scores(u + 1) if u + 1 < 2 * C_HEADS else None
            accumulate(u, *stats)
            stats = nxt

    @pl.when(j < i)
    def _():
        step(False)

    @pl.when(j == i)
    def _():
        step(True)

    @pl.when(j == pl.num_programs(1) - 1)
    def _():
        lp = lam_ref[...]
        lam = (jnp.exp(jnp.sum(lp[0:1] * lp[1:2], axis=1, keepdims=True))
               - jnp.exp(jnp.sum(lp[2:3] * lp[3:4], axis=1, keepdims=True)) + lam_init)
        for h in range(C_HEADS):
            hs = slice(h * LANES, (h + 1) * LANES)
            o = acc_ref[0, :, hs] / l_ref[2 * h] - lam * (acc_ref[1, :, hs] / l_ref[2 * h + 1])
            ms = jnp.mean(o * o, axis=1, keepdims=True)
            o_ref[:, hs] = (o * lax.rsqrt(ms + EPS) * g_ref[:, hs] * (1.0 - lam_init)).astype(o_ref.dtype)


def _diff_attention(pb, lam_params, g, lam_init, tq, tk):
    s = pb.shape[0]
    assert tq == tk, "the kernel separates diagonal from off-diagonal blocks"
    kern = functools.partial(_diff_attn_kernel, tq=tq, tk=tk, lam_init=lam_init)
    kv = lambda i, j: jnp.minimum(j, _last_kv(i, tq, tk))
    return pl.pallas_call(
        kern,
        grid=(s // tq, s // tk),
        in_specs=[pl.BlockSpec((tq, C_WIDTH), lambda i, j: (i, PB_QC // C_WIDTH)),
                  pl.BlockSpec((tk, C_WIDTH), lambda i, j: (kv(i, j), PB_KC // C_WIDTH)),
                  pl.BlockSpec((tk, C_WIDTH), lambda i, j: (kv(i, j), PB_VC // C_WIDTH)),
                  pl.BlockSpec((4, C_QK), lambda i, j: (0, 0)),
                  pl.BlockSpec((1, C_WIDTH), lambda i, j: (0, 0))],
        out_specs=pl.BlockSpec((tq, C_WIDTH), lambda i, j: (i, 0)),
        out_shape=jax.ShapeDtypeStruct((s, C_WIDTH), BF16),
        scratch_shapes=[pltpu.VMEM((2, tq, C_WIDTH), BF16),
                        pltpu.VMEM((2, tq, tk), F32),
                        pltpu.VMEM((2 * C_HEADS, tq, LANES), F32),
                        pltpu.VMEM((2 * C_HEADS, tq, LANES), F32),
                        pltpu.VMEM((2, tq, C_WIDTH), F32)],
        compiler_params=_cparams("arbitrary", "arbitrary"),
        name="diff_attention",
    )(pb, pb, pb, lam_params, g)


def _split3(x):
    x1 = x.astype(BF16)
    r1 = x - x1.astype(F32)
    x2 = r1.astype(BF16)
    x3 = (r1 - x2.astype(F32)).astype(BF16)
    return x1, x2, x3


def _mlstm_kernel(q_ref, k_ref, qh_ref, kh_ref, v_ref, o_ref, gi_ref, gf_ref, cw_ref, bi_ref, bf_ref,
                  g_ref, y_ref, xq_ref, xk_ref, c_ref, n_ref, m_ref, *, L):
    c = pl.program_id(0)

    @pl.when(c == 0)
    def _():
        c_ref[...] = jnp.zeros(c_ref.shape, F32)
        n_ref[...] = jnp.zeros(n_ref.shape, F32)
        m_ref[...] = jnp.zeros(m_ref.shape, F32)

    first = (c > 0).astype(F32)
    xq_ref[0:SUBLANES, :] = qh_ref[...] * first
    xq_ref[SUBLANES:SUBLANES + L, :] = q_ref[...]
    xk_ref[0:SUBLANES, :] = kh_ref[...] * first
    xk_ref[SUBLANES:SUBLANES + L, :] = k_ref[...]
    qc = jnp.zeros((L, M_WIDTH), F32)
    kc = jnp.zeros((L, M_WIDTH), F32)
    for t in range(CONV_W):
        off = SUBLANES - (CONV_W - 1) + t
        qc = qc + xq_ref[off:off + L, :] * cw_ref[t:t + 1, 0:M_WIDTH]
        kc = kc + xk_ref[off:off + L, :] * cw_ref[t:t + 1, M_WIDTH:2 * M_WIDTH]
    qc = qc * jax.nn.sigmoid(qc)
    kc = kc * jax.nn.sigmoid(kc) * (HEAD_DIM ** -0.5)

    li = gi_ref[...] + bi_ref[...]
    fp = gf_ref[...] + bf_ref[...]
    lf = jnp.minimum(fp, 0.0) - jnp.log(1.0 + jnp.exp(-jnp.abs(fp)))
    r_i = lax.broadcasted_iota(I32, (L, L), 0)
    c_i = lax.broadcasted_iota(I32, (L, L), 1)
    tril = c_i <= r_i
    tril_b = jnp.where(tril, 1.0, 0.0).astype(BF16)
    a = jnp.zeros((L, LANES), F32)
    for part in _split3(lf):
        a = a + jnp.dot(tril_b, part, preferred_element_type=F32)
    b = li - a
    g_tot = a[L - 1:L, :]
    b_t = b.T

    m_all = m_ref[...]
    m_new_all = m_all
    for h in range(M_HEADS):
        hs = slice(h * HEAD_DIM, (h + 1) * HEAD_DIM)
        qh = qc[:, hs]
        kh = kc[:, hs]
        vh = v_ref[:, hs]
        qb = qh.astype(BF16)
        kb = kh.astype(BF16)
        m_prev = m_all[:, h:h + 1]
        d = jnp.where(tril, b_t[h:h + 1, :], NEG_INF)
        mm = jnp.maximum(m_prev, jnp.max(d, axis=1, keepdims=True))
        w_intra = jnp.exp(d - mm)
        w_inter = jnp.exp(m_prev - mm)
        qk = lax.dot_general(qb, kb, (((1,), (1,)), ((), ())), preferred_element_type=F32) * w_intra
        c_h = c_ref[h]
        num = (jnp.dot(qk.astype(BF16), vh, preferred_element_type=F32)
               + w_inter * jnp.dot(qb, c_h.astype(BF16), preferred_element_type=F32))
        den = (jnp.sum(qk, axis=1, keepdims=True)
               + w_inter * jnp.sum(qh * n_ref[h:h + 1, :], axis=1, keepdims=True))
        m_row = a[:, h:h + 1] + mm
        hh = num / jnp.maximum(jnp.abs(den), jnp.exp(-m_row))
        mm_last = mm[L - 1:L, :]
        ws = jnp.exp(b[:, h:h + 1] - mm_last)
        wc = jnp.exp(m_prev - mm_last)
        c_ref[h] = wc * c_h + lax.dot_general(kb, (ws * vh.astype(F32)).astype(BF16),
                                              (((0,), (0,)), ((), ())), preferred_element_type=F32)
        n_ref[h:h + 1, :] = wc * n_ref[h:h + 1, :] + jnp.sum(ws * kh, axis=0, keepdims=True)
        lane = lax.broadcasted_iota(I32, (1, LANES), 1)
        m_new_all = jnp.where(lane == h, g_tot + mm_last, m_new_all)
        ms = jnp.mean(hh * hh, axis=1, keepdims=True)
        hn = hh * lax.rsqrt(ms + EPS) * g_ref[:, hs]
        y_ref[:, hs] = (jax.nn.sigmoid(o_ref[:, hs]) * hn).astype(y_ref.dtype)
    m_ref[...] = m_new_all


def _mlstm(pb, pf, conv_w, b_i, b_f, g, L):
    s = pb.shape[0]
    kern = functools.partial(_mlstm_kernel, L=L)
    halo = lambda c: jnp.maximum(c * (L // SUBLANES) - 1, 0)
    return pl.pallas_call(
        kern,
        grid=(s // L,),
        in_specs=[pl.BlockSpec((L, M_WIDTH), lambda c: (c, PF_QM // M_WIDTH)),
                  pl.BlockSpec((L, M_WIDTH), lambda c: (c, PF_KM // M_WIDTH)),
                  pl.BlockSpec((SUBLANES, M_WIDTH), lambda c: (halo(c), PF_QM // M_WIDTH)),
                  pl.BlockSpec((SUBLANES, M_WIDTH), lambda c: (halo(c), PF_KM // M_WIDTH)),
                  pl.BlockSpec((L, M_WIDTH), lambda c: (c, PB_VM // M_WIDTH)),
                  pl.BlockSpec((L, M_WIDTH), lambda c: (c, PF_OM // M_WIDTH)),
                  pl.BlockSpec((L, LANES), lambda c: (c, PF_GI // LANES)),
                  pl.BlockSpec((L, LANES), lambda c: (c, PF_GF // LANES)),
                  pl.BlockSpec((CONV_W, 2 * M_WIDTH), lambda c: (0, 0)),
                  pl.BlockSpec((1, LANES), lambda c: (0, 0)),
                  pl.BlockSpec((1, LANES), lambda c: (0, 0)),
                  pl.BlockSpec((1, M_WIDTH), lambda c: (0, 0))],
        out_specs=pl.BlockSpec((L, M_WIDTH), lambda c: (c, 0)),
        out_shape=jax.ShapeDtypeStruct((s, M_WIDTH), BF16),
        scratch_shapes=[pltpu.VMEM((L + SUBLANES, M_WIDTH), F32),
                        pltpu.VMEM((L + SUBLANES, M_WIDTH), F32),
                        pltpu.VMEM((M_HEADS, HEAD_DIM, HEAD_DIM), F32),
                        pltpu.VMEM((SUBLANES, HEAD_DIM), F32),
                        pltpu.VMEM((1, LANES), F32)],
        compiler_params=_cparams("arbitrary"),
        name="mlstm",
    )(pf, pf, pf, pf, pb, pf, pf, pf, conv_w, b_i, b_f, g)


def _layer_norm(z, g, b):
    mu = jnp.mean(z, axis=-1, keepdims=True)
    zc = z - mu
    var = jnp.mean(zc * zc, axis=-1, keepdims=True)
    return zc * lax.rsqrt(var + EPS) * g + b


def _out_kernel(ya_ref, ym_ref, yc_ref, w_ref, x_ref, g_ref, b_ref, o_ref):
    acc = jnp.dot(ya_ref[...], w_ref[0:A_WIDTH, :], preferred_element_type=F32)
    acc = acc + jnp.dot(ym_ref[...], w_ref[A_WIDTH:A_WIDTH + M_WIDTH, :], preferred_element_type=F32)
    acc = acc + jnp.dot(yc_ref[...], w_ref[A_WIDTH + M_WIDTH:D_MODEL, :], preferred_element_type=F32)
    o_ref[...] = _layer_norm(ALPHA * x_ref[...] + acc, g_ref[...], b_ref[...])


def _out_proj(ya, ym, yc, w, x, g, b, tm):
    s = x.shape[0]
    return pl.pallas_call(
        _out_kernel,
        grid=(s // tm,),
        in_specs=[pl.BlockSpec((tm, A_WIDTH), lambda i: (i, 0)),
                  pl.BlockSpec((tm, M_WIDTH), lambda i: (i, 0)),
                  pl.BlockSpec((tm, C_WIDTH), lambda i: (i, 0)),
                  pl.BlockSpec((D_MODEL, D_MODEL), lambda i: (0, 0)),
                  pl.BlockSpec((tm, D_MODEL), lambda i: (i, 0)),
                  pl.BlockSpec((1, D_MODEL), lambda i: (0, 0)),
                  pl.BlockSpec((1, D_MODEL), lambda i: (0, 0))],
        out_specs=pl.BlockSpec((tm, D_MODEL), lambda i: (i, 0)),
        out_shape=jax.ShapeDtypeStruct((s, D_MODEL), F32),
        compiler_params=_cparams("arbitrary"),
        name="out_proj_ln",
    )(ya, ym, yc, w, x, g, b)


def _ffn_kernel(x_ref, wu_ref, wd_ref, g_ref, b_ref, o_ref, xb_ref, acc_ref):
    f = pl.program_id(1)

    @pl.when(f == 0)
    def _():
        xb_ref[...] = x_ref[...].astype(BF16)
        acc_ref[...] = jnp.zeros(acc_ref.shape, F32)

    hdn = jnp.maximum(jnp.dot(xb_ref[...], wu_ref[...], preferred_element_type=F32), 0.0)
    acc_ref[...] += jnp.dot((hdn * hdn).astype(BF16), wd_ref[...], preferred_element_type=F32)

    @pl.when(f == pl.num_programs(1) - 1)
    def _():
        o_ref[...] = _layer_norm(ALPHA * x_ref[...] + acc_ref[...], g_ref[...], b_ref[...])


def _ffn(x, wu, wd, g, b, tm, tf):
    s = x.shape[0]
    return pl.pallas_call(
        _ffn_kernel,
        grid=(s // tm, D_FF // tf),
        in_specs=[pl.BlockSpec((tm, D_MODEL), lambda i, f: (i, 0)),
                  pl.BlockSpec((D_MODEL, tf), lambda i, f: (0, f)),
                  pl.BlockSpec((tf, D_MODEL), lambda i, f: (f, 0)),
                  pl.BlockSpec((1, D_MODEL), lambda i, f: (0, 0)),
                  pl.BlockSpec((1, D_MODEL), lambda i, f: (0, 0))],
        out_specs=pl.BlockSpec((tm, D_MODEL), lambda i, f: (i, 0)),
        out_shape=jax.ShapeDtypeStruct((s, D_MODEL), F32),
        scratch_shapes=[pltpu.VMEM((tm, D_MODEL), BF16),
                        pltpu.VMEM((tm, D_MODEL), F32)],
        compiler_params=_cparams("arbitrary", "arbitrary"),
        name="ffn_ln",
    )(x, wu, wd, g, b)


def _seg(w, i):
    return w[..., _OFFS[i]:_OFFS[i] + SIZES[i]]


def _pad_cols(w, width):
    return jnp.pad(w, [(0, 0)] * (w.ndim - 1) + [(0, width - w.shape[-1])])


def _layout_w_in(w_in):
    (q_a, k_a, v_a, q_i, k_i, w_i, q_m, k_m, v_m, o_m, i_m, f_m, q_c, k_c, v_c) = [
        _seg(w_in, i) for i in range(len(SIZES))]
    wb = jnp.concatenate([q_a, k_a, v_a, v_m, q_c, k_c, v_c, q_i, k_i, k_i], axis=-1)
    wb = _pad_cols(wb, PB_WIDTH).astype(BF16)
    wf = jnp.concatenate([q_m, k_m, o_m, _pad_cols(w_i, LANES), _pad_cols(i_m, LANES),
                          _pad_cols(f_m, LANES)], axis=-1)
    wf = _pad_cols(wf, PF_WIDTH).astype(BF16)
    return wb, wf


def _tile(n, pref):
    t = min(n, pref)
    assert n % t == 0, (n, t)
    return t


def kernel(x, w_in, conv_m, b_i, b_f, m_norm_g, lam_q1, lam_k1, lam_q2, lam_k2, c_norm_g, w_out,
           ln1_g, ln1_b, w_up, w_down, ln2_g, ln2_b):
    batch, s, d = x.shape
    assert batch == 1 and d == D_MODEL
    ksel = min(TOPK_MAX, s // 4)
    wb_all, wf_all = _layout_w_in(w_in)
    w_out_b = w_out.astype(BF16)
    w_up_b = w_up.astype(BF16)
    w_down_b = w_down.astype(BF16)
    conv_w = conv_m.reshape(DEPTH, CONV_W, 2 * M_WIDTH)
    b_i_p = _pad_cols(b_i, LANES).reshape(DEPTH, 1, LANES)
    b_f_p = _pad_cols(b_f, LANES).reshape(DEPTH, 1, LANES)
    lam_p = jnp.stack([lam_q1, lam_k1, lam_q2, lam_k2], axis=1)

    tm_proj = _tile(s, 1024)
    t_attn = _tile(s, 512)
    tq_idx = _tile(s, 256)
    tk_idx = _tile(s, 512)
    l_chunk = _tile(s, 256)
    tm_out = _tile(s, 512)
    tm_ffn = _tile(s, 512)

    h = x.reshape(s, d)
    for l in range(DEPTH):
        pb = _matmul(h, wb_all[l], BF16, tm_proj, 768, "proj_bf16")
        pf = _matmul(h, wf_all[l], F32, tm_proj, 768, "proj_f32")
        mask = _dsa_mask(pb, pf, ksel, tq_idx, tk_idx)
        y_a = _dsa_attention(pb, mask, t_attn, t_attn)
        y_m = _mlstm(pb, pf, conv_w[l], b_i_p[l], b_f_p[l], m_norm_g[l].reshape(1, M_WIDTH), l_chunk)
        lam_init = 0.8 - 0.6 * math.exp(-0.3 * l)
        y_c = _diff_attention(pb, lam_p[l], c_norm_g[l].reshape(1, C_WIDTH), lam_init, t_attn, t_attn)
        h = _out_proj(y_a, y_m, y_c, w_out_b[l], h, ln1_g[l].reshape(1, d), ln1_b[l].reshape(1, d), tm_out)
        h = _ffn(h, w_up_b[l], w_down_b[l], ln2_g[l].reshape(1, d), ln2_b[l].reshape(1, d), tm_ffn, 1024)
    return h.reshape(batch, s, d)
```

```python
import functools
import math

import jax
import jax.numpy as jnp
import numpy as np
from jax import lax
from jax.experimental import pallas as pl
from jax.experimental.pallas import tpu as pltpu

F32 = jnp.float32
BF16 = jnp.bfloat16
I32 = jnp.int32

D_MODEL = 2048
DEPTH = 4
HEAD_DIM = 128
A_HEADS = 6
A_WIDTH = A_HEADS * HEAD_DIM
IDX_HEADS = 8
IDX_DIM = 64
TOPK_MAX = 256
M_HEADS = 6
M_WIDTH = M_HEADS * HEAD_DIM
CONV_W = 4
C_HEADS = 4
C_QK = 64
C_WIDTH = C_HEADS * 2 * C_QK
D_FF = 4 * D_MODEL
ALPHA = (2.0 * DEPTH) ** 0.25
EPS = 1e-5

SIZES = (A_WIDTH, A_WIDTH, A_WIDTH, IDX_HEADS * IDX_DIM, IDX_DIM, IDX_HEADS,
         M_WIDTH, M_WIDTH, M_WIDTH, M_WIDTH, M_HEADS, M_HEADS,
         C_WIDTH, C_WIDTH, C_WIDTH)
_OFFS = tuple(int(sum(SIZES[:i])) for i in range(len(SIZES)))

LANES = 128
SUBLANES = 8
VMEM_LIMIT_BYTES = 56 * 1024 * 1024
INT_MIN = -2 ** 31
NEG_INF = float("-inf")
LOG2E = math.log2(math.e)

PB_QA, PB_KA, PB_VA, PB_VM = 0, 768, 1536, 2304
PB_QC, PB_KC, PB_VC, PB_QI = 3072, 3584, 4096, 4608
PB_KK = 5120
PB_WIDTH = 5376
PF_QM, PF_KM, PF_OM = 0, 768, 1536
PF_WI, PF_GI, PF_GF = 2304, 2432, 2560
PF_WIDTH = 3072


def _cparams(*sem):
    return pltpu.CompilerParams(dimension_semantics=sem, vmem_limit_bytes=VMEM_LIMIT_BYTES)


def _alibi_slopes(n):
    return [2.0 ** (-8.0 * (h + 1) / n) for h in range(n)]


def _mm_kernel(x_ref, w_ref, o_ref, xb_ref):
    @pl.when(pl.program_id(1) == 0)
    def _():
        xb_ref[...] = x_ref[...].astype(BF16)

    o_ref[...] = jnp.dot(xb_ref[...], w_ref[...], preferred_element_type=F32).astype(o_ref.dtype)


def _matmul(x, w, out_dtype, tm, tn, name):
    m, k = x.shape
    n = w.shape[1]
    return pl.pallas_call(
        _mm_kernel,
        grid=(m // tm, n // tn),
        in_specs=[pl.BlockSpec((tm, k), lambda i, j: (i, 0)),
                  pl.BlockSpec((k, tn), lambda i, j: (0, j))],
        out_specs=pl.BlockSpec((tm, tn), lambda i, j: (i, j)),
        out_shape=jax.ShapeDtypeStruct((m, n), out_dtype),
        scratch_shapes=[pltpu.VMEM((tm, k), BF16)],
        compiler_params=_cparams("arbitrary", "arbitrary"),
        name=name,
    )(x, w)


def _order_key(x):
    bits = lax.bitcast_convert_type(x, I32)
    key = jnp.where(bits < 0, bits ^ jnp.int32(0x7FFFFFFF), bits)
    return jnp.where(bits == jnp.int32(INT_MIN), 0, key)


def _idx_kernel(qi_ref, kk_ref, wi_ref, mask_ref, q8_ref, w8_ref, keys_ref, gm_ref,
                *, tq, tk, nk, ksel, jbits):
    i = pl.program_id(0)
    nck = ((i + 1) * tq + tk - 1) // tk
    groups = tk // LANES
    rb = min(tq, 128)

    lane = lax.broadcasted_iota(I32, (tq, LANES), 1)
    lo_half = jnp.where(lane < IDX_DIM, 1.0, 0.0).astype(F32)
    hi_half = 1.0 - lo_half
    for h in range(IDX_HEADS):
        qp = qi_ref[:, (h // 2) * LANES:(h // 2 + 1) * LANES].astype(F32)
        q8_ref[h * tq:(h + 1) * tq, :] = (qp * (lo_half if h % 2 == 0 else hi_half)).astype(BF16)
        w8_ref[h * tq:(h + 1) * tq, :] = wi_ref[:, h:h + 1]

    row = i * tq + lax.broadcasted_iota(I32, (tq, tk), 0)
    col0 = lax.broadcasted_iota(I32, (tq, tk), 1)

    def score_chunk(c, gmax):
        kc = kk_ref[pl.ds(pl.multiple_of(c * tk, tk), tk), :]
        y = lax.dot_general(q8_ref[...], kc, (((1,), (1,)), ((), ())), preferred_element_type=F32)
        z = jnp.maximum(y, 0.0) * w8_ref[...]
        sc = z[0:tq]
        for h in range(1, IDX_HEADS):
            sc = sc + z[h * tq:(h + 1) * tq]
        causal = c * tk + col0 <= row
        keys_ref[c] = jnp.where(causal, _order_key(sc), jnp.int32(INT_MIN))
        scm = jnp.where(causal, sc, NEG_INF)
        gmax = list(gmax)
        for g in range(groups):
            gmax[g % 2] = jnp.maximum(gmax[g % 2], scm[:, g * LANES:(g + 1) * LANES])
        return tuple(gmax)

    ninf = jnp.full((tq, LANES), NEG_INF, F32)
    g0, g1 = lax.fori_loop(0, nck, score_chunk, (ninf, ninf))
    gm_ref[0] = jnp.minimum(g0, g1)
    gm_ref[1] = jnp.maximum(g0, g1)

    lane_rb = lax.broadcasted_iota(I32, (rb, LANES), 1)
    col_rb = lax.broadcasted_iota(I32, (rb, tk), 1)

    for r in range(tq // rb):
        rs = slice(r * rb, (r + 1) * rb)
        last = (i * tq + (r + 1) * rb + tk - 1) // tk

        def count(pred, rs=rs, last=last):
            def body(c, acc):
                for g in range(groups):
                    k = keys_ref[c, rs, g * LANES:(g + 1) * LANES]
                    acc = jnp.where(pred(k, c * tk + g * LANES), acc + 1.0, acc)
                return acc
            acc = lax.fori_loop(0, last, body, jnp.zeros((rb, LANES), F32))
            return jnp.sum(acc, axis=1, keepdims=True)

        lo_f = jnp.min(gm_ref[0, rs, :], axis=1, keepdims=True)
        hi_f = jnp.max(gm_ref[1, rs, :], axis=1, keepdims=True)
        n_valid = i * tq + r * rb + lax.broadcasted_iota(I32, (rb, 1), 0) + 1
        few = n_valid <= ksel
        lo0 = jnp.where(jnp.logical_or(few, lo_f == NEG_INF), jnp.int32(INT_MIN), _order_key(lo_f))
        hi0 = jnp.where(few, jnp.int32(INT_MIN), _order_key(hi_f))

        def unresolved(st):
            _, lo, hi, _, _ = st
            return jnp.max(jnp.where(lo < hi, 1.0, 0.0)) > 0.0

        def probe(st, count=count):
            it, lo, hi, cl, ch = st
            frac = jnp.where(it % 2 == 0, (cl - ksel + 0.5) / (cl - ch), 0.5)
            span = hi.astype(F32) - lo.astype(F32)
            step = jnp.clip(jnp.ceil(span * frac), 1.0, 2.0 ** 30).astype(I32)
            gap = hi - lo
            mid = lo + jnp.minimum(step, jnp.where(gap < 0, jnp.int32(2 ** 31 - 1), gap))
            cnt = count(lambda k, c0: k >= mid)
            ge = cnt >= ksel
            exact = cnt == ksel
            return (it + 1,
                    jnp.where(ge, mid, lo), jnp.where(exact, mid, jnp.where(ge, hi, mid - 1)),
                    jnp.where(ge, cnt, cl), jnp.where(ge, ch, cnt))

        cl0 = count(lambda k, c0: k >= lo0)
        _, thr, _, _, _ = lax.while_loop(unresolved, probe,
                                         (jnp.int32(0), lo0, hi0, cl0, jnp.zeros((rb, 1), F32)))
        n_gt = count(lambda k, c0: k > thr)
        n_ge = count(lambda k, c0: k >= thr)
        need = ksel - n_gt
        excess = jnp.logical_and(thr > jnp.int32(INT_MIN), n_ge > ksel)
        any_excess = jnp.max(jnp.where(excess, 1.0, 0.0)) > 0.0

        def tie_search(count=count, thr=thr, need=need):
            def jbody(b, jc):
                cand = jc + lax.shift_left(jnp.int32(1), jbits - 1 - b)
                cnt = count(lambda k, c0: jnp.logical_and(k == thr, c0 + lane_rb < cand))
                return jnp.where(cnt <= need, cand, jc)
            return lax.fori_loop(0, jbits, jbody, jnp.zeros((rb, 1), I32))

        jcut = lax.cond(any_excess, tie_search, lambda: jnp.full((rb, 1), 2 ** jbits - 1, I32))

        for c in range(nk):
            @pl.when(c < last)
            def _(c=c, rs=rs, thr=thr, jcut=jcut):
                key = keys_ref[c, rs, :]
                tie = jnp.logical_and(key == thr, c * tk + col_rb < jcut)
                sel = jnp.logical_and(key > jnp.int32(INT_MIN), jnp.logical_or(key > thr, tie))
                mask_ref[rs, c * tk:(c + 1) * tk] = jnp.where(sel, 1.0, 0.0).astype(BF16)

            @pl.when(c >= last)
            def _(c=c, rs=rs):
                mask_ref[rs, c * tk:(c + 1) * tk] = jnp.zeros((rb, tk), BF16)


def _dsa_mask(pb, pf, ksel, tq, tk):
    s = pb.shape[0]
    nk = s // tk
    jbits = int(s).bit_length()
    assert ksel <= 2 * LANES, "the threshold bracket relies on 2 * LANES disjoint key groups"
    kern = functools.partial(_idx_kernel, tq=tq, tk=tk, nk=nk, ksel=ksel, jbits=jbits)
    return pl.pallas_call(
        kern,
        grid=(s // tq,),
        in_specs=[pl.BlockSpec((tq, IDX_HEADS * IDX_DIM), lambda i: (i, PB_QI // (IDX_HEADS * IDX_DIM))),
                  pl.BlockSpec((s, LANES), lambda i: (0, PB_KK // LANES)),
                  pl.BlockSpec((tq, LANES), lambda i: (i, PF_WI // LANES))],
        out_specs=pl.BlockSpec((tq, s), lambda i: (i, 0)),
        out_shape=jax.ShapeDtypeStruct((s, s), BF16),
        scratch_shapes=[pltpu.VMEM((IDX_HEADS * tq, LANES), BF16),
                        pltpu.VMEM((IDX_HEADS * tq, 1), F32),
                        pltpu.VMEM((nk, tq, tk), I32),
                        pltpu.VMEM((2, tq, LANES), F32)],
        compiler_params=_cparams("arbitrary"),
        name="dsa_index_mask",
    )(pb, pb, pf)


def _causal_steps(n):
    qi = [i for i in range(n) for _ in range(i + 1)]
    kj = [j for i in range(n) for j in range(i + 1)]
    return jnp.asarray(qi, I32), jnp.asarray(kj, I32)


def _bf16_terms(x):
    out = []
    r = np.float32(x)
    for _ in range(3):
        t = np.float32(np.asarray(r, dtype=jnp.bfloat16))
        out.append(float(t))
        r = np.float32(r - t)
    return out


def _alibi_q_ext(slope, rows):
    s1, s2, s3 = _bf16_terms(slope * LOG2E)
    lane = lax.broadcasted_iota(I32, (rows, LANES), 1)
    ext = jnp.where(lane < 3, s1, jnp.where(lane < 6, s2, jnp.where(lane < 9, s3, 0.0)))
    return ext.astype(BF16)


def _alibi_k_ext(first_pos, rows):
    pos = (first_pos + lax.broadcasted_iota(I32, (rows, LANES), 0)).astype(F32)
    p1, p2, p3 = _split3(pos)
    lane = lax.broadcasted_iota(I32, (rows, LANES), 1)
    sel = lane % 3
    ext = jnp.where(sel == 0, p1.astype(F32), jnp.where(sel == 1, p2.astype(F32), p3.astype(F32)))
    return jnp.where(lane < 9, ext, 0.0).astype(BF16)


def _dsa_attn_kernel(qi_ref, kj_ref, q_ref, k_ref, v_ref, mask_ref, o_ref, qa_ref, mb_ref, s_ref, m_ref,
                     acc_ref, *, tq, tk):
    t = pl.program_id(0)
    i = qi_ref[t]
    j = kj_ref[t]
    slopes = _alibi_slopes(A_HEADS)
    reps = tk // LANES

    @pl.when(j == 0)
    def _():
        m_ref[...] = jnp.full(m_ref.shape, NEG_INF, F32)
        acc_ref[...] = jnp.zeros(acc_ref.shape, F32)
        for h in range(A_HEADS):
            hs = slice(h * HEAD_DIM, (h + 1) * HEAD_DIM)
            qh = (q_ref[:, hs].astype(F32) * (HEAD_DIM ** -0.5 * LOG2E)).astype(BF16)
            qa_ref[h] = jnp.concatenate([qh, _alibi_q_ext(slopes[h], tq)], axis=1)

    mf = mask_ref[...].astype(F32)
    mb_ref[...] = jnp.where(mf > 0.0, mf - 1.0, NEG_INF)
    k_ext = _alibi_k_ext(j * tk - i * tq, tk)
    ones = jnp.ones((tk, LANES), BF16)

    def scores(h):
        hs = slice(h * HEAD_DIM, (h + 1) * HEAD_DIM)
        ka = jnp.concatenate([k_ref[:, hs], k_ext], axis=1)
        s = lax.dot_general(qa_ref[h], ka, (((1,), (1,)), ((), ())), preferred_element_type=F32)
        s = s + mb_ref[...]
        s_ref[h % 2] = s
        m_prev = m_ref[h]
        m_cur = jnp.maximum(m_prev, jnp.max(s, axis=1, keepdims=True))
        m_ref[h] = m_cur
        m_safe = jnp.where(m_cur == NEG_INF, 0.0, m_cur)
        return m_safe, jnp.exp2(m_prev - m_safe)

    def accumulate(h, m_safe, a):
        hs = slice(h * HEAD_DIM, (h + 1) * HEAD_DIM)
        p = jnp.exp2(s_ref[h % 2] - jnp.tile(m_safe, (1, reps)))
        va = jnp.concatenate([v_ref[:, hs], ones], axis=1)
        acc_ref[h] = jnp.tile(a, (1, 2)) * acc_ref[h] + jnp.dot(p.astype(BF16), va,
                                                                 preferred_element_type=F32)

    stats = scores(0)
    for h in range(A_HEADS):
        nxt = scores(h + 1) if h + 1 < A_HEADS else None
        accumulate(h, *stats)
        stats = nxt

    @pl.when(j == i)
    def _():
        for h in range(A_HEADS):
            hs = slice(h * HEAD_DIM, (h + 1) * HEAD_DIM)
            o_ref[:, hs] = (acc_ref[h, :, 0:HEAD_DIM] / acc_ref[h, :, HEAD_DIM:2 * HEAD_DIM]).astype(o_ref.dtype)


def _dsa_attention(pb, mask, tq):
    s = pb.shape[0]
    kern = functools.partial(_dsa_attn_kernel, tq=tq, tk=tq)
    qi, kj = _causal_steps(s // tq)
    grid_spec = pltpu.PrefetchScalarGridSpec(
        num_scalar_prefetch=2,
        grid=(int(qi.shape[0]),),
        in_specs=[pl.BlockSpec((tq, A_WIDTH), lambda t, qi, kj: (qi[t], PB_QA // A_WIDTH)),
                  pl.BlockSpec((tq, A_WIDTH), lambda t, qi, kj: (kj[t], PB_KA // A_WIDTH)),
                  pl.BlockSpec((tq, A_WIDTH), lambda t, qi, kj: (kj[t], PB_VA // A_WIDTH)),
                  pl.BlockSpec((tq, tq), lambda t, qi, kj: (qi[t], kj[t]))],
        out_specs=pl.BlockSpec((tq, A_WIDTH), lambda t, qi, kj: (qi[t], 0)),
        scratch_shapes=[pltpu.VMEM((A_HEADS, tq, 2 * HEAD_DIM), BF16),
                        pltpu.VMEM((tq, tq), F32),
                        pltpu.VMEM((2, tq, tq), F32),
                        pltpu.VMEM((A_HEADS, tq, LANES), F32),
                        pltpu.VMEM((A_HEADS, tq, 2 * HEAD_DIM), F32)])
    return pl.pallas_call(
        kern,
        grid_spec=grid_spec,
        out_shape=jax.ShapeDtypeStruct((s, A_WIDTH), BF16),
        compiler_params=_cparams("arbitrary"),
        name="dsa_attention",
    )(qi, kj, pb, pb, pb, mask)


def _diff_attn_kernel(qi_ref, kj_ref, q_ref, k_ref, v_ref, lam_ref, g_ref, o_ref, qa_ref, s_ref, m_ref,
                      acc_ref, *, tq, tk, lam_init):
    t = pl.program_id(0)
    i = qi_ref[t]
    j = kj_ref[t]
    slopes = _alibi_slopes(C_HEADS)
    reps = tk // LANES

    @pl.when(j == 0)
    def _():
        m_ref[...] = jnp.full(m_ref.shape, NEG_INF, F32)
        acc_ref[...] = jnp.zeros(acc_ref.shape, F32)
        lane = lax.broadcasted_iota(I32, (tq, LANES), 1)
        for h in range(C_HEADS):
            qf = q_ref[:, h * LANES:(h + 1) * LANES].astype(F32) * (C_QK ** -0.5 * LOG2E)
            ext = _alibi_q_ext(slopes[h], tq)
            qa_ref[2 * h] = jnp.concatenate([jnp.where(lane < C_QK, qf, 0.0).astype(BF16), ext], axis=1)
            qa_ref[2 * h + 1] = jnp.concatenate([jnp.where(lane < C_QK, 0.0, qf).astype(BF16), ext], axis=1)

    def step(diagonal):
        k_ext = _alibi_k_ext(j * tk - i * tq, tk)
        ones = jnp.ones((tk, LANES), BF16)
        if diagonal:
            causal = lax.broadcasted_iota(I32, (tq, tk), 1) <= lax.broadcasted_iota(I32, (tq, tk), 0)

        def scores(u):
            h = u // 2
            ka = jnp.concatenate([k_ref[:, h * LANES:(h + 1) * LANES], k_ext], axis=1)
            s = lax.dot_general(qa_ref[u], ka, (((1,), (1,)), ((), ())), preferred_element_type=F32)
            if diagonal:
                s = jnp.where(causal, s, NEG_INF)
            s_ref[u % 2] = s
            m_prev = m_ref[u]
            m_cur = jnp.maximum(m_prev, jnp.max(s, axis=1, keepdims=True))
            m_ref[u] = m_cur
            return m_cur, jnp.exp2(m_prev - m_cur)

        def accumulate(u, m_cur, a):
            h = u // 2
            p = jnp.exp2(s_ref[u % 2] - jnp.tile(m_cur, (1, reps)))
            va = jnp.concatenate([v_ref[:, h * LANES:(h + 1) * LANES], ones], axis=1)
            acc_ref[u] = jnp.tile(a, (1, 2)) * acc_ref[u] + jnp.dot(p.astype(BF16), va,
                                                                     preferred_element_type=F32)

        stats = scores(0)
        for u in range(2 * C_HEADS):
            nxt = scores(u + 1) if u + 1 < 2 * C_HEADS else None
            accumulate(u, *stats)
            stats = nxt

    @pl.when(j < i)
    def _():
        step(False)

    @pl.when(j == i)
    def _():
        step(True)
        lp = lam_ref[...]
        lam = (jnp.exp(jnp.sum(lp[0:1] * lp[1:2], axis=1, keepdims=True))
               - jnp.exp(jnp.sum(lp[2:3] * lp[3:4], axis=1, keepdims=True)) + lam_init)
        for h in range(C_HEADS):
            hs = slice(h * LANES, (h + 1) * LANES)
            o = (acc_ref[2 * h, :, 0:LANES] / acc_ref[2 * h, :, LANES:2 * LANES]
                 - lam * (acc_ref[2 * h + 1, :, 0:LANES] / acc_ref[2 * h + 1, :, LANES:2 * LANES]))
            ms = jnp.mean(o * o, axis=1, keepdims=True)
            o_ref[:, hs] = (o * lax.rsqrt(ms + EPS) * g_ref[:, hs] * (1.0 - lam_init)).astype(o_ref.dtype)


def _diff_attention(pb, lam_params, g, lam_init, tq):
    s = pb.shape[0]
    kern = functools.partial(_diff_attn_kernel, tq=tq, tk=tq, lam_init=lam_init)
    qi, kj = _causal_steps(s // tq)
    grid_spec = pltpu.PrefetchScalarGridSpec(
        num_scalar_prefetch=2,
        grid=(int(qi.shape[0]),),
        in_specs=[pl.BlockSpec((tq, C_WIDTH), lambda t, qi, kj: (qi[t], PB_QC // C_WIDTH)),
                  pl.BlockSpec((tq, C_WIDTH), lambda t, qi, kj: (kj[t], PB_KC // C_WIDTH)),
                  pl.BlockSpec((tq, C_WIDTH), lambda t, qi, kj: (kj[t], PB_VC // C_WIDTH)),
                  pl.BlockSpec((4, C_QK), lambda t, qi, kj: (0, 0)),
                  pl.BlockSpec((1, C_WIDTH), lambda t, qi, kj: (0, 0))],
        out_specs=pl.BlockSpec((tq, C_WIDTH), lambda t, qi, kj: (qi[t], 0)),
        scratch_shapes=[pltpu.VMEM((2 * C_HEADS, tq, 2 * LANES), BF16),
                        pltpu.VMEM((2, tq, tq), F32),
                        pltpu.VMEM((2 * C_HEADS, tq, LANES), F32),
                        pltpu.VMEM((2 * C_HEADS, tq, 2 * LANES), F32)])
    return pl.pallas_call(
        kern,
        grid_spec=grid_spec,
        out_shape=jax.ShapeDtypeStruct((s, C_WIDTH), BF16),
        compiler_params=_cparams("arbitrary"),
        name="diff_attention",
    )(qi, kj, pb, pb, pb, lam_params, g)


def _split3(x):
    x1 = x.astype(BF16)
    r1 = x - x1.astype(F32)
    x2 = r1.astype(BF16)
    x3 = (r1 - x2.astype(F32)).astype(BF16)
    return x1, x2, x3


def _mlstm_kernel(q_ref, k_ref, qh_ref, kh_ref, v_ref, o_ref, gi_ref, gf_ref, cw_ref, bi_ref, bf_ref,
                  g_ref, y_ref, xq_ref, xk_ref, c_ref, n_ref, m_ref, *, L):
    c = pl.program_id(0)

    @pl.when(c == 0)
    def _():
        c_ref[...] = jnp.zeros(c_ref.shape, F32)
        n_ref[...] = jnp.zeros(n_ref.shape, F32)
        m_ref[...] = jnp.zeros(m_ref.shape, F32)

    first = (c > 0).astype(F32)
    xq_ref[0:SUBLANES, :] = qh_ref[...] * first
    xq_ref[SUBLANES:SUBLANES + L, :] = q_ref[...]
    xk_ref[0:SUBLANES, :] = kh_ref[...] * first
    xk_ref[SUBLANES:SUBLANES + L, :] = k_ref[...]
    qc = jnp.zeros((L, M_WIDTH), F32)
    kc = jnp.zeros((L, M_WIDTH), F32)
    for t in range(CONV_W):
        off = SUBLANES - (CONV_W - 1) + t
        qc = qc + xq_ref[off:off + L, :] * cw_ref[t:t + 1, 0:M_WIDTH]
        kc = kc + xk_ref[off:off + L, :] * cw_ref[t:t + 1, M_WIDTH:2 * M_WIDTH]
    qc = qc * jax.nn.sigmoid(qc)
    kc = kc * jax.nn.sigmoid(kc) * (HEAD_DIM ** -0.5)

    li = gi_ref[...] + bi_ref[...]
    fp = gf_ref[...] + bf_ref[...]
    lf = jnp.minimum(fp, 0.0) - jnp.log(1.0 + jnp.exp(-jnp.abs(fp)))
    r_i = lax.broadcasted_iota(I32, (L, L), 0)
    c_i = lax.broadcasted_iota(I32, (L, L), 1)
    tril = c_i <= r_i
    tril_b = jnp.where(tril, 1.0, 0.0).astype(BF16)
    a = jnp.zeros((L, LANES), F32)
    for part in _split3(lf):
        a = a + jnp.dot(tril_b, part, preferred_element_type=F32)
    b = li - a
    g_tot = a[L - 1:L, :]
    b_t = b.T

    m_all = m_ref[...]
    m_new_all = m_all
    for h in range(M_HEADS):
        hs = slice(h * HEAD_DIM, (h + 1) * HEAD_DIM)
        qh = qc[:, hs]
        kh = kc[:, hs]
        vh = v_ref[:, hs]
        qb = qh.astype(BF16)
        kb = kh.astype(BF16)
        m_prev = m_all[:, h:h + 1]
        d = jnp.where(tril, b_t[h:h + 1, :], NEG_INF)
        mm = jnp.maximum(m_prev, jnp.max(d, axis=1, keepdims=True))
        w_intra = jnp.exp(d - mm)
        w_inter = jnp.exp(m_prev - mm)
        qk = lax.dot_general(qb, kb, (((1,), (1,)), ((), ())), preferred_element_type=F32) * w_intra
        c_h = c_ref[h]
        num = (jnp.dot(qk.astype(BF16), vh, preferred_element_type=F32)
               + w_inter * jnp.dot(qb, c_h.astype(BF16), preferred_element_type=F32))
        den = (jnp.sum(qk, axis=1, keepdims=True)
               + w_inter * jnp.sum(qh * n_ref[h:h + 1, :], axis=1, keepdims=True))
        m_row = a[:, h:h + 1] + mm
        hh = num / jnp.maximum(jnp.abs(den), jnp.exp(-m_row))
        mm_last = mm[L - 1:L, :]
        ws = jnp.exp(b[:, h:h + 1] - mm_last)
        wc = jnp.exp(m_prev - mm_last)
        c_ref[h] = wc * c_h + lax.dot_general(kb, (ws * vh.astype(F32)).astype(BF16),
                                              (((0,), (0,)), ((), ())), preferred_element_type=F32)
        n_ref[h:h + 1, :] = wc * n_ref[h:h + 1, :] + jnp.sum(ws * kh, axis=0, keepdims=True)
        lane = lax.broadcasted_iota(I32, (1, LANES), 1)
        m_new_all = jnp.where(lane == h, g_tot + mm_last, m_new_all)
        ms = jnp.mean(hh * hh, axis=1, keepdims=True)
        hn = hh * lax.rsqrt(ms + EPS) * g_ref[:, hs]
        y_ref[:, hs] = (jax.nn.sigmoid(o_ref[:, hs]) * hn).astype(y_ref.dtype)
    m_ref[...] = m_new_all


def _mlstm(pb, pf, conv_w, b_i, b_f, g, L):
    s = pb.shape[0]
    kern = functools.partial(_mlstm_kernel, L=L)
    halo = lambda c: jnp.maximum(c * (L // SUBLANES) - 1, 0)
    return pl.pallas_call(
        kern,
        grid=(s // L,),
        in_specs=[pl.BlockSpec((L, M_WIDTH), lambda c: (c, PF_QM // M_WIDTH)),
                  pl.BlockSpec((L, M_WIDTH), lambda c: (c, PF_KM // M_WIDTH)),
                  pl.BlockSpec((SUBLANES, M_WIDTH), lambda c: (halo(c), PF_QM // M_WIDTH)),
                  pl.BlockSpec((SUBLANES, M_WIDTH), lambda c: (halo(c), PF_KM // M_WIDTH)),
                  pl.BlockSpec((L, M_WIDTH), lambda c: (c, PB_VM // M_WIDTH)),
                  pl.BlockSpec((L, M_WIDTH), lambda c: (c, PF_OM // M_WIDTH)),
                  pl.BlockSpec((L, LANES), lambda c: (c, PF_GI // LANES)),
                  pl.BlockSpec((L, LANES), lambda c: (c, PF_GF // LANES)),
                  pl.BlockSpec((CONV_W, 2 * M_WIDTH), lambda c: (0, 0)),
                  pl.BlockSpec((1, LANES), lambda c: (0, 0)),
                  pl.BlockSpec((1, LANES), lambda c: (0, 0)),
                  pl.BlockSpec((1, M_WIDTH), lambda c: (0, 0))],
        out_specs=pl.BlockSpec((L, M_WIDTH), lambda c: (c, 0)),
        out_shape=jax.ShapeDtypeStruct((s, M_WIDTH), BF16),
        scratch_shapes=[pltpu.VMEM((L + SUBLANES, M_WIDTH), F32),
                        pltpu.VMEM((L + SUBLANES, M_WIDTH), F32),
                        pltpu.VMEM((M_HEADS, HEAD_DIM, HEAD_DIM), F32),
                        pltpu.VMEM((SUBLANES, HEAD_DIM), F32),
                        pltpu.VMEM((1, LANES), F32)],
        compiler_params=_cparams("arbitrary"),
        name="mlstm",
    )(pf, pf, pf, pf, pb, pf, pf, pf, conv_w, b_i, b_f, g)


def _layer_norm(z, g, b):
    mu = jnp.mean(z, axis=-1, keepdims=True)
    zc = z - mu
    var = jnp.mean(zc * zc, axis=-1, keepdims=True)
    return zc * lax.rsqrt(var + EPS) * g + b


def _out_kernel(ya_ref, ym_ref, yc_ref, w_ref, x_ref, g_ref, b_ref, o_ref):
    acc = jnp.dot(ya_ref[...], w_ref[0:A_WIDTH, :], preferred_element_type=F32)
    acc = acc + jnp.dot(ym_ref[...], w_ref[A_WIDTH:A_WIDTH + M_WIDTH, :], preferred_element_type=F32)
    acc = acc + jnp.dot(yc_ref[...], w_ref[A_WIDTH + M_WIDTH:D_MODEL, :], preferred_element_type=F32)
    o_ref[...] = _layer_norm(ALPHA * x_ref[...] + acc, g_ref[...], b_ref[...])


def _out_proj(ya, ym, yc, w, x, g, b, tm):
    s = x.shape[0]
    return pl.pallas_call(
        _out_kernel,
        grid=(s // tm,),
        in_specs=[pl.BlockSpec((tm, A_WIDTH), lambda i: (i, 0)),
                  pl.BlockSpec((tm, M_WIDTH), lambda i: (i, 0)),
                  pl.BlockSpec((tm, C_WIDTH), lambda i: (i, 0)),
                  pl.BlockSpec((D_MODEL, D_MODEL), lambda i: (0, 0)),
                  pl.BlockSpec((tm, D_MODEL), lambda i: (i, 0)),
                  pl.BlockSpec((1, D_MODEL), lambda i: (0, 0)),
                  pl.BlockSpec((1, D_MODEL), lambda i: (0, 0))],
        out_specs=pl.BlockSpec((tm, D_MODEL), lambda i: (i, 0)),
        out_shape=jax.ShapeDtypeStruct((s, D_MODEL), F32),
        compiler_params=_cparams("arbitrary"),
        name="out_proj_ln",
    )(ya, ym, yc, w, x, g, b)


def _ffn_kernel(x_ref, wu_ref, wd_ref, g_ref, b_ref, o_ref, xb_ref, acc_ref):
    f = pl.program_id(1)

    @pl.when(f == 0)
    def _():
        xb_ref[...] = x_ref[...].astype(BF16)
        acc_ref[...] = jnp.zeros(acc_ref.shape, F32)

    hdn = jnp.maximum(jnp.dot(xb_ref[...], wu_ref[...], preferred_element_type=F32), 0.0)
    acc_ref[...] += jnp.dot((hdn * hdn).astype(BF16), wd_ref[...], preferred_element_type=F32)

    @pl.when(f == pl.num_programs(1) - 1)
    def _():
        o_ref[...] = _layer_norm(ALPHA * x_ref[...] + acc_ref[...], g_ref[...], b_ref[...])


def _ffn(x, wu, wd, g, b, tm, tf):
    s = x.shape[0]
    return pl.pallas_call(
        _ffn_kernel,
        grid=(s // tm, D_FF // tf),
        in_specs=[pl.BlockSpec((tm, D_MODEL), lambda i, f: (i, 0)),
                  pl.BlockSpec((D_MODEL, tf), lambda i, f: (0, f)),
                  pl.BlockSpec((tf, D_MODEL), lambda i, f: (f, 0)),
                  pl.BlockSpec((1, D_MODEL), lambda i, f: (0, 0)),
                  pl.BlockSpec((1, D_MODEL), lambda i, f: (0, 0))],
        out_specs=pl.BlockSpec((tm, D_MODEL), lambda i, f: (i, 0)),
        out_shape=jax.ShapeDtypeStruct((s, D_MODEL), F32),
        scratch_shapes=[pltpu.VMEM((tm, D_MODEL), BF16),
                        pltpu.VMEM((tm, D_MODEL), F32)],
        compiler_params=_cparams("arbitrary", "arbitrary"),
        name="ffn_ln",
    )(x, wu, wd, g, b)


def _seg(w, i):
    return w[..., _OFFS[i]:_OFFS[i] + SIZES[i]]


def _pad_cols(w, width):
    return jnp.pad(w, [(0, 0)] * (w.ndim - 1) + [(0, width - w.shape[-1])])


def _layout_w_in(w_in):
    w_in = w_in.astype(BF16)
    (q_a, k_a, v_a, q_i, k_i, w_i, q_m, k_m, v_m, o_m, i_m, f_m, q_c, k_c, v_c) = [
        _seg(w_in, i) for i in range(len(SIZES))]
    wb = jnp.concatenate([q_a, k_a, v_a, v_m, q_c, k_c, v_c, q_i, k_i, k_i], axis=-1)
    wf = jnp.concatenate([q_m, k_m, o_m, _pad_cols(w_i, LANES), _pad_cols(i_m, LANES),
                          _pad_cols(f_m, LANES)], axis=-1)
    return _pad_cols(wb, PB_WIDTH), _pad_cols(wf, PF_WIDTH)


def _tile(n, pref):
    t = min(n, pref)
    assert n % t == 0, (n, t)
    return t


def kernel(x, w_in, conv_m, b_i, b_f, m_norm_g, lam_q1, lam_k1, lam_q2, lam_k2, c_norm_g, w_out,
           ln1_g, ln1_b, w_up, w_down, ln2_g, ln2_b):
    batch, s, d = x.shape
    assert batch == 1 and d == D_MODEL
    ksel = min(TOPK_MAX, s // 4)
    wb_all, wf_all = _layout_w_in(w_in)
    w_out_b = w_out.astype(BF16)
    w_up_b = w_up.astype(BF16)
    w_down_b = w_down.astype(BF16)
    conv_w = conv_m.reshape(DEPTH, CONV_W, 2 * M_WIDTH)
    b_i_p = _pad_cols(b_i, LANES).reshape(DEPTH, 1, LANES)
    b_f_p = _pad_cols(b_f, LANES).reshape(DEPTH, 1, LANES)
    lam_p = jnp.stack([lam_q1, lam_k1, lam_q2, lam_k2], axis=1)

    tm_proj = _tile(s, 1024)
    t_attn = _tile(s, 512)
    tq_idx = _tile(s, 256)
    tk_idx = _tile(s, 512)
    l_chunk = _tile(s, 256)
    tm_out = _tile(s, 512)
    tm_ffn = _tile(s, 512)

    h = x.reshape(s, d)
    for l in range(DEPTH):
        pb = _matmul(h, wb_all[l], BF16, tm_proj, 768, "proj_bf16")
        pf = _matmul(h, wf_all[l], F32, tm_proj, 768, "proj_f32")
        mask = _dsa_mask(pb, pf, ksel, tq_idx, tk_idx)
        y_a = _dsa_attention(pb, mask, t_attn)
        y_m = _mlstm(pb, pf, conv_w[l], b_i_p[l], b_f_p[l], m_norm_g[l].reshape(1, M_WIDTH), l_chunk)
        lam_init = 0.8 - 0.6 * math.exp(-0.3 * l)
        y_c = _diff_attention(pb, lam_p[l], c_norm_g[l].reshape(1, C_WIDTH), lam_init, t_attn)
        h = _out_proj(y_a, y_m, y_c, w_out_b[l], h, ln1_g[l].reshape(1, d), ln1_b[l].reshape(1, d), tm_out)
        h = _ffn(h, w_up_b[l], w_down_b[l], ln2_g[l].reshape(1, d), ln2_b[l].reshape(1, d), tm_ffn, 1024)
    return h.reshape(batch, s, d)
```

```python
import functools
import math

import jax
import jax.numpy as jnp
import numpy as np
from jax import lax
from jax.experimental import pallas as pl
from jax.experimental.pallas import tpu as pltpu

F32 = jnp.float32
BF16 = jnp.bfloat16
I32 = jnp.int32

D_MODEL = 2048
DEPTH = 4
HEAD_DIM = 128
A_HEADS = 6
A_WIDTH = A_HEADS * HEAD_DIM
IDX_HEADS = 8
IDX_DIM = 64
TOPK_MAX = 256
M_HEADS = 6
M_WIDTH = M_HEADS * HEAD_DIM
CONV_W = 4
C_HEADS = 4
C_QK = 64
C_WIDTH = C_HEADS * 2 * C_QK
D_FF = 4 * D_MODEL
ALPHA = (2.0 * DEPTH) ** 0.25
EPS = 1e-5

SIZES = (A_WIDTH, A_WIDTH, A_WIDTH, IDX_HEADS * IDX_DIM, IDX_DIM, IDX_HEADS,
         M_WIDTH, M_WIDTH, M_WIDTH, M_WIDTH, M_HEADS, M_HEADS,
         C_WIDTH, C_WIDTH, C_WIDTH)
_OFFS = tuple(int(sum(SIZES[:i])) for i in range(len(SIZES)))

LANES = 128
SUBLANES = 8
VMEM_LIMIT_BYTES = 56 * 1024 * 1024
INT_MIN = -2 ** 31
NEG_INF = float("-inf")
LOG2E = math.log2(math.e)

PB_QA, PB_KA, PB_VA, PB_VM = 0, 768, 1536, 2304
PB_QC, PB_KC, PB_VC, PB_QI = 3072, 3584, 4096, 4608
PB_KK = 5120
PB_WIDTH = 5376
PF_QM, PF_KM, PF_OM = 0, 768, 1536
PF_WI, PF_GI, PF_GF = 2304, 2432, 2560
PF_WIDTH = 3072


def _cparams(*sem):
    return pltpu.CompilerParams(dimension_semantics=sem, vmem_limit_bytes=VMEM_LIMIT_BYTES)


def _alibi_slopes(n):
    return [2.0 ** (-8.0 * (h + 1) / n) for h in range(n)]


def _mm_kernel(x_ref, w_ref, o_ref, xb_ref):
    @pl.when(pl.program_id(1) == 0)
    def _():
        xb_ref[...] = x_ref[...].astype(BF16)

    o_ref[...] = jnp.dot(xb_ref[...], w_ref[...], preferred_element_type=F32).astype(o_ref.dtype)


def _matmul(x, w, layer, out_dtype, tm, tn, name):
    m, k = x.shape
    n = w.shape[2]
    return pl.pallas_call(
        _mm_kernel,
        grid=(m // tm, n // tn),
        in_specs=[pl.BlockSpec((tm, k), lambda i, j: (i, 0)),
                  pl.BlockSpec((None, k, tn), lambda i, j: (layer, 0, j))],
        out_specs=pl.BlockSpec((tm, tn), lambda i, j: (i, j)),
        out_shape=jax.ShapeDtypeStruct((m, n), out_dtype),
        scratch_shapes=[pltpu.VMEM((tm, k), BF16)],
        compiler_params=_cparams("arbitrary", "arbitrary"),
        name=name,
    )(x, w)


def _order_key(x):
    bits = lax.bitcast_convert_type(x, I32)
    key = jnp.where(bits < 0, bits ^ jnp.int32(0x7FFFFFFF), bits)
    return jnp.where(bits == jnp.int32(INT_MIN), 0, key)


def _idx_kernel(qi_ref, kk_ref, wi_ref, mask_ref, q8_ref, w8_ref, keys_ref, gm_ref,
                *, tq, tk, nk, ksel, jbits):
    i = pl.program_id(0)
    nck = ((i + 1) * tq + tk - 1) // tk
    groups = tk // LANES
    rb = min(tq, 128)

    lane = lax.broadcasted_iota(I32, (tq, LANES), 1)
    lo_half = jnp.where(lane < IDX_DIM, 1.0, 0.0).astype(F32)
    hi_half = 1.0 - lo_half
    for h in range(IDX_HEADS):
        qp = qi_ref[:, (h // 2) * LANES:(h // 2 + 1) * LANES].astype(F32)
        q8_ref[h * tq:(h + 1) * tq, :] = (qp * (lo_half if h % 2 == 0 else hi_half)).astype(BF16)
        w8_ref[h * tq:(h + 1) * tq, :] = wi_ref[:, h:h + 1]

    row = i * tq + lax.broadcasted_iota(I32, (tq, tk), 0)
    col0 = lax.broadcasted_iota(I32, (tq, tk), 1)

    def score_chunk(c, gmax):
        kc = kk_ref[pl.ds(pl.multiple_of(c * tk, tk), tk), :]
        y = lax.dot_general(q8_ref[...], kc, (((1,), (1,)), ((), ())), preferred_element_type=F32)
        z = jnp.maximum(y, 0.0) * w8_ref[...]
        sc = z[0:tq]
        for h in range(1, IDX_HEADS):
            sc = sc + z[h * tq:(h + 1) * tq]
        causal = c * tk + col0 <= row
        keys_ref[c] = jnp.where(causal, _order_key(sc), jnp.int32(INT_MIN))
        scm = jnp.where(causal, sc, NEG_INF)
        gmax = list(gmax)
        for g in range(groups):
            gmax[g % 2] = jnp.maximum(gmax[g % 2], scm[:, g * LANES:(g + 1) * LANES])
        return tuple(gmax)

    ninf = jnp.full((tq, LANES), NEG_INF, F32)
    g0, g1 = lax.fori_loop(0, nck, score_chunk, (ninf, ninf))
    gm_ref[0] = jnp.minimum(g0, g1)
    gm_ref[1] = jnp.maximum(g0, g1)

    lane_rb = lax.broadcasted_iota(I32, (rb, LANES), 1)
    col_rb = lax.broadcasted_iota(I32, (rb, tk), 1)

    for r in range(tq // rb):
        rs = slice(r * rb, (r + 1) * rb)
        last = (i * tq + (r + 1) * rb + tk - 1) // tk

        def count(pred, rs=rs, last=last):
            def body(c, acc):
                for g in range(groups):
                    k = keys_ref[c, rs, g * LANES:(g + 1) * LANES]
                    acc = jnp.where(pred(k, c * tk + g * LANES), acc + 1.0, acc)
                return acc
            acc = lax.fori_loop(0, last, body, jnp.zeros((rb, LANES), F32))
            return jnp.sum(acc, axis=1, keepdims=True)

        lo_f = jnp.min(gm_ref[0, rs, :], axis=1, keepdims=True)
        hi_f = jnp.max(gm_ref[1, rs, :], axis=1, keepdims=True)
        n_valid = i * tq + r * rb + lax.broadcasted_iota(I32, (rb, 1), 0) + 1
        few = n_valid <= ksel
        lo0 = jnp.where(jnp.logical_or(few, lo_f == NEG_INF), jnp.int32(INT_MIN), _order_key(lo_f))
        hi0 = jnp.where(few, jnp.int32(INT_MIN), _order_key(hi_f))

        def unresolved(st):
            lo, hi = st
            return jnp.max(jnp.where(lo < hi, 1.0, 0.0)) > 0.0

        def bisect(st, count=count):
            lo, hi = st
            mid = (lo >> 1) + (hi >> 1) + ((lo | hi) & 1)
            cnt = count(lambda k, c0: k >= mid)
            ge = cnt >= ksel
            exact = cnt == ksel
            return (jnp.where(ge, mid, lo), jnp.where(exact, mid, jnp.where(ge, hi, mid - 1)))

        thr, _ = lax.while_loop(unresolved, bisect, (lo0, hi0))
        n_gt = count(lambda k, c0: k > thr)
        n_ge = count(lambda k, c0: k >= thr)
        need = ksel - n_gt
        excess = jnp.logical_and(thr > jnp.int32(INT_MIN), n_ge > ksel)
        any_excess = jnp.max(jnp.where(excess, 1.0, 0.0)) > 0.0

        def tie_search(count=count, thr=thr, need=need):
            def jbody(b, jc):
                cand = jc + lax.shift_left(jnp.int32(1), jbits - 1 - b)
                cnt = count(lambda k, c0: jnp.logical_and(k == thr, c0 + lane_rb < cand))
                return jnp.where(cnt <= need, cand, jc)
            return lax.fori_loop(0, jbits, jbody, jnp.zeros((rb, 1), I32))

        def write_mask(select, rs=rs, last=last):
            for c in range(nk):
                @pl.when(c < last)
                def _(c=c):
                    sel = select(keys_ref[c, rs, :], c)
                    mask_ref[rs, c * tk:(c + 1) * tk] = jnp.where(sel, 1.0, 0.0).astype(BF16)

                @pl.when(c >= last)
                def _(c=c):
                    mask_ref[rs, c * tk:(c + 1) * tk] = jnp.zeros((rb, tk), BF16)

        @pl.when(any_excess)
        def _(thr=thr, need=need, tie_search=tie_search, write_mask=write_mask):
            jcut = tie_search()

            def select(key, c):
                tie = jnp.logical_and(key == thr, c * tk + col_rb < jcut)
                return jnp.logical_and(key > jnp.int32(INT_MIN), jnp.logical_or(key > thr, tie))
            write_mask(select)

        @pl.when(jnp.logical_not(any_excess))
        def _(thr=thr, write_mask=write_mask):
            floor = jnp.maximum(thr, jnp.int32(INT_MIN + 1))
            write_mask(lambda key, c: key >= floor)


def _dsa_mask(pb, pf, ksel, tq, tk):
    s = pb.shape[0]
    nk = s // tk
    jbits = int(s).bit_length()
    assert ksel <= 2 * LANES, "the threshold bracket relies on 2 * LANES disjoint key groups"
    kern = functools.partial(_idx_kernel, tq=tq, tk=tk, nk=nk, ksel=ksel, jbits=jbits)
    return pl.pallas_call(
        kern,
        grid=(s // tq,),
        in_specs=[pl.BlockSpec((tq, IDX_HEADS * IDX_DIM), lambda i: (i, PB_QI // (IDX_HEADS * IDX_DIM))),
                  pl.BlockSpec((s, LANES), lambda i: (0, PB_KK // LANES)),
                  pl.BlockSpec((tq, LANES), lambda i: (i, PF_WI // LANES))],
        out_specs=pl.BlockSpec((tq, s), lambda i: (i, 0)),
        out_shape=jax.ShapeDtypeStruct((s, s), BF16),
        scratch_shapes=[pltpu.VMEM((IDX_HEADS * tq, LANES), BF16),
                        pltpu.VMEM((IDX_HEADS * tq, 1), F32),
                        pltpu.VMEM((nk, tq, tk), I32),
                        pltpu.VMEM((2, tq, LANES), F32)],
        compiler_params=_cparams("arbitrary"),
        name="dsa_index_mask",
    )(pb, pb, pf)


def _causal_steps(n):
    qi = [i for i in range(n) for _ in range(i + 1)]
    kj = [j for i in range(n) for j in range(i + 1)]
    return jnp.asarray(qi, I32), jnp.asarray(kj, I32)


def _bf16_terms(x):
    out = []
    r = np.float32(x)
    for _ in range(3):
        t = np.float32(np.asarray(r, dtype=jnp.bfloat16))
        out.append(float(t))
        r = np.float32(r - t)
    return out


def _alibi_q_ext(slope, rows):
    s1, s2, s3 = _bf16_terms(slope * LOG2E)
    lane = lax.broadcasted_iota(I32, (rows, LANES), 1)
    ext = jnp.where(lane < 3, s1, jnp.where(lane < 6, s2, jnp.where(lane < 9, s3, 0.0)))
    return ext.astype(BF16)


def _alibi_k_ext(first_pos, rows):
    pos = (first_pos + lax.broadcasted_iota(I32, (rows, LANES), 0)).astype(F32)
    p1, p2, p3 = _split3(pos)
    lane = lax.broadcasted_iota(I32, (rows, LANES), 1)
    sel = lane % 3
    ext = jnp.where(sel == 0, p1.astype(F32), jnp.where(sel == 1, p2.astype(F32), p3.astype(F32)))
    return jnp.where(lane < 9, ext, 0.0).astype(BF16)


def _dsa_attn_kernel(qi_ref, kj_ref, q_ref, k_ref, v_ref, mask_ref, o_ref, qa_ref, mb_ref, s_ref, m_ref,
                     acc_ref, *, tq, tk):
    t = pl.program_id(0)
    i = qi_ref[t]
    j = kj_ref[t]
    slopes = _alibi_slopes(A_HEADS)
    reps = tk // LANES

    @pl.when(j == 0)
    def _():
        m_ref[...] = jnp.full(m_ref.shape, NEG_INF, F32)
        acc_ref[...] = jnp.zeros(acc_ref.shape, F32)
        for h in range(A_HEADS):
            hs = slice(h * HEAD_DIM, (h + 1) * HEAD_DIM)
            qh = (q_ref[:, hs].astype(F32) * (HEAD_DIM ** -0.5 * LOG2E)).astype(BF16)
            qa_ref[h] = jnp.concatenate([qh, _alibi_q_ext(slopes[h], tq)], axis=1)

    mf = mask_ref[...].astype(F32)
    mb_ref[...] = jnp.where(mf > 0.0, mf - 1.0, NEG_INF)
    k_ext = _alibi_k_ext(j * tk - i * tq, tk)
    ones = jnp.ones((tk, LANES), BF16)

    def scores(h):
        hs = slice(h * HEAD_DIM, (h + 1) * HEAD_DIM)
        ka = jnp.concatenate([k_ref[:, hs], k_ext], axis=1)
        s = lax.dot_general(qa_ref[h], ka, (((1,), (1,)), ((), ())), preferred_element_type=F32)
        s = s + mb_ref[...]
        s_ref[h % 2] = s
        m_prev = m_ref[h]
        m_cur = jnp.maximum(m_prev, jnp.max(s, axis=1, keepdims=True))
        m_ref[h] = m_cur
        m_safe = jnp.where(m_cur == NEG_INF, 0.0, m_cur)
        return m_safe, jnp.exp2(m_prev - m_safe)

    def accumulate(h, m_safe, a):
        hs = slice(h * HEAD_DIM, (h + 1) * HEAD_DIM)
        p = jnp.exp2(s_ref[h % 2] - jnp.tile(m_safe, (1, reps)))
        va = jnp.concatenate([v_ref[:, hs], ones], axis=1)
        acc_ref[h] = jnp.tile(a, (1, 2)) * acc_ref[h] + jnp.dot(p.astype(BF16), va,
                                                                 preferred_element_type=F32)

    stats = scores(0)
    for h in range(A_HEADS):
        nxt = scores(h + 1) if h + 1 < A_HEADS else None
        accumulate(h, *stats)
        stats = nxt

    @pl.when(j == i)
    def _():
        for h in range(A_HEADS):
            hs = slice(h * HEAD_DIM, (h + 1) * HEAD_DIM)
            o_ref[:, hs] = (acc_ref[h, :, 0:HEAD_DIM] / acc_ref[h, :, HEAD_DIM:2 * HEAD_DIM]).astype(o_ref.dtype)


def _dsa_attention(pb, mask, tq):
    s = pb.shape[0]
    kern = functools.partial(_dsa_attn_kernel, tq=tq, tk=tq)
    qi, kj = _causal_steps(s // tq)
    grid_spec = pltpu.PrefetchScalarGridSpec(
        num_scalar_prefetch=2,
        grid=(int(qi.shape[0]),),
        in_specs=[pl.BlockSpec((tq, A_WIDTH), lambda t, qi, kj: (qi[t], PB_QA // A_WIDTH)),
                  pl.BlockSpec((tq, A_WIDTH), lambda t, qi, kj: (kj[t], PB_KA // A_WIDTH)),
                  pl.BlockSpec((tq, A_WIDTH), lambda t, qi, kj: (kj[t], PB_VA // A_WIDTH)),
                  pl.BlockSpec((tq, tq), lambda t, qi, kj: (qi[t], kj[t]))],
        out_specs=pl.BlockSpec((tq, A_WIDTH), lambda t, qi, kj: (qi[t], 0)),
        scratch_shapes=[pltpu.VMEM((A_HEADS, tq, 2 * HEAD_DIM), BF16),
                        pltpu.VMEM((tq, tq), F32),
                        pltpu.VMEM((2, tq, tq), F32),
                        pltpu.VMEM((A_HEADS, tq, LANES), F32),
                        pltpu.VMEM((A_HEADS, tq, 2 * HEAD_DIM), F32)])
    return pl.pallas_call(
        kern,
        grid_spec=grid_spec,
        out_shape=jax.ShapeDtypeStruct((s, A_WIDTH), BF16),
        compiler_params=_cparams("arbitrary"),
        name="dsa_attention",
    )(qi, kj, pb, pb, pb, mask)


def _diff_attn_kernel(qi_ref, kj_ref, q_ref, k_ref, v_ref, lam_ref, g_ref, o_ref, qa_ref, s_ref, m_ref,
                      acc_ref, *, tq, tk, lam_init):
    t = pl.program_id(0)
    i = qi_ref[t]
    j = kj_ref[t]
    slopes = _alibi_slopes(C_HEADS)
    reps = tk // LANES

    @pl.when(j == 0)
    def _():
        m_ref[...] = jnp.full(m_ref.shape, NEG_INF, F32)
        acc_ref[...] = jnp.zeros(acc_ref.shape, F32)
        lane = lax.broadcasted_iota(I32, (tq, LANES), 1)
        for h in range(C_HEADS):
            qf = q_ref[:, h * LANES:(h + 1) * LANES].astype(F32) * (C_QK ** -0.5 * LOG2E)
            ext = _alibi_q_ext(slopes[h], tq)
            qa_ref[2 * h] = jnp.concatenate([jnp.where(lane < C_QK, qf, 0.0).astype(BF16), ext], axis=1)
            qa_ref[2 * h + 1] = jnp.concatenate([jnp.where(lane < C_QK, 0.0, qf).astype(BF16), ext], axis=1)

    def step(diagonal):
        k_ext = _alibi_k_ext(j * tk - i * tq, tk)
        ones = jnp.ones((tk, LANES), BF16)
        if diagonal:
            causal = lax.broadcasted_iota(I32, (tq, tk), 1) <= lax.broadcasted_iota(I32, (tq, tk), 0)

        def scores(u):
            h = u // 2
            ka = jnp.concatenate([k_ref[:, h * LANES:(h + 1) * LANES], k_ext], axis=1)
            s = lax.dot_general(qa_ref[u], ka, (((1,), (1,)), ((), ())), preferred_element_type=F32)
            if diagonal:
                s = jnp.where(causal, s, NEG_INF)
            s_ref[u % 2] = s
            m_prev = m_ref[u]
            m_cur = jnp.maximum(m_prev, jnp.max(s, axis=1, keepdims=True))
            m_ref[u] = m_cur
            return m_cur, jnp.exp2(m_prev - m_cur)

        def accumulate(u, m_cur, a):
            h = u // 2
            p = jnp.exp2(s_ref[u % 2] - jnp.tile(m_cur, (1, reps)))
            va = jnp.concatenate([v_ref[:, h * LANES:(h + 1) * LANES], ones], axis=1)
            acc_ref[u] = jnp.tile(a, (1, 2)) * acc_ref[u] + jnp.dot(p.astype(BF16), va,
                                                                     preferred_element_type=F32)

        stats = scores(0)
        for u in range(2 * C_HEADS):
            nxt = scores(u + 1) if u + 1 < 2 * C_HEADS else None
            accumulate(u, *stats)
            stats = nxt

    @pl.when(j < i)
    def _():
        step(False)

    @pl.when(j == i)
    def _():
        step(True)
        lp = lam_ref[...]
        lam = (jnp.exp(jnp.sum(lp[0:1] * lp[1:2], axis=1, keepdims=True))
               - jnp.exp(jnp.sum(lp[2:3] * lp[3:4], axis=1, keepdims=True)) + lam_init)
        for h in range(C_HEADS):
            hs = slice(h * LANES, (h + 1) * LANES)
            o = (acc_ref[2 * h, :, 0:LANES] / acc_ref[2 * h, :, LANES:2 * LANES]
                 - lam * (acc_ref[2 * h + 1, :, 0:LANES] / acc_ref[2 * h + 1, :, LANES:2 * LANES]))
            ms = jnp.mean(o * o, axis=1, keepdims=True)
            o_ref[:, hs] = (o * lax.rsqrt(ms + EPS) * g_ref[:, hs] * (1.0 - lam_init)).astype(o_ref.dtype)


def _diff_attention(pb, lam_params, g, lam_init, tq):
    s = pb.shape[0]
    kern = functools.partial(_diff_attn_kernel, tq=tq, tk=tq, lam_init=lam_init)
    qi, kj = _causal_steps(s // tq)
    grid_spec = pltpu.PrefetchScalarGridSpec(
        num_scalar_prefetch=2,
        grid=(int(qi.shape[0]),),
        in_specs=[pl.BlockSpec((tq, C_WIDTH), lambda t, qi, kj: (qi[t], PB_QC // C_WIDTH)),
                  pl.BlockSpec((tq, C_WIDTH), lambda t, qi, kj: (kj[t], PB_KC // C_WIDTH)),
                  pl.BlockSpec((tq, C_WIDTH), lambda t, qi, kj: (kj[t], PB_VC // C_WIDTH)),
                  pl.BlockSpec((4, C_QK), lambda t, qi, kj: (0, 0)),
                  pl.BlockSpec((1, C_WIDTH), lambda t, qi, kj: (0, 0))],
        out_specs=pl.BlockSpec((tq, C_WIDTH), lambda t, qi, kj: (qi[t], 0)),
        scratch_shapes=[pltpu.VMEM((2 * C_HEADS, tq, 2 * LANES), BF16),
                        pltpu.VMEM((2, tq, tq), F32),
                        pltpu.VMEM((2 * C_HEADS, tq, LANES), F32),
                        pltpu.VMEM((2 * C_HEADS, tq, 2 * LANES), F32)])
    return pl.pallas_call(
        kern,
        grid_spec=grid_spec,
        out_shape=jax.ShapeDtypeStruct((s, C_WIDTH), BF16),
        compiler_params=_cparams("arbitrary"),
        name="diff_attention",
    )(qi, kj, pb, pb, pb, lam_params, g)


def _split3(x):
    x1 = x.astype(BF16)
    r1 = x - x1.astype(F32)
    x2 = r1.astype(BF16)
    x3 = (r1 - x2.astype(F32)).astype(BF16)
    return x1, x2, x3


def _mlstm_kernel(q_ref, k_ref, qh_ref, kh_ref, v_ref, o_ref, gi_ref, gf_ref, cw_ref, bi_ref, bf_ref,
                  g_ref, y_ref, xq_ref, xk_ref, c_ref, n_ref, m_ref, *, L):
    c = pl.program_id(0)

    @pl.when(c == 0)
    def _():
        c_ref[...] = jnp.zeros(c_ref.shape, F32)
        n_ref[...] = jnp.zeros(n_ref.shape, F32)
        m_ref[...] = jnp.zeros(m_ref.shape, F32)

    first = (c > 0).astype(F32)
    xq_ref[0:SUBLANES, :] = qh_ref[...] * first
    xq_ref[SUBLANES:SUBLANES + L, :] = q_ref[...]
    xk_ref[0:SUBLANES, :] = kh_ref[...] * first
    xk_ref[SUBLANES:SUBLANES + L, :] = k_ref[...]
    qc = jnp.zeros((L, M_WIDTH), F32)
    kc = jnp.zeros((L, M_WIDTH), F32)
    for t in range(CONV_W):
        off = SUBLANES - (CONV_W - 1) + t
        qc = qc + xq_ref[off:off + L, :] * cw_ref[t:t + 1, 0:M_WIDTH]
        kc = kc + xk_ref[off:off + L, :] * cw_ref[t:t + 1, M_WIDTH:2 * M_WIDTH]
    qc = qc * jax.nn.sigmoid(qc)
    kc = kc * jax.nn.sigmoid(kc) * (HEAD_DIM ** -0.5)

    li = gi_ref[...] + bi_ref[...]
    fp = gf_ref[...] + bf_ref[...]
    lf = jnp.minimum(fp, 0.0) - jnp.log(1.0 + jnp.exp(-jnp.abs(fp)))
    r_i = lax.broadcasted_iota(I32, (L, L), 0)
    c_i = lax.broadcasted_iota(I32, (L, L), 1)
    tril = c_i <= r_i
    tril_b = jnp.where(tril, 1.0, 0.0).astype(BF16)
    a = jnp.zeros((L, LANES), F32)
    for part in _split3(lf):
        a = a + jnp.dot(tril_b, part, preferred_element_type=F32)
    b = li - a
    g_tot = a[L - 1:L, :]
    b_t = b.T

    m_all = m_ref[...]
    m_new_all = m_all
    for h in range(M_HEADS):
        hs = slice(h * HEAD_DIM, (h + 1) * HEAD_DIM)
        qh = qc[:, hs]
        kh = kc[:, hs]
        vh = v_ref[:, hs]
        qb = qh.astype(BF16)
        kb = kh.astype(BF16)
        m_prev = m_all[:, h:h + 1]
        d = jnp.where(tril, b_t[h:h + 1, :], NEG_INF)
        mm = jnp.maximum(m_prev, jnp.max(d, axis=1, keepdims=True))
        w_intra = jnp.exp(d - mm)
        w_inter = jnp.exp(m_prev - mm)
        qk = lax.dot_general(qb, kb, (((1,), (1,)), ((), ())), preferred_element_type=F32) * w_intra
        c_h = c_ref[h]
        num = (jnp.dot(qk.astype(BF16), vh, preferred_element_type=F32)
               + w_inter * jnp.dot(qb, c_h.astype(BF16), preferred_element_type=F32))
        den = (jnp.sum(qk, axis=1, keepdims=True)
               + w_inter * jnp.sum(qh * n_ref[h:h + 1, :], axis=1, keepdims=True))
        m_row = a[:, h:h + 1] + mm
        hh = num / jnp.maximum(jnp.abs(den), jnp.exp(-m_row))
        mm_last = mm[L - 1:L, :]
        ws = jnp.exp(b[:, h:h + 1] - mm_last)
        wc = jnp.exp(m_prev - mm_last)
        c_ref[h] = wc * c_h + lax.dot_general(kb, (ws * vh.astype(F32)).astype(BF16),
                                              (((0,), (0,)), ((), ())), preferred_element_type=F32)
        n_ref[h:h + 1, :] = wc * n_ref[h:h + 1, :] + jnp.sum(ws * kh, axis=0, keepdims=True)
        lane = lax.broadcasted_iota(I32, (1, LANES), 1)
        m_new_all = jnp.where(lane == h, g_tot + mm_last, m_new_all)
        ms = jnp.mean(hh * hh, axis=1, keepdims=True)
        hn = hh * lax.rsqrt(ms + EPS) * g_ref[:, hs]
        y_ref[:, hs] = (jax.nn.sigmoid(o_ref[:, hs]) * hn).astype(y_ref.dtype)
    m_ref[...] = m_new_all


def _mlstm(pb, pf, conv_w, b_i, b_f, g, L):
    s = pb.shape[0]
    kern = functools.partial(_mlstm_kernel, L=L)
    halo = lambda c: jnp.maximum(c * (L // SUBLANES) - 1, 0)
    return pl.pallas_call(
        kern,
        grid=(s // L,),
        in_specs=[pl.BlockSpec((L, M_WIDTH), lambda c: (c, PF_QM // M_WIDTH)),
                  pl.BlockSpec((L, M_WIDTH), lambda c: (c, PF_KM // M_WIDTH)),
                  pl.BlockSpec((SUBLANES, M_WIDTH), lambda c: (halo(c), PF_QM // M_WIDTH)),
                  pl.BlockSpec((SUBLANES, M_WIDTH), lambda c: (halo(c), PF_KM // M_WIDTH)),
                  pl.BlockSpec((L, M_WIDTH), lambda c: (c, PB_VM // M_WIDTH)),
                  pl.BlockSpec((L, M_WIDTH), lambda c: (c, PF_OM // M_WIDTH)),
                  pl.BlockSpec((L, LANES), lambda c: (c, PF_GI // LANES)),
                  pl.BlockSpec((L, LANES), lambda c: (c, PF_GF // LANES)),
                  pl.BlockSpec((CONV_W, 2 * M_WIDTH), lambda c: (0, 0)),
                  pl.BlockSpec((1, LANES), lambda c: (0, 0)),
                  pl.BlockSpec((1, LANES), lambda c: (0, 0)),
                  pl.BlockSpec((1, M_WIDTH), lambda c: (0, 0))],
        out_specs=pl.BlockSpec((L, M_WIDTH), lambda c: (c, 0)),
        out_shape=jax.ShapeDtypeStruct((s, M_WIDTH), BF16),
        scratch_shapes=[pltpu.VMEM((L + SUBLANES, M_WIDTH), F32),
                        pltpu.VMEM((L + SUBLANES, M_WIDTH), F32),
                        pltpu.VMEM((M_HEADS, HEAD_DIM, HEAD_DIM), F32),
                        pltpu.VMEM((SUBLANES, HEAD_DIM), F32),
                        pltpu.VMEM((1, LANES), F32)],
        compiler_params=_cparams("arbitrary"),
        name="mlstm",
    )(pf, pf, pf, pf, pb, pf, pf, pf, conv_w, b_i, b_f, g)


def _layer_norm(z, g, b):
    mu = jnp.mean(z, axis=-1, keepdims=True)
    zc = z - mu
    var = jnp.mean(zc * zc, axis=-1, keepdims=True)
    return zc * lax.rsqrt(var + EPS) * g + b


def _out_kernel(ya_ref, ym_ref, yc_ref, w_ref, x_ref, g_ref, b_ref, o_ref):
    acc = jnp.dot(ya_ref[...], w_ref[0:A_WIDTH, :], preferred_element_type=F32)
    acc = acc + jnp.dot(ym_ref[...], w_ref[A_WIDTH:A_WIDTH + M_WIDTH, :], preferred_element_type=F32)
    acc = acc + jnp.dot(yc_ref[...], w_ref[A_WIDTH + M_WIDTH:D_MODEL, :], preferred_element_type=F32)
    o_ref[...] = _layer_norm(ALPHA * x_ref[...] + acc, g_ref[...], b_ref[...])


def _out_proj(ya, ym, yc, w, layer, x, g, b, tm):
    s = x.shape[0]
    return pl.pallas_call(
        _out_kernel,
        grid=(s // tm,),
        in_specs=[pl.BlockSpec((tm, A_WIDTH), lambda i: (i, 0)),
                  pl.BlockSpec((tm, M_WIDTH), lambda i: (i, 0)),
                  pl.BlockSpec((tm, C_WIDTH), lambda i: (i, 0)),
                  pl.BlockSpec((None, D_MODEL, D_MODEL), lambda i: (layer, 0, 0)),
                  pl.BlockSpec((tm, D_MODEL), lambda i: (i, 0)),
                  pl.BlockSpec((1, D_MODEL), lambda i: (0, 0)),
                  pl.BlockSpec((1, D_MODEL), lambda i: (0, 0))],
        out_specs=pl.BlockSpec((tm, D_MODEL), lambda i: (i, 0)),
        out_shape=jax.ShapeDtypeStruct((s, D_MODEL), F32),
        compiler_params=_cparams("arbitrary"),
        name="out_proj_ln",
    )(ya, ym, yc, w, x, g, b)


def _ffn_kernel(x_ref, wu_ref, wd_ref, g_ref, b_ref, o_ref, xb_ref, acc_ref):
    f = pl.program_id(1)

    @pl.when(f == 0)
    def _():
        xb_ref[...] = x_ref[...].astype(BF16)
        acc_ref[...] = jnp.zeros(acc_ref.shape, F32)

    hdn = jnp.maximum(jnp.dot(xb_ref[...], wu_ref[...], preferred_element_type=F32), 0.0)
    acc_ref[...] += jnp.dot((hdn * hdn).astype(BF16), wd_ref[...], preferred_element_type=F32)

    @pl.when(f == pl.num_programs(1) - 1)
    def _():
        o_ref[...] = _layer_norm(ALPHA * x_ref[...] + acc_ref[...], g_ref[...], b_ref[...])


def _ffn(x, wu, wd, layer, g, b, tm, tf):
    s = x.shape[0]
    return pl.pallas_call(
        _ffn_kernel,
        grid=(s // tm, D_FF // tf),
        in_specs=[pl.BlockSpec((tm, D_MODEL), lambda i, f: (i, 0)),
                  pl.BlockSpec((None, D_MODEL, tf), lambda i, f: (layer, 0, f)),
                  pl.BlockSpec((None, tf, D_MODEL), lambda i, f: (layer, f, 0)),
                  pl.BlockSpec((1, D_MODEL), lambda i, f: (0, 0)),
                  pl.BlockSpec((1, D_MODEL), lambda i, f: (0, 0))],
        out_specs=pl.BlockSpec((tm, D_MODEL), lambda i, f: (i, 0)),
        out_shape=jax.ShapeDtypeStruct((s, D_MODEL), F32),
        scratch_shapes=[pltpu.VMEM((tm, D_MODEL), BF16),
                        pltpu.VMEM((tm, D_MODEL), F32)],
        compiler_params=_cparams("arbitrary", "arbitrary"),
        name="ffn_ln",
    )(x, wu, wd, g, b)


def _seg(w, i):
    return w[..., _OFFS[i]:_OFFS[i] + SIZES[i]]


def _pad_cols(w, width):
    return jnp.pad(w, [(0, 0)] * (w.ndim - 1) + [(0, width - w.shape[-1])])


def _layout_w_in(w_in):
    w_in = w_in.astype(BF16)
    (q_a, k_a, v_a, q_i, k_i, w_i, q_m, k_m, v_m, o_m, i_m, f_m, q_c, k_c, v_c) = [
        _seg(w_in, i) for i in range(len(SIZES))]
    wb = jnp.concatenate([q_a, k_a, v_a, v_m, q_c, k_c, v_c, q_i, k_i, k_i], axis=-1)
    wf = jnp.concatenate([q_m, k_m, o_m, _pad_cols(w_i, LANES), _pad_cols(i_m, LANES),
                          _pad_cols(f_m, LANES)], axis=-1)
    return _pad_cols(wb, PB_WIDTH), _pad_cols(wf, PF_WIDTH)


def _tile(n, pref):
    t = min(n, pref)
    assert n % t == 0, (n, t)
    return t


def kernel(x, w_in, conv_m, b_i, b_f, m_norm_g, lam_q1, lam_k1, lam_q2, lam_k2, c_norm_g, w_out,
           ln1_g, ln1_b, w_up, w_down, ln2_g, ln2_b):
    batch, s, d = x.shape
    assert batch == 1 and d == D_MODEL
    ksel = min(TOPK_MAX, s // 4)
    wb_all, wf_all = _layout_w_in(w_in)
    w_out_b = w_out.astype(BF16)
    w_up_b = w_up.astype(BF16)
    w_down_b = w_down.astype(BF16)
    conv_w = conv_m.reshape(DEPTH, CONV_W, 2 * M_WIDTH)
    b_i_p = _pad_cols(b_i, LANES).reshape(DEPTH, 1, LANES)
    b_f_p = _pad_cols(b_f, LANES).reshape(DEPTH, 1, LANES)
    lam_p = jnp.stack([lam_q1, lam_k1, lam_q2, lam_k2], axis=1)

    tm_proj = _tile(s, 1024)
    t_attn = _tile(s, 512)
    tq_idx = _tile(s, 256)
    tk_idx = _tile(s, 512)
    l_chunk = _tile(s, 256)
    tm_out = _tile(s, 512)
    tm_ffn = _tile(s, 512)

    h = x.reshape(s, d)
    for l in range(DEPTH):
        pb = _matmul(h, wb_all, l, BF16, tm_proj, 768, "proj_bf16")
        pf = _matmul(h, wf_all, l, F32, tm_proj, 768, "proj_f32")
        mask = _dsa_mask(pb, pf, ksel, tq_idx, tk_idx)
        y_a = _dsa_attention(pb, mask, t_attn)
        y_m = _mlstm(pb, pf, conv_w[l], b_i_p[l], b_f_p[l], m_norm_g[l].reshape(1, M_WIDTH), l_chunk)
        lam_init = 0.8 - 0.6 * math.exp(-0.3 * l)
        y_c = _diff_attention(pb, lam_p[l], c_norm_g[l].reshape(1, C_WIDTH), lam_init, t_attn)
        h = _out_proj(y_a, y_m, y_c, w_out_b, l, h, ln1_g[l].reshape(1, d), ln1_b[l].reshape(1, d), tm_out)
        h = _ffn(h, w_up_b, w_down_b, l, ln2_g[l].reshape(1, d), ln2_b[l].reshape(1, d), tm_ffn, 1024)
    return h.reshape(batch, s, d)
```

```python
import functools
import math

import jax
import jax.numpy as jnp
import numpy as np
from jax import lax
from jax.experimental import pallas as pl
from jax.experimental.pallas import tpu as pltpu

F32 = jnp.float32
BF16 = jnp.bfloat16
I32 = jnp.int32

D_MODEL = 2048
DEPTH = 4
HEAD_DIM = 128
A_HEADS = 6
A_WIDTH = A_HEADS * HEAD_DIM
IDX_HEADS = 8
IDX_DIM = 64
TOPK_MAX = 256
M_HEADS = 6
M_WIDTH = M_HEADS * HEAD_DIM
CONV_W = 4
C_HEADS = 4
C_QK = 64
C_WIDTH = C_HEADS * 2 * C_QK
D_FF = 4 * D_MODEL
ALPHA = (2.0 * DEPTH) ** 0.25
EPS = 1e-5

SIZES = (A_WIDTH, A_WIDTH, A_WIDTH, IDX_HEADS * IDX_DIM, IDX_DIM, IDX_HEADS,
         M_WIDTH, M_WIDTH, M_WIDTH, M_WIDTH, M_HEADS, M_HEADS,
         C_WIDTH, C_WIDTH, C_WIDTH)
_OFFS = tuple(int(sum(SIZES[:i])) for i in range(len(SIZES)))

LANES = 128
SUBLANES = 8
VMEM_LIMIT_BYTES = 56 * 1024 * 1024
INT_MIN = -2 ** 31
NEG_INF = float("-inf")
LOG2E = math.log2(math.e)

PB_QA, PB_KA, PB_VA, PB_VM = 0, 768, 1536, 2304
PB_QC, PB_KC, PB_VC, PB_QI = 3072, 3584, 4096, 4608
PB_KK = 5120
PB_WIDTH = 5376
PF_QM, PF_KM, PF_OM = 0, 768, 1536
PF_WI, PF_GI, PF_GF = 2304, 2432, 2560
PF_WIDTH = 3072


def _cparams(*sem):
    return pltpu.CompilerParams(dimension_semantics=sem, vmem_limit_bytes=VMEM_LIMIT_BYTES)


def _alibi_slopes(n):
    return [2.0 ** (-8.0 * (h + 1) / n) for h in range(n)]


def _mm_kernel(x_ref, w_ref, o_ref, xb_ref):
    @pl.when(pl.program_id(1) == 0)
    def _():
        xb_ref[...] = x_ref[...].astype(BF16)

    o_ref[...] = jnp.dot(xb_ref[...], w_ref[...], preferred_element_type=F32).astype(o_ref.dtype)


def _matmul(x, w, layer, out_dtype, tm, tn, name):
    m, k = x.shape
    n = w.shape[2]
    return pl.pallas_call(
        _mm_kernel,
        grid=(m // tm, n // tn),
        in_specs=[pl.BlockSpec((tm, k), lambda i, j: (i, 0)),
                  pl.BlockSpec((None, k, tn), lambda i, j: (layer, 0, j))],
        out_specs=pl.BlockSpec((tm, tn), lambda i, j: (i, j)),
        out_shape=jax.ShapeDtypeStruct((m, n), out_dtype),
        scratch_shapes=[pltpu.VMEM((tm, k), BF16)],
        compiler_params=_cparams("arbitrary", "arbitrary"),
        name=name,
    )(x, w)


def _order_key(x):
    bits = lax.bitcast_convert_type(x, I32)
    key = jnp.where(bits < 0, bits ^ jnp.int32(0x7FFFFFFF), bits)
    return jnp.where(bits == jnp.int32(INT_MIN), 0, key)


def _idx_kernel(qi_ref, kk_ref, wi_ref, mask_ref, q8_ref, w8_ref, keys_ref, gm_ref,
                *, tq, tk, nk, ksel, jbits):
    i = pl.program_id(0)
    nck = ((i + 1) * tq + tk - 1) // tk
    classes = gm_ref.shape[0]
    acc_rows = 32

    lane = lax.broadcasted_iota(I32, (tq, LANES), 1)
    lo_half = jnp.where(lane < IDX_DIM, 1.0, 0.0).astype(F32)
    hi_half = 1.0 - lo_half
    for h in range(IDX_HEADS):
        qp = qi_ref[:, (h // 2) * LANES:(h // 2 + 1) * LANES].astype(F32)
        q8_ref[h * tq:(h + 1) * tq, :] = (qp * (lo_half if h % 2 == 0 else hi_half)).astype(BF16)
    w8_ref[...] = wi_ref[...].T[0:SUBLANES, :]
    gm_ref[...] = jnp.full(gm_ref.shape, NEG_INF, F32)

    q_idx = i * tq + lax.broadcasted_iota(I32, (tk, tq), 1)
    k_off = lax.broadcasted_iota(I32, (tk, tq), 0)

    def score_chunk(c, carry):
        kc = kk_ref[pl.ds(pl.multiple_of(c * tk, tk), tk), :]
        y = lax.dot_general(kc, q8_ref[...], (((1,), (1,)), ((), ())), preferred_element_type=F32)
        sc = jnp.maximum(y[:, 0:tq], 0.0) * w8_ref[0:1, :]
        for h in range(1, IDX_HEADS):
            sc = sc + jnp.maximum(y[:, h * tq:(h + 1) * tq], 0.0) * w8_ref[h:h + 1, :]
        causal = c * tk + k_off <= q_idx
        keys_ref[c] = jnp.where(causal, _order_key(sc), jnp.int32(INT_MIN))
        scm = jnp.where(causal, sc, NEG_INF)
        for cls in range(classes):
            g = gm_ref[cls]
            for r in range(cls * SUBLANES, tk, classes * SUBLANES):
                g = jnp.maximum(g, scm[r:r + SUBLANES, :])
            gm_ref[cls] = g
        return carry

    lax.fori_loop(0, nck, score_chunk, 0)

    def count(pred):
        def body(c, acc):
            for r in range(0, tk, acc_rows):
                acc = jnp.where(pred(keys_ref[c, r:r + acc_rows, :], c * tk + r), acc + 1.0, acc)
            return acc
        acc = lax.fori_loop(0, nck, body, jnp.zeros((acc_rows, tq), F32))
        return jnp.sum(acc, axis=0, keepdims=True)

    g_min = gm_ref[0]
    g_max = gm_ref[0]
    for cls in range(1, classes):
        g_min = jnp.minimum(g_min, gm_ref[cls])
        g_max = jnp.maximum(g_max, gm_ref[cls])
    lo_f = jnp.min(g_min, axis=0, keepdims=True)
    hi_f = jnp.max(g_max, axis=0, keepdims=True)
    n_valid = i * tq + lax.broadcasted_iota(I32, (1, tq), 1) + 1
    few = n_valid <= ksel
    lo0 = jnp.where(jnp.logical_or(few, lo_f == NEG_INF), jnp.int32(INT_MIN), _order_key(lo_f))
    hi0 = jnp.where(few, jnp.int32(INT_MIN), _order_key(hi_f))

    def unresolved(st):
        lo, hi = st
        return jnp.max(jnp.where(lo < hi, 1.0, 0.0)) > 0.0

    def bisect(st):
        lo, hi = st
        mid = (lo >> 1) + (hi >> 1) + ((lo | hi) & 1)
        cnt = count(lambda k, r0: k >= mid)
        ge = cnt >= ksel
        exact = cnt == ksel
        return (jnp.where(ge, mid, lo), jnp.where(exact, mid, jnp.where(ge, hi, mid - 1)))

    thr, _ = lax.while_loop(unresolved, bisect, (lo0, hi0))
    n_gt = count(lambda k, r0: k > thr)
    n_ge = count(lambda k, r0: k >= thr)
    need = ksel - n_gt
    excess = jnp.logical_and(thr > jnp.int32(INT_MIN), n_ge > ksel)
    any_excess = jnp.max(jnp.where(excess, 1.0, 0.0)) > 0.0
    sub = lax.broadcasted_iota(I32, (acc_rows, tq), 0)

    def tie_search():
        def jbody(b, jc):
            cand = jc + lax.shift_left(jnp.int32(1), jbits - 1 - b)
            cnt = count(lambda k, r0: jnp.logical_and(k == thr, r0 + sub < cand))
            return jnp.where(cnt <= need, cand, jc)
        return lax.fori_loop(0, jbits, jbody, jnp.zeros((1, tq), I32))

    def write_mask(select):
        for c in range(nk):
            @pl.when(c < nck)
            def _(c=c):
                sel = jnp.where(select(keys_ref[c], c), 1.0, 0.0).astype(F32)
                mask_ref[:, c * tk:(c + 1) * tk] = sel.T.astype(BF16)

            @pl.when(c >= nck)
            def _(c=c):
                mask_ref[:, c * tk:(c + 1) * tk] = jnp.zeros((tq, tk), BF16)

    @pl.when(any_excess)
    def _():
        jcut = tie_search()

        def select(key, c):
            tie = jnp.logical_and(key == thr, c * tk + k_off < jcut)
            return jnp.logical_and(key > jnp.int32(INT_MIN), jnp.logical_or(key > thr, tie))
        write_mask(select)

    @pl.when(jnp.logical_not(any_excess))
    def _():
        floor = jnp.maximum(thr, jnp.int32(INT_MIN + 1))
        write_mask(lambda key, c: key >= floor)


def _dsa_mask(pb, pf, ksel, tq, tk):
    s = pb.shape[0]
    nk = s // tk
    jbits = int(s).bit_length()
    classes = min(32, tk // SUBLANES)
    assert ksel <= SUBLANES * classes, "the threshold bracket relies on that many disjoint key groups"
    kern = functools.partial(_idx_kernel, tq=tq, tk=tk, nk=nk, ksel=ksel, jbits=jbits)
    return pl.pallas_call(
        kern,
        grid=(s // tq,),
        in_specs=[pl.BlockSpec((tq, IDX_HEADS * IDX_DIM), lambda i: (i, PB_QI // (IDX_HEADS * IDX_DIM))),
                  pl.BlockSpec((s, LANES), lambda i: (0, PB_KK // LANES)),
                  pl.BlockSpec((tq, LANES), lambda i: (i, PF_WI // LANES))],
        out_specs=pl.BlockSpec((tq, s), lambda i: (i, 0)),
        out_shape=jax.ShapeDtypeStruct((s, s), BF16),
        scratch_shapes=[pltpu.VMEM((IDX_HEADS * tq, LANES), BF16),
                        pltpu.VMEM((SUBLANES, tq), F32),
                        pltpu.VMEM((nk, tk, tq), I32),
                        pltpu.VMEM((classes, SUBLANES, tq), F32)],
        compiler_params=_cparams("arbitrary"),
        name="dsa_index_mask",
    )(pb, pb, pf)


def _causal_steps(n):
    qi = [i for i in range(n) for _ in range(i + 1)]
    kj = [j for i in range(n) for j in range(i + 1)]
    return jnp.asarray(qi, I32), jnp.asarray(kj, I32)


def _bf16_terms(x):
    out = []
    r = np.float32(x)
    for _ in range(3):
        t = np.float32(np.asarray(r, dtype=jnp.bfloat16))
        out.append(float(t))
        r = np.float32(r - t)
    return out


def _alibi_q_ext(slope, rows):
    s1, s2, s3 = _bf16_terms(slope * LOG2E)
    lane = lax.broadcasted_iota(I32, (rows, LANES), 1)
    ext = jnp.where(lane < 3, s1, jnp.where(lane < 6, s2, jnp.where(lane < 9, s3, 0.0)))
    return ext.astype(BF16)


def _alibi_k_ext(first_pos, rows):
    pos = (first_pos + lax.broadcasted_iota(I32, (rows, LANES), 0)).astype(F32)
    p1, p2, p3 = _split3(pos)
    lane = lax.broadcasted_iota(I32, (rows, LANES), 1)
    sel = lane % 3
    ext = jnp.where(sel == 0, p1.astype(F32), jnp.where(sel == 1, p2.astype(F32), p3.astype(F32)))
    return jnp.where(lane < 9, ext, 0.0).astype(BF16)


def _dsa_attn_kernel(qi_ref, kj_ref, q_ref, k_ref, v_ref, mask_ref, o_ref, qa_ref, mb_ref, s_ref, m_ref,
                     acc_ref, *, tq, tk):
    t = pl.program_id(0)
    i = qi_ref[t]
    j = kj_ref[t]
    slopes = _alibi_slopes(A_HEADS)
    reps = tk // LANES

    @pl.when(j == 0)
    def _():
        m_ref[...] = jnp.full(m_ref.shape, NEG_INF, F32)
        acc_ref[...] = jnp.zeros(acc_ref.shape, F32)
        for h in range(A_HEADS):
            hs = slice(h * HEAD_DIM, (h + 1) * HEAD_DIM)
            qh = (q_ref[:, hs].astype(F32) * (HEAD_DIM ** -0.5 * LOG2E)).astype(BF16)
            qa_ref[h] = jnp.concatenate([qh, _alibi_q_ext(slopes[h], tq)], axis=1)

    mf = mask_ref[...].astype(F32)
    mb_ref[...] = jnp.where(mf > 0.0, mf - 1.0, NEG_INF)
    k_ext = _alibi_k_ext(j * tk - i * tq, tk)
    ones = jnp.ones((tk, LANES), BF16)

    def scores(h):
        hs = slice(h * HEAD_DIM, (h + 1) * HEAD_DIM)
        ka = jnp.concatenate([k_ref[:, hs], k_ext], axis=1)
        s = lax.dot_general(qa_ref[h], ka, (((1,), (1,)), ((), ())), preferred_element_type=F32)
        s = s + mb_ref[...]
        s_ref[h % 2] = s
        m_prev = m_ref[h]
        m_cur = jnp.maximum(m_prev, jnp.max(s, axis=1, keepdims=True))
        m_ref[h] = m_cur
        m_safe = jnp.where(m_cur == NEG_INF, 0.0, m_cur)
        return m_safe, jnp.exp2(m_prev - m_safe)

    def accumulate(h, m_safe, a):
        hs = slice(h * HEAD_DIM, (h + 1) * HEAD_DIM)
        p = jnp.exp2(s_ref[h % 2] - jnp.tile(m_safe, (1, reps)))
        va = jnp.concatenate([v_ref[:, hs], ones], axis=1)
        acc_ref[h] = jnp.tile(a, (1, 2)) * acc_ref[h] + jnp.dot(p.astype(BF16), va,
                                                                 preferred_element_type=F32)

    stats = scores(0)
    for h in range(A_HEADS):
        nxt = scores(h + 1) if h + 1 < A_HEADS else None
        accumulate(h, *stats)
        stats = nxt

    @pl.when(j == i)
    def _():
        for h in range(A_HEADS):
            hs = slice(h * HEAD_DIM, (h + 1) * HEAD_DIM)
            o_ref[:, hs] = (acc_ref[h, :, 0:HEAD_DIM] / acc_ref[h, :, HEAD_DIM:2 * HEAD_DIM]).astype(o_ref.dtype)


def _dsa_attention(pb, mask, tq):
    s = pb.shape[0]
    kern = functools.partial(_dsa_attn_kernel, tq=tq, tk=tq)
    qi, kj = _causal_steps(s // tq)
    grid_spec = pltpu.PrefetchScalarGridSpec(
        num_scalar_prefetch=2,
        grid=(int(qi.shape[0]),),
        in_specs=[pl.BlockSpec((tq, A_WIDTH), lambda t, qi, kj: (qi[t], PB_QA // A_WIDTH)),
                  pl.BlockSpec((tq, A_WIDTH), lambda t, qi, kj: (kj[t], PB_KA // A_WIDTH)),
                  pl.BlockSpec((tq, A_WIDTH), lambda t, qi, kj: (kj[t], PB_VA // A_WIDTH)),
                  pl.BlockSpec((tq, tq), lambda t, qi, kj: (qi[t], kj[t]))],
        out_specs=pl.BlockSpec((tq, A_WIDTH), lambda t, qi, kj: (qi[t], 0)),
        scratch_shapes=[pltpu.VMEM((A_HEADS, tq, 2 * HEAD_DIM), BF16),
                        pltpu.VMEM((tq, tq), F32),
                        pltpu.VMEM((2, tq, tq), F32),
                        pltpu.VMEM((A_HEADS, tq, LANES), F32),
                        pltpu.VMEM((A_HEADS, tq, 2 * HEAD_DIM), F32)])
    return pl.pallas_call(
        kern,
        grid_spec=grid_spec,
        out_shape=jax.ShapeDtypeStruct((s, A_WIDTH), BF16),
        compiler_params=_cparams("arbitrary"),
        name="dsa_attention",
    )(qi, kj, pb, pb, pb, mask)


def _diff_attn_kernel(qi_ref, kj_ref, q_ref, k_ref, v_ref, lam_ref, g_ref, o_ref, qa_ref, s_ref, m_ref,
                      acc_ref, *, tq, tk, lam_init):
    t = pl.program_id(0)
    i = qi_ref[t]
    j = kj_ref[t]
    slopes = _alibi_slopes(C_HEADS)
    reps = tk // LANES

    @pl.when(j == 0)
    def _():
        m_ref[...] = jnp.full(m_ref.shape, NEG_INF, F32)
        acc_ref[...] = jnp.zeros(acc_ref.shape, F32)
        lane = lax.broadcasted_iota(I32, (tq, LANES), 1)
        for h in range(C_HEADS):
            qf = q_ref[:, h * LANES:(h + 1) * LANES].astype(F32) * (C_QK ** -0.5 * LOG2E)
            ext = _alibi_q_ext(slopes[h], tq)
            qa_ref[2 * h] = jnp.concatenate([jnp.where(lane < C_QK, qf, 0.0).astype(BF16), ext], axis=1)
            qa_ref[2 * h + 1] = jnp.concatenate([jnp.where(lane < C_QK, 0.0, qf).astype(BF16), ext], axis=1)

    def step(diagonal):
        k_ext = _alibi_k_ext(j * tk - i * tq, tk)
        ones = jnp.ones((tk, LANES), BF16)
        if diagonal:
            causal = lax.broadcasted_iota(I32, (tq, tk), 1) <= lax.broadcasted_iota(I32, (tq, tk), 0)

        def scores(u):
            h = u // 2
            ka = jnp.concatenate([k_ref[:, h * LANES:(h + 1) * LANES], k_ext], axis=1)
            s = lax.dot_general(qa_ref[u], ka, (((1,), (1,)), ((), ())), preferred_element_type=F32)
            if diagonal:
                s = jnp.where(causal, s, NEG_INF)
            s_ref[u % 2] = s
            m_prev = m_ref[u]
            m_cur = jnp.maximum(m_prev, jnp.max(s, axis=1, keepdims=True))
            m_ref[u] = m_cur
            return m_cur, jnp.exp2(m_prev - m_cur)

        def accumulate(u, m_cur, a):
            h = u // 2
            p = jnp.exp2(s_ref[u % 2] - jnp.tile(m_cur, (1, reps)))
            va = jnp.concatenate([v_ref[:, h * LANES:(h + 1) * LANES], ones], axis=1)
            acc_ref[u] = jnp.tile(a, (1, 2)) * acc_ref[u] + jnp.dot(p.astype(BF16), va,
                                                                     preferred_element_type=F32)

        stats = scores(0)
        for u in range(2 * C_HEADS):
            nxt = scores(u + 1) if u + 1 < 2 * C_HEADS else None
            accumulate(u, *stats)
            stats = nxt

    @pl.when(j < i)
    def _():
        step(False)

    @pl.when(j == i)
    def _():
        step(True)
        lp = lam_ref[...]
        lam = (jnp.exp(jnp.sum(lp[0:1] * lp[1:2], axis=1, keepdims=True))
               - jnp.exp(jnp.sum(lp[2:3] * lp[3:4], axis=1, keepdims=True)) + lam_init)
        for h in range(C_HEADS):
            hs = slice(h * LANES, (h + 1) * LANES)
            o = (acc_ref[2 * h, :, 0:LANES] / acc_ref[2 * h, :, LANES:2 * LANES]
                 - lam * (acc_ref[2 * h + 1, :, 0:LANES] / acc_ref[2 * h + 1, :, LANES:2 * LANES]))
            ms = jnp.mean(o * o, axis=1, keepdims=True)
            o_ref[:, hs] = (o * lax.rsqrt(ms + EPS) * g_ref[:, hs] * (1.0 - lam_init)).astype(o_ref.dtype)


def _diff_attention(pb, lam_params, g, lam_init, tq):
    s = pb.shape[0]
    kern = functools.partial(_diff_attn_kernel, tq=tq, tk=tq, lam_init=lam_init)
    qi, kj = _causal_steps(s // tq)
    grid_spec = pltpu.PrefetchScalarGridSpec(
        num_scalar_prefetch=2,
        grid=(int(qi.shape[0]),),
        in_specs=[pl.BlockSpec((tq, C_WIDTH), lambda t, qi, kj: (qi[t], PB_QC // C_WIDTH)),
                  pl.BlockSpec((tq, C_WIDTH), lambda t, qi, kj: (kj[t], PB_KC // C_WIDTH)),
                  pl.BlockSpec((tq, C_WIDTH), lambda t, qi, kj: (kj[t], PB_VC // C_WIDTH)),
                  pl.BlockSpec((4, C_QK), lambda t, qi, kj: (0, 0)),
                  pl.BlockSpec((1, C_WIDTH), lambda t, qi, kj: (0, 0))],
        out_specs=pl.BlockSpec((tq, C_WIDTH), lambda t, qi, kj: (qi[t], 0)),
        scratch_shapes=[pltpu.VMEM((2 * C_HEADS, tq, 2 * LANES), BF16),
                        pltpu.VMEM((2, tq, tq), F32),
                        pltpu.VMEM((2 * C_HEADS, tq, LANES), F32),
                        pltpu.VMEM((2 * C_HEADS, tq, 2 * LANES), F32)])
    return pl.pallas_call(
        kern,
        grid_spec=grid_spec,
        out_shape=jax.ShapeDtypeStruct((s, C_WIDTH), BF16),
        compiler_params=_cparams("arbitrary"),
        name="diff_attention",
    )(qi, kj, pb, pb, pb, lam_params, g)


def _split3(x):
    x1 = x.astype(BF16)
    r1 = x - x1.astype(F32)
    x2 = r1.astype(BF16)
    x3 = (r1 - x2.astype(F32)).astype(BF16)
    return x1, x2, x3


def _mlstm_kernel(q_ref, k_ref, qh_ref, kh_ref, v_ref, o_ref, gi_ref, gf_ref, cw_ref, bi_ref, bf_ref,
                  g_ref, y_ref, xq_ref, xk_ref, c_ref, n_ref, m_ref, *, L):
    c = pl.program_id(0)

    @pl.when(c == 0)
    def _():
        c_ref[...] = jnp.zeros(c_ref.shape, F32)
        n_ref[...] = jnp.zeros(n_ref.shape, F32)
        m_ref[...] = jnp.zeros(m_ref.shape, F32)

    first = (c > 0).astype(F32)
    xq_ref[0:SUBLANES, :] = qh_ref[...] * first
    xq_ref[SUBLANES:SUBLANES + L, :] = q_ref[...]
    xk_ref[0:SUBLANES, :] = kh_ref[...] * first
    xk_ref[SUBLANES:SUBLANES + L, :] = k_ref[...]
    qc = jnp.zeros((L, M_WIDTH), F32)
    kc = jnp.zeros((L, M_WIDTH), F32)
    for t in range(CONV_W):
        off = SUBLANES - (CONV_W - 1) + t
        qc = qc + xq_ref[off:off + L, :] * cw_ref[t:t + 1, 0:M_WIDTH]
        kc = kc + xk_ref[off:off + L, :] * cw_ref[t:t + 1, M_WIDTH:2 * M_WIDTH]
    qc = qc * jax.nn.sigmoid(qc)
    kc = kc * jax.nn.sigmoid(kc) * (HEAD_DIM ** -0.5)

    li = gi_ref[...] + bi_ref[...]
    fp = gf_ref[...] + bf_ref[...]
    lf = jnp.minimum(fp, 0.0) - jnp.log(1.0 + jnp.exp(-jnp.abs(fp)))
    r_i = lax.broadcasted_iota(I32, (L, L), 0)
    c_i = lax.broadcasted_iota(I32, (L, L), 1)
    tril = c_i <= r_i
    tril_b = jnp.where(tril, 1.0, 0.0).astype(BF16)
    a = jnp.zeros((L, LANES), F32)
    for part in _split3(lf):
        a = a + jnp.dot(tril_b, part, preferred_element_type=F32)
    b = li - a
    g_tot = a[L - 1:L, :]
    b_t = b.T

    m_all = m_ref[...]
    m_new_all = m_all
    for h in range(M_HEADS):
        hs = slice(h * HEAD_DIM, (h + 1) * HEAD_DIM)
        qh = qc[:, hs]
        kh = kc[:, hs]
        vh = v_ref[:, hs]
        qb = qh.astype(BF16)
        kb = kh.astype(BF16)
        m_prev = m_all[:, h:h + 1]
        d = jnp.where(tril, b_t[h:h + 1, :], NEG_INF)
        mm = jnp.maximum(m_prev, jnp.max(d, axis=1, keepdims=True))
        w_intra = jnp.exp(d - mm)
        w_inter = jnp.exp(m_prev - mm)
        qk = lax.dot_general(qb, kb, (((1,), (1,)), ((), ())), preferred_element_type=F32) * w_intra
        c_h = c_ref[h]
        num = (jnp.dot(qk.astype(BF16), vh, preferred_element_type=F32)
               + w_inter * jnp.dot(qb, c_h.astype(BF16), preferred_element_type=F32))
        den = (jnp.sum(qk, axis=1, keepdims=True)
               + w_inter * jnp.sum(qh * n_ref[h:h + 1, :], axis=1, keepdims=True))
        m_row = a[:, h:h + 1] + mm
        hh = num / jnp.maximum(jnp.abs(den), jnp.exp(-m_row))
        mm_last = mm[L - 1:L, :]
        ws = jnp.exp(b[:, h:h + 1] - mm_last)
        wc = jnp.exp(m_prev - mm_last)
        c_ref[h] = wc * c_h + lax.dot_general(kb, (ws * vh.astype(F32)).astype(BF16),
                                              (((0,), (0,)), ((), ())), preferred_element_type=F32)
        n_ref[h:h + 1, :] = wc * n_ref[h:h + 1, :] + jnp.sum(ws * kh, axis=0, keepdims=True)
        lane = lax.broadcasted_iota(I32, (1, LANES), 1)
        m_new_all = jnp.where(lane == h, g_tot + mm_last, m_new_all)
        ms = jnp.mean(hh * hh, axis=1, keepdims=True)
        hn = hh * lax.rsqrt(ms + EPS) * g_ref[:, hs]
        y_ref[:, hs] = (jax.nn.sigmoid(o_ref[:, hs]) * hn).astype(y_ref.dtype)
    m_ref[...] = m_new_all


def _mlstm(pb, pf, conv_w, b_i, b_f, g, L):
    s = pb.shape[0]
    kern = functools.partial(_mlstm_kernel, L=L)
    halo = lambda c: jnp.maximum(c * (L // SUBLANES) - 1, 0)
    return pl.pallas_call(
        kern,
        grid=(s // L,),
        in_specs=[pl.BlockSpec((L, M_WIDTH), lambda c: (c, PF_QM // M_WIDTH)),
                  pl.BlockSpec((L, M_WIDTH), lambda c: (c, PF_KM // M_WIDTH)),
                  pl.BlockSpec((SUBLANES, M_WIDTH), lambda c: (halo(c), PF_QM // M_WIDTH)),
                  pl.BlockSpec((SUBLANES, M_WIDTH), lambda c: (halo(c), PF_KM // M_WIDTH)),
                  pl.BlockSpec((L, M_WIDTH), lambda c: (c, PB_VM // M_WIDTH)),
                  pl.BlockSpec((L, M_WIDTH), lambda c: (c, PF_OM // M_WIDTH)),
                  pl.BlockSpec((L, LANES), lambda c: (c, PF_GI // LANES)),
                  pl.BlockSpec((L, LANES), lambda c: (c, PF_GF // LANES)),
                  pl.BlockSpec((CONV_W, 2 * M_WIDTH), lambda c: (0, 0)),
                  pl.BlockSpec((1, LANES), lambda c: (0, 0)),
                  pl.BlockSpec((1, LANES), lambda c: (0, 0)),
                  pl.BlockSpec((1, M_WIDTH), lambda c: (0, 0))],
        out_specs=pl.BlockSpec((L, M_WIDTH), lambda c: (c, 0)),
        out_shape=jax.ShapeDtypeStruct((s, M_WIDTH), BF16),
        scratch_shapes=[pltpu.VMEM((L + SUBLANES, M_WIDTH), F32),
                        pltpu.VMEM((L + SUBLANES, M_WIDTH), F32),
                        pltpu.VMEM((M_HEADS, HEAD_DIM, HEAD_DIM), F32),
                        pltpu.VMEM((SUBLANES, HEAD_DIM), F32),
                        pltpu.VMEM((1, LANES), F32)],
        compiler_params=_cparams("arbitrary"),
        name="mlstm",
    )(pf, pf, pf, pf, pb, pf, pf, pf, conv_w, b_i, b_f, g)


def _layer_norm(z, g, b):
    mu = jnp.mean(z, axis=-1, keepdims=True)
    zc = z - mu
    var = jnp.mean(zc * zc, axis=-1, keepdims=True)
    return zc * lax.rsqrt(var + EPS) * g + b


def _out_kernel(ya_ref, ym_ref, yc_ref, w_ref, x_ref, g_ref, b_ref, o_ref):
    acc = jnp.dot(ya_ref[...], w_ref[0:A_WIDTH, :], preferred_element_type=F32)
    acc = acc + jnp.dot(ym_ref[...], w_ref[A_WIDTH:A_WIDTH + M_WIDTH, :], preferred_element_type=F32)
    acc = acc + jnp.dot(yc_ref[...], w_ref[A_WIDTH + M_WIDTH:D_MODEL, :], preferred_element_type=F32)
    o_ref[...] = _layer_norm(ALPHA * x_ref[...] + acc, g_ref[...], b_ref[...])


def _out_proj(ya, ym, yc, w, layer, x, g, b, tm):
    s = x.shape[0]
    return pl.pallas_call(
        _out_kernel,
        grid=(s // tm,),
        in_specs=[pl.BlockSpec((tm, A_WIDTH), lambda i: (i, 0)),
                  pl.BlockSpec((tm, M_WIDTH), lambda i: (i, 0)),
                  pl.BlockSpec((tm, C_WIDTH), lambda i: (i, 0)),
                  pl.BlockSpec((None, D_MODEL, D_MODEL), lambda i: (layer, 0, 0)),
                  pl.BlockSpec((tm, D_MODEL), lambda i: (i, 0)),
                  pl.BlockSpec((1, D_MODEL), lambda i: (0, 0)),
                  pl.BlockSpec((1, D_MODEL), lambda i: (0, 0))],
        out_specs=pl.BlockSpec((tm, D_MODEL), lambda i: (i, 0)),
        out_shape=jax.ShapeDtypeStruct((s, D_MODEL), F32),
        compiler_params=_cparams("arbitrary"),
        name="out_proj_ln",
    )(ya, ym, yc, w, x, g, b)


def _ffn_kernel(x_ref, wu_ref, wd_ref, g_ref, b_ref, o_ref, xb_ref, acc_ref):
    f = pl.program_id(1)

    @pl.when(f == 0)
    def _():
        xb_ref[...] = x_ref[...].astype(BF16)
        acc_ref[...] = jnp.zeros(acc_ref.shape, F32)

    hdn = jnp.maximum(jnp.dot(xb_ref[...], wu_ref[...], preferred_element_type=F32), 0.0)
    acc_ref[...] += jnp.dot((hdn * hdn).astype(BF16), wd_ref[...], preferred_element_type=F32)

    @pl.when(f == pl.num_programs(1) - 1)
    def _():
        o_ref[...] = _layer_norm(ALPHA * x_ref[...] + acc_ref[...], g_ref[...], b_ref[...])


def _ffn(x, wu, wd, layer, g, b, tm, tf):
    s = x.shape[0]
    return pl.pallas_call(
        _ffn_kernel,
        grid=(s // tm, D_FF // tf),
        in_specs=[pl.BlockSpec((tm, D_MODEL), lambda i, f: (i, 0)),
                  pl.BlockSpec((None, D_MODEL, tf), lambda i, f: (layer, 0, f)),
                  pl.BlockSpec((None, tf, D_MODEL), lambda i, f: (layer, f, 0)),
                  pl.BlockSpec((1, D_MODEL), lambda i, f: (0, 0)),
                  pl.BlockSpec((1, D_MODEL), lambda i, f: (0, 0))],
        out_specs=pl.BlockSpec((tm, D_MODEL), lambda i, f: (i, 0)),
        out_shape=jax.ShapeDtypeStruct((s, D_MODEL), F32),
        scratch_shapes=[pltpu.VMEM((tm, D_MODEL), BF16),
                        pltpu.VMEM((tm, D_MODEL), F32)],
        compiler_params=_cparams("arbitrary", "arbitrary"),
        name="ffn_ln",
    )(x, wu, wd, g, b)


def _seg(w, i):
    return w[..., _OFFS[i]:_OFFS[i] + SIZES[i]]


def _pad_cols(w, width):
    return jnp.pad(w, [(0, 0)] * (w.ndim - 1) + [(0, width - w.shape[-1])])


def _layout_w_in(w_in):
    w_in = w_in.astype(BF16)
    (q_a, k_a, v_a, q_i, k_i, w_i, q_m, k_m, v_m, o_m, i_m, f_m, q_c, k_c, v_c) = [
        _seg(w_in, i) for i in range(len(SIZES))]
    wb = jnp.concatenate([q_a, k_a, v_a, v_m, q_c, k_c, v_c, q_i, k_i, k_i], axis=-1)
    wf = jnp.concatenate([q_m, k_m, o_m, _pad_cols(w_i, LANES), _pad_cols(i_m, LANES),
                          _pad_cols(f_m, LANES)], axis=-1)
    return _pad_cols(wb, PB_WIDTH), _pad_cols(wf, PF_WIDTH)


def _tile(n, pref):
    t = min(n, pref)
    assert n % t == 0, (n, t)
    return t


def kernel(x, w_in, conv_m, b_i, b_f, m_norm_g, lam_q1, lam_k1, lam_q2, lam_k2, c_norm_g, w_out,
           ln1_g, ln1_b, w_up, w_down, ln2_g, ln2_b):
    batch, s, d = x.shape
    assert batch == 1 and d == D_MODEL
    ksel = min(TOPK_MAX, s // 4)
    wb_all, wf_all = _layout_w_in(w_in)
    w_out_b = w_out.astype(BF16)
    w_up_b = w_up.astype(BF16)
    w_down_b = w_down.astype(BF16)
    conv_w = conv_m.reshape(DEPTH, CONV_W, 2 * M_WIDTH)
    b_i_p = _pad_cols(b_i, LANES).reshape(DEPTH, 1, LANES)
    b_f_p = _pad_cols(b_f, LANES).reshape(DEPTH, 1, LANES)
    lam_p = jnp.stack([lam_q1, lam_k1, lam_q2, lam_k2], axis=1)

    tm_proj = _tile(s, 1024)
    t_attn = _tile(s, 512)
    tq_idx = _tile(s, 256)
    tk_idx = _tile(s, 512)
    l_chunk = _tile(s, 256)
    tm_out = _tile(s, 512)
    tm_ffn = _tile(s, 512)

    h = x.reshape(s, d)
    for l in range(DEPTH):
        pb = _matmul(h, wb_all, l, BF16, tm_proj, 768, "proj_bf16")
        pf = _matmul(h, wf_all, l, F32, tm_proj, 768, "proj_f32")
        mask = _dsa_mask(pb, pf, ksel, tq_idx, tk_idx)
        y_a = _dsa_attention(pb, mask, t_attn)
        y_m = _mlstm(pb, pf, conv_w[l], b_i_p[l], b_f_p[l], m_norm_g[l].reshape(1, M_WIDTH), l_chunk)
        lam_init = 0.8 - 0.6 * math.exp(-0.3 * l)
        y_c = _diff_attention(pb, lam_p[l], c_norm_g[l].reshape(1, C_WIDTH), lam_init, t_attn)
        h = _out_proj(y_a, y_m, y_c, w_out_b, l, h, ln1_g[l].reshape(1, d), ln1_b[l].reshape(1, d), tm_out)
        h = _ffn(h, w_up_b, w_down_b, l, ln2_g[l].reshape(1, d), ln2_b[l].reshape(1, d), tm_ffn, 1024)
    return h.reshape(batch, s, d)
```

```python
import functools
import math

import jax
import jax.numpy as jnp
import numpy as np
from jax import lax
from jax.experimental import pallas as pl
from jax.experimental.pallas import tpu as pltpu

F32 = jnp.float32
BF16 = jnp.bfloat16
I32 = jnp.int32

D_MODEL = 2048
DEPTH = 4
HEAD_DIM = 128
A_HEADS = 6
A_WIDTH = A_HEADS * HEAD_DIM
IDX_HEADS = 8
IDX_DIM = 64
TOPK_MAX = 256
M_HEADS = 6
M_WIDTH = M_HEADS * HEAD_DIM
CONV_W = 4
C_HEADS = 4
C_QK = 64
C_WIDTH = C_HEADS * 2 * C_QK
D_FF = 4 * D_MODEL
ALPHA = (2.0 * DEPTH) ** 0.25
EPS = 1e-5

SIZES = (A_WIDTH, A_WIDTH, A_WIDTH, IDX_HEADS * IDX_DIM, IDX_DIM, IDX_HEADS,
         M_WIDTH, M_WIDTH, M_WIDTH, M_WIDTH, M_HEADS, M_HEADS,
         C_WIDTH, C_WIDTH, C_WIDTH)
_OFFS = tuple(int(sum(SIZES[:i])) for i in range(len(SIZES)))

LANES = 128
SUBLANES = 8
VMEM_LIMIT_BYTES = 56 * 1024 * 1024
INT_MIN = -2 ** 31
NEG_INF = float("-inf")
LOG2E = math.log2(math.e)

PB_QA, PB_KA, PB_VA, PB_VM = 0, 768, 1536, 2304
PB_QC, PB_KC, PB_VC, PB_QI = 3072, 3584, 4096, 4608
PB_KK = 5120
PB_WIDTH = 5376
PF_QM, PF_KM, PF_OM = 0, 768, 1536
PF_WI, PF_GI, PF_GF = 2304, 2432, 2560
PF_WIDTH = 3072


def _cparams(*sem):
    return pltpu.CompilerParams(dimension_semantics=sem, vmem_limit_bytes=VMEM_LIMIT_BYTES)


def _alibi_slopes(n):
    return [2.0 ** (-8.0 * (h + 1) / n) for h in range(n)]


def _mm_kernel(x_ref, w_ref, o_ref, xb_ref):
    @pl.when(pl.program_id(1) == 0)
    def _():
        xb_ref[...] = x_ref[...].astype(BF16)

    o_ref[...] = jnp.dot(xb_ref[...], w_ref[...], preferred_element_type=F32).astype(o_ref.dtype)


def _matmul(x, w, layer, out_dtype, tm, tn, name):
    m, k = x.shape
    n = w.shape[2]
    return pl.pallas_call(
        _mm_kernel,
        grid=(m // tm, n // tn),
        in_specs=[pl.BlockSpec((tm, k), lambda i, j: (i, 0)),
                  pl.BlockSpec((None, k, tn), lambda i, j: (layer, 0, j))],
        out_specs=pl.BlockSpec((tm, tn), lambda i, j: (i, j)),
        out_shape=jax.ShapeDtypeStruct((m, n), out_dtype),
        scratch_shapes=[pltpu.VMEM((tm, k), BF16)],
        compiler_params=_cparams("arbitrary", "arbitrary"),
        name=name,
    )(x, w)


def _order_key(x):
    bits = lax.bitcast_convert_type(x, I32)
    key = jnp.where(bits < 0, bits ^ jnp.int32(0x7FFFFFFF), bits)
    return jnp.where(bits == jnp.int32(INT_MIN), 0, key)


def _idx_kernel(qi_ref, kk_ref, wi_ref, mask_ref, q8_ref, w8_ref, keys_ref, gm_ref,
                *, tq, tk, nk, ksel, jbits):
    i = pl.program_id(0)
    nck = ((i + 1) * tq + tk - 1) // tk
    classes = gm_ref.shape[0]
    acc_rows = 32

    lane = lax.broadcasted_iota(I32, (tq, LANES), 1)
    lo_half = jnp.where(lane < IDX_DIM, 1.0, 0.0).astype(F32)
    hi_half = 1.0 - lo_half
    for h in range(IDX_HEADS):
        qp = qi_ref[:, (h // 2) * LANES:(h // 2 + 1) * LANES].astype(F32)
        q8_ref[h * tq:(h + 1) * tq, :] = (qp * (lo_half if h % 2 == 0 else hi_half)).astype(BF16)
    w8_ref[...] = wi_ref[...].T[0:SUBLANES, :]
    gm_ref[...] = jnp.full(gm_ref.shape, NEG_INF, F32)

    q_idx = i * tq + lax.broadcasted_iota(I32, (tk, tq), 1)
    k_off = lax.broadcasted_iota(I32, (tk, tq), 0)

    def score_chunk(c, carry):
        kc = kk_ref[pl.ds(pl.multiple_of(c * tk, tk), tk), :]
        y = lax.dot_general(kc, q8_ref[...], (((1,), (1,)), ((), ())), preferred_element_type=F32)
        sc = jnp.maximum(y[:, 0:tq], 0.0) * w8_ref[0:1, :]
        for h in range(1, IDX_HEADS):
            sc = sc + jnp.maximum(y[:, h * tq:(h + 1) * tq], 0.0) * w8_ref[h:h + 1, :]
        causal = c * tk + k_off <= q_idx
        keys_ref[c] = jnp.where(causal, _order_key(sc), jnp.int32(INT_MIN))
        scm = jnp.where(causal, sc, NEG_INF)
        for cls in range(classes):
            g = gm_ref[cls]
            for r in range(cls * SUBLANES, tk, classes * SUBLANES):
                g = jnp.maximum(g, scm[r:r + SUBLANES, :])
            gm_ref[cls] = g
        return carry

    lax.fori_loop(0, nck, score_chunk, 0)

    def count(pred):
        def body(c, acc):
            for r in range(0, tk, acc_rows):
                acc = jnp.where(pred(keys_ref[c, r:r + acc_rows, :], c * tk + r), acc + 1.0, acc)
            return acc
        acc = lax.fori_loop(0, nck, body, jnp.zeros((acc_rows, tq), F32))
        return jnp.sum(acc, axis=0, keepdims=True)

    g_min = gm_ref[0]
    g_max = gm_ref[0]
    for cls in range(1, classes):
        g_min = jnp.minimum(g_min, gm_ref[cls])
        g_max = jnp.maximum(g_max, gm_ref[cls])
    lo_f = jnp.min(g_min, axis=0, keepdims=True)
    hi_f = jnp.max(g_max, axis=0, keepdims=True)
    n_valid = i * tq + lax.broadcasted_iota(I32, (1, tq), 1) + 1
    few = n_valid <= ksel
    lo0 = jnp.where(jnp.logical_or(few, lo_f == NEG_INF), jnp.int32(INT_MIN), _order_key(lo_f))
    hi0 = jnp.where(few, jnp.int32(INT_MIN), _order_key(hi_f))

    def unresolved(st):
        lo, hi = st
        return jnp.max(jnp.where(lo < hi, 1.0, 0.0)) > 0.0

    def bisect(st):
        lo, hi = st
        mid = (lo >> 1) + (hi >> 1) + ((lo | hi) & 1)
        cnt = count(lambda k, r0: k >= mid)
        ge = cnt >= ksel
        exact = cnt == ksel
        return (jnp.where(ge, mid, lo), jnp.where(exact, mid, jnp.where(ge, hi, mid - 1)))

    thr, _ = lax.while_loop(unresolved, bisect, (lo0, hi0))
    n_gt = count(lambda k, r0: k > thr)
    n_ge = count(lambda k, r0: k >= thr)
    need = ksel - n_gt
    excess = jnp.logical_and(thr > jnp.int32(INT_MIN), n_ge > ksel)
    any_excess = jnp.max(jnp.where(excess, 1.0, 0.0)) > 0.0
    sub = lax.broadcasted_iota(I32, (acc_rows, tq), 0)

    def tie_search():
        def jbody(b, jc):
            cand = jc + lax.shift_left(jnp.int32(1), jbits - 1 - b)
            cnt = count(lambda k, r0: jnp.logical_and(k == thr, r0 + sub < cand))
            return jnp.where(cnt <= need, cand, jc)
        return lax.fori_loop(0, jbits, jbody, jnp.zeros((1, tq), I32))

    def write_mask(select):
        for c in range(nk):
            @pl.when(c < nck)
            def _(c=c):
                sel = jnp.where(select(keys_ref[c], c), 1.0, 0.0).astype(F32)
                mask_ref[:, c * tk:(c + 1) * tk] = sel.T.astype(BF16)

            @pl.when(c >= nck)
            def _(c=c):
                mask_ref[:, c * tk:(c + 1) * tk] = jnp.zeros((tq, tk), BF16)

    @pl.when(any_excess)
    def _():
        jcut = tie_search()

        def select(key, c):
            tie = jnp.logical_and(key == thr, c * tk + k_off < jcut)
            return jnp.logical_and(key > jnp.int32(INT_MIN), jnp.logical_or(key > thr, tie))
        write_mask(select)

    @pl.when(jnp.logical_not(any_excess))
    def _():
        floor = jnp.maximum(thr, jnp.int32(INT_MIN + 1))
        write_mask(lambda key, c: key >= floor)


def _dsa_mask(pb, pf, ksel, tq, tk):
    s = pb.shape[0]
    nk = s // tk
    jbits = int(s).bit_length()
    classes = min(32, tk // SUBLANES)
    assert ksel <= SUBLANES * classes, "the threshold bracket relies on that many disjoint key groups"
    kern = functools.partial(_idx_kernel, tq=tq, tk=tk, nk=nk, ksel=ksel, jbits=jbits)
    return pl.pallas_call(
        kern,
        grid=(s // tq,),
        in_specs=[pl.BlockSpec((tq, IDX_HEADS * IDX_DIM), lambda i: (i, PB_QI // (IDX_HEADS * IDX_DIM))),
                  pl.BlockSpec((s, LANES), lambda i: (0, PB_KK // LANES)),
                  pl.BlockSpec((tq, LANES), lambda i: (i, PF_WI // LANES))],
        out_specs=pl.BlockSpec((tq, s), lambda i: (i, 0)),
        out_shape=jax.ShapeDtypeStruct((s, s), BF16),
        scratch_shapes=[pltpu.VMEM((IDX_HEADS * tq, LANES), BF16),
                        pltpu.VMEM((SUBLANES, tq), F32),
                        pltpu.VMEM((nk, tk, tq), I32),
                        pltpu.VMEM((classes, SUBLANES, tq), F32)],
        compiler_params=_cparams("arbitrary"),
        name="dsa_index_mask",
    )(pb, pb, pf)


def _causal_steps(n):
    qi = [i for i in range(n) for _ in range(i + 1)]
    kj = [j for i in range(n) for j in range(i + 1)]
    return jnp.asarray(qi, I32), jnp.asarray(kj, I32)


def _bf16_terms(x):
    out = []
    r = np.float32(x)
    for _ in range(3):
        t = np.float32(np.asarray(r, dtype=jnp.bfloat16))
        out.append(float(t))
        r = np.float32(r - t)
    return out


def _alibi_q_ext(slope, rows):
    s1, s2, s3 = _bf16_terms(slope * LOG2E)
    lane = lax.broadcasted_iota(I32, (rows, LANES), 1)
    ext = jnp.where(lane < 3, s1, jnp.where(lane < 6, s2, jnp.where(lane < 9, s3, 0.0)))
    return ext.astype(BF16)


def _alibi_k_ext(first_pos, rows):
    pos = (first_pos + lax.broadcasted_iota(I32, (rows, LANES), 0)).astype(F32)
    p1, p2, p3 = _split3(pos)
    lane = lax.broadcasted_iota(I32, (rows, LANES), 1)
    sel = lane % 3
    ext = jnp.where(sel == 0, p1.astype(F32), jnp.where(sel == 1, p2.astype(F32), p3.astype(F32)))
    return jnp.where(lane < 9, ext, 0.0).astype(BF16)


def _dsa_attn_kernel(qi_ref, kj_ref, q_ref, k_ref, v_ref, mask_ref, o_ref, qa_ref, mb_ref, s_ref, m_ref,
                     acc_ref, *, tq, tk):
    t = pl.program_id(0)
    i = qi_ref[t]
    j = kj_ref[t]
    slopes = _alibi_slopes(A_HEADS)
    reps = tk // LANES

    @pl.when(j == 0)
    def _():
        m_ref[...] = jnp.full(m_ref.shape, NEG_INF, F32)
        acc_ref[...] = jnp.zeros(acc_ref.shape, F32)
        for h in range(A_HEADS):
            hs = slice(h * HEAD_DIM, (h + 1) * HEAD_DIM)
            qh = (q_ref[:, hs].astype(F32) * (HEAD_DIM ** -0.5 * LOG2E)).astype(BF16)
            qa_ref[h] = jnp.concatenate([qh, _alibi_q_ext(slopes[h], tq)], axis=1)

    mf = mask_ref[...].astype(F32)
    mb_ref[...] = jnp.where(mf > 0.0, mf - 1.0, NEG_INF)
    k_ext = _alibi_k_ext(j * tk - i * tq, tk)
    ones = jnp.ones((tk, LANES), BF16)

    def scores(h):
        hs = slice(h * HEAD_DIM, (h + 1) * HEAD_DIM)
        ka = jnp.concatenate([k_ref[:, hs], k_ext], axis=1)
        s = lax.dot_general(qa_ref[h], ka, (((1,), (1,)), ((), ())), preferred_element_type=F32)
        s = s + mb_ref[...]
        s_ref[h % 2] = s
        m_prev = m_ref[h]
        m_cur = jnp.maximum(m_prev, jnp.max(s, axis=1, keepdims=True))
        m_ref[h] = m_cur
        m_safe = jnp.where(m_cur == NEG_INF, 0.0, m_cur)
        return m_safe, jnp.exp2(m_prev - m_safe)

    def accumulate(h, m_safe, a):
        hs = slice(h * HEAD_DIM, (h + 1) * HEAD_DIM)
        p = jnp.exp2(s_ref[h % 2] - jnp.tile(m_safe, (1, reps)))
        va = jnp.concatenate([v_ref[:, hs], ones], axis=1)
        acc_ref[h] = jnp.tile(a, (1, 2)) * acc_ref[h] + jnp.dot(p.astype(BF16), va,
                                                                 preferred_element_type=F32)

    stats = scores(0)
    for h in range(A_HEADS):
        nxt = scores(h + 1) if h + 1 < A_HEADS else None
        accumulate(h, *stats)
        stats = nxt

    @pl.when(j == i)
    def _():
        for h in range(A_HEADS):
            hs = slice(h * HEAD_DIM, (h + 1) * HEAD_DIM)
            o_ref[:, hs] = (acc_ref[h, :, 0:HEAD_DIM] / acc_ref[h, :, HEAD_DIM:2 * HEAD_DIM]).astype(o_ref.dtype)


def _dsa_attention(pb, mask, tq):
    s = pb.shape[0]
    kern = functools.partial(_dsa_attn_kernel, tq=tq, tk=tq)
    qi, kj = _causal_steps(s // tq)
    grid_spec = pltpu.PrefetchScalarGridSpec(
        num_scalar_prefetch=2,
        grid=(int(qi.shape[0]),),
        in_specs=[pl.BlockSpec((tq, A_WIDTH), lambda t, qi, kj: (qi[t], PB_QA // A_WIDTH)),
                  pl.BlockSpec((tq, A_WIDTH), lambda t, qi, kj: (kj[t], PB_KA // A_WIDTH)),
                  pl.BlockSpec((tq, A_WIDTH), lambda t, qi, kj: (kj[t], PB_VA // A_WIDTH)),
                  pl.BlockSpec((tq, tq), lambda t, qi, kj: (qi[t], kj[t]))],
        out_specs=pl.BlockSpec((tq, A_WIDTH), lambda t, qi, kj: (qi[t], 0)),
        scratch_shapes=[pltpu.VMEM((A_HEADS, tq, 2 * HEAD_DIM), BF16),
                        pltpu.VMEM((tq, tq), F32),
                        pltpu.VMEM((2, tq, tq), F32),
                        pltpu.VMEM((A_HEADS, tq, LANES), F32),
                        pltpu.VMEM((A_HEADS, tq, 2 * HEAD_DIM), F32)])
    return pl.pallas_call(
        kern,
        grid_spec=grid_spec,
        out_shape=jax.ShapeDtypeStruct((s, A_WIDTH), BF16),
        compiler_params=_cparams("arbitrary"),
        name="dsa_attention",
    )(qi, kj, pb, pb, pb, mask)


def _diff_attn_kernel(qi_ref, kj_ref, q_ref, k_ref, v_ref, lam_ref, g_ref, o_ref, qa_ref, s_ref, m_ref,
                      acc_ref, *, tq, tk, lam_init):
    t = pl.program_id(0)
    i = qi_ref[t]
    j = kj_ref[t]
    slopes = _alibi_slopes(C_HEADS)
    reps = tk // LANES

    @pl.when(j == 0)
    def _():
        m_ref[...] = jnp.full(m_ref.shape, NEG_INF, F32)
        acc_ref[...] = jnp.zeros(acc_ref.shape, F32)
        lane = lax.broadcasted_iota(I32, (tq, LANES), 1)
        for h in range(C_HEADS):
            qf = q_ref[:, h * LANES:(h + 1) * LANES].astype(F32) * (C_QK ** -0.5 * LOG2E)
            ext = _alibi_q_ext(slopes[h], tq)
            qa_ref[2 * h] = jnp.concatenate([jnp.where(lane < C_QK, qf, 0.0).astype(BF16), ext], axis=1)
            qa_ref[2 * h + 1] = jnp.concatenate([jnp.where(lane < C_QK, 0.0, qf).astype(BF16), ext], axis=1)

    def step(diagonal):
        k_ext = _alibi_k_ext(j * tk - i * tq, tk)
        ones = jnp.ones((tk, LANES), BF16)
        if diagonal:
            causal = lax.broadcasted_iota(I32, (tq, tk), 1) <= lax.broadcasted_iota(I32, (tq, tk), 0)

        def scores(u):
            h = u // 2
            ka = jnp.concatenate([k_ref[:, h * LANES:(h + 1) * LANES], k_ext], axis=1)
            s = lax.dot_general(qa_ref[u], ka, (((1,), (1,)), ((), ())), preferred_element_type=F32)
            if diagonal:
                s = jnp.where(causal, s, NEG_INF)
            s_ref[u % 2] = s
            m_prev = m_ref[u]
            m_cur = jnp.maximum(m_prev, jnp.max(s, axis=1, keepdims=True))
            m_ref[u] = m_cur
            return m_cur, jnp.exp2(m_prev - m_cur)

        def accumulate(u, m_cur, a):
            h = u // 2
            p = jnp.exp2(s_ref[u % 2] - jnp.tile(m_cur, (1, reps)))
            va = jnp.concatenate([v_ref[:, h * LANES:(h + 1) * LANES], ones], axis=1)
            acc_ref[u] = jnp.tile(a, (1, 2)) * acc_ref[u] + jnp.dot(p.astype(BF16), va,
                                                                     preferred_element_type=F32)

        stats = scores(0)
        for u in range(2 * C_HEADS):
            nxt = scores(u + 1) if u + 1 < 2 * C_HEADS else None
            accumulate(u, *stats)
            stats = nxt

    @pl.when(j < i)
    def _():
        step(False)

    @pl.when(j == i)
    def _():
        step(True)
        lp = lam_ref[...]
        lam = (jnp.exp(jnp.sum(lp[0:1] * lp[1:2], axis=1, keepdims=True))
               - jnp.exp(jnp.sum(lp[2:3] * lp[3:4], axis=1, keepdims=True)) + lam_init)
        for h in range(C_HEADS):
            hs = slice(h * LANES, (h + 1) * LANES)
            o = (acc_ref[2 * h, :, 0:LANES] / acc_ref[2 * h, :, LANES:2 * LANES]
                 - lam * (acc_ref[2 * h + 1, :, 0:LANES] / acc_ref[2 * h + 1, :, LANES:2 * LANES]))
            ms = jnp.mean(o * o, axis=1, keepdims=True)
            o_ref[:, hs] = (o * lax.rsqrt(ms + EPS) * g_ref[:, hs] * (1.0 - lam_init)).astype(o_ref.dtype)


def _diff_attention(pb, lam_params, g, lam_init, tq):
    s = pb.shape[0]
    kern = functools.partial(_diff_attn_kernel, tq=tq, tk=tq, lam_init=lam_init)
    qi, kj = _causal_steps(s // tq)
    grid_spec = pltpu.PrefetchScalarGridSpec(
        num_scalar_prefetch=2,
        grid=(int(qi.shape[0]),),
        in_specs=[pl.BlockSpec((tq, C_WIDTH), lambda t, qi, kj: (qi[t], PB_QC // C_WIDTH)),
                  pl.BlockSpec((tq, C_WIDTH), lambda t, qi, kj: (kj[t], PB_KC // C_WIDTH)),
                  pl.BlockSpec((tq, C_WIDTH), lambda t, qi, kj: (kj[t], PB_VC // C_WIDTH)),
                  pl.BlockSpec((4, C_QK), lambda t, qi, kj: (0, 0)),
                  pl.BlockSpec((1, C_WIDTH), lambda t, qi, kj: (0, 0))],
        out_specs=pl.BlockSpec((tq, C_WIDTH), lambda t, qi, kj: (qi[t], 0)),
        scratch_shapes=[pltpu.VMEM((2 * C_HEADS, tq, 2 * LANES), BF16),
                        pltpu.VMEM((2, tq, tq), F32),
                        pltpu.VMEM((2 * C_HEADS, tq, LANES), F32),
                        pltpu.VMEM((2 * C_HEADS, tq, 2 * LANES), F32)])
    return pl.pallas_call(
        kern,
        grid_spec=grid_spec,
        out_shape=jax.ShapeDtypeStruct((s, C_WIDTH), BF16),
        compiler_params=_cparams("arbitrary"),
        name="diff_attention",
    )(qi, kj, pb, pb, pb, lam_params, g)


def _split3(x):
    x1 = x.astype(BF16)
    r1 = x - x1.astype(F32)
    x2 = r1.astype(BF16)
    x3 = (r1 - x2.astype(F32)).astype(BF16)
    return x1, x2, x3


def _mlstm_kernel(q_ref, k_ref, qh_ref, kh_ref, v_ref, o_ref, gi_ref, gf_ref, cw_ref, bi_ref, bf_ref,
                  g_ref, y_ref, xq_ref, xk_ref, c_ref, n_ref, m_ref, *, L):
    c = pl.program_id(0)

    @pl.when(c == 0)
    def _():
        c_ref[...] = jnp.zeros(c_ref.shape, F32)
        n_ref[...] = jnp.zeros(n_ref.shape, F32)
        m_ref[...] = jnp.zeros(m_ref.shape, F32)

    first = (c > 0).astype(F32)
    xq_ref[0:SUBLANES, :] = qh_ref[...] * first
    xq_ref[SUBLANES:SUBLANES + L, :] = q_ref[...]
    xk_ref[0:SUBLANES, :] = kh_ref[...] * first
    xk_ref[SUBLANES:SUBLANES + L, :] = k_ref[...]
    qc = jnp.zeros((L, M_WIDTH), F32)
    kc = jnp.zeros((L, M_WIDTH), F32)
    for t in range(CONV_W):
        off = SUBLANES - (CONV_W - 1) + t
        qc = qc + xq_ref[off:off + L, :] * cw_ref[t:t + 1, 0:M_WIDTH]
        kc = kc + xk_ref[off:off + L, :] * cw_ref[t:t + 1, M_WIDTH:2 * M_WIDTH]
    qc = qc * jax.nn.sigmoid(qc)
    kc = kc * jax.nn.sigmoid(kc) * (HEAD_DIM ** -0.5)

    li = gi_ref[...] + bi_ref[...]
    fp = gf_ref[...] + bf_ref[...]
    lf = jnp.minimum(fp, 0.0) - jnp.log(1.0 + jnp.exp(-jnp.abs(fp)))
    r_i = lax.broadcasted_iota(I32, (L, L), 0)
    c_i = lax.broadcasted_iota(I32, (L, L), 1)
    tril = c_i <= r_i
    tril_b = jnp.where(tril, 1.0, 0.0).astype(BF16)
    a = jnp.zeros((L, LANES), F32)
    for part in _split3(lf):
        a = a + jnp.dot(tril_b, part, preferred_element_type=F32)
    b = li - a
    g_tot = a[L - 1:L, :]
    b_t = b.T

    m_all = m_ref[...]
    m_new_all = m_all
    for h in range(M_HEADS):
        hs = slice(h * HEAD_DIM, (h + 1) * HEAD_DIM)
        qh = qc[:, hs]
        kh = kc[:, hs]
        vh = v_ref[:, hs]
        qb = qh.astype(BF16)
        kb = kh.astype(BF16)
        m_prev = m_all[:, h:h + 1]
        d = jnp.where(tril, b_t[h:h + 1, :], NEG_INF)
        mm = jnp.maximum(m_prev, jnp.max(d, axis=1, keepdims=True))
        w_intra = jnp.exp(d - mm)
        w_inter = jnp.exp(m_prev - mm)
        qk = lax.dot_general(qb, kb, (((1,), (1,)), ((), ())), preferred_element_type=F32) * w_intra
        c_h = c_ref[h]
        num = (jnp.dot(qk.astype(BF16), vh, preferred_element_type=F32)
               + w_inter * jnp.dot(qb, c_h.astype(BF16), preferred_element_type=F32))
        den = (jnp.sum(qk, axis=1, keepdims=True)
               + w_inter * jnp.sum(qh * n_ref[h:h + 1, :], axis=1, keepdims=True))
        m_row = a[:, h:h + 1] + mm
        hh = num / jnp.maximum(jnp.abs(den), jnp.exp(-m_row))
        mm_last = mm[L - 1:L, :]
        ws = jnp.exp(b[:, h:h + 1] - mm_last)
        wc = jnp.exp(m_prev - mm_last)
        c_ref[h] = wc * c_h + lax.dot_general(kb, (ws * vh.astype(F32)).astype(BF16),
                                              (((0,), (0,)), ((), ())), preferred_element_type=F32)
        n_ref[h:h + 1, :] = wc * n_ref[h:h + 1, :] + jnp.sum(ws * kh, axis=0, keepdims=True)
        lane = lax.broadcasted_iota(I32, (1, LANES), 1)
        m_new_all = jnp.where(lane == h, g_tot + mm_last, m_new_all)
        ms = jnp.mean(hh * hh, axis=1, keepdims=True)
        hn = hh * lax.rsqrt(ms + EPS) * g_ref[:, hs]
        y_ref[:, hs] = (jax.nn.sigmoid(o_ref[:, hs]) * hn).astype(y_ref.dtype)
    m_ref[...] = m_new_all


def _mlstm(pb, pf, conv_w, b_i, b_f, g, L):
    s = pb.shape[0]
    kern = functools.partial(_mlstm_kernel, L=L)
    halo = lambda c: jnp.maximum(c * (L // SUBLANES) - 1, 0)
    return pl.pallas_call(
        kern,
        grid=(s // L,),
        in_specs=[pl.BlockSpec((L, M_WIDTH), lambda c: (c, PF_QM // M_WIDTH)),
                  pl.BlockSpec((L, M_WIDTH), lambda c: (c, PF_KM // M_WIDTH)),
                  pl.BlockSpec((SUBLANES, M_WIDTH), lambda c: (halo(c), PF_QM // M_WIDTH)),
                  pl.BlockSpec((SUBLANES, M_WIDTH), lambda c: (halo(c), PF_KM // M_WIDTH)),
                  pl.BlockSpec((L, M_WIDTH), lambda c: (c, PB_VM // M_WIDTH)),
                  pl.BlockSpec((L, M_WIDTH), lambda c: (c, PF_OM // M_WIDTH)),
                  pl.BlockSpec((L, LANES), lambda c: (c, PF_GI // LANES)),
                  pl.BlockSpec((L, LANES), lambda c: (c, PF_GF // LANES)),
                  pl.BlockSpec((CONV_W, 2 * M_WIDTH), lambda c: (0, 0)),
                  pl.BlockSpec((1, LANES), lambda c: (0, 0)),
                  pl.BlockSpec((1, LANES), lambda c: (0, 0)),
                  pl.BlockSpec((1, M_WIDTH), lambda c: (0, 0))],
        out_specs=pl.BlockSpec((L, M_WIDTH), lambda c: (c, 0)),
        out_shape=jax.ShapeDtypeStruct((s, M_WIDTH), BF16),
        scratch_shapes=[pltpu.VMEM((L + SUBLANES, M_WIDTH), F32),
                        pltpu.VMEM((L + SUBLANES, M_WIDTH), F32),
                        pltpu.VMEM((M_HEADS, HEAD_DIM, HEAD_DIM), F32),
                        pltpu.VMEM((SUBLANES, HEAD_DIM), F32),
                        pltpu.VMEM((1, LANES), F32)],
        compiler_params=_cparams("arbitrary"),
        name="mlstm",
    )(pf, pf, pf, pf, pb, pf, pf, pf, conv_w, b_i, b_f, g)


def _layer_norm(z, g, b):
    mu = jnp.mean(z, axis=-1, keepdims=True)
    zc = z - mu
    var = jnp.mean(zc * zc, axis=-1, keepdims=True)
    return zc * lax.rsqrt(var + EPS) * g + b


def _out_kernel(ya_ref, ym_ref, yc_ref, w_ref, x_ref, g_ref, b_ref, o_ref):
    acc = jnp.dot(ya_ref[...], w_ref[0:A_WIDTH, :], preferred_element_type=F32)
    acc = acc + jnp.dot(ym_ref[...], w_ref[A_WIDTH:A_WIDTH + M_WIDTH, :], preferred_element_type=F32)
    acc = acc + jnp.dot(yc_ref[...], w_ref[A_WIDTH + M_WIDTH:D_MODEL, :], preferred_element_type=F32)
    o_ref[...] = _layer_norm(ALPHA * x_ref[...] + acc, g_ref[...], b_ref[...])


def _out_proj(ya, ym, yc, w, layer, x, g, b, tm):
    s = x.shape[0]
    return pl.pallas_call(
        _out_kernel,
        grid=(s // tm,),
        in_specs=[pl.BlockSpec((tm, A_WIDTH), lambda i: (i, 0)),
                  pl.BlockSpec((tm, M_WIDTH), lambda i: (i, 0)),
                  pl.BlockSpec((tm, C_WIDTH), lambda i: (i, 0)),
                  pl.BlockSpec((None, D_MODEL, D_MODEL), lambda i: (layer, 0, 0)),
                  pl.BlockSpec((tm, D_MODEL), lambda i: (i, 0)),
                  pl.BlockSpec((1, D_MODEL), lambda i: (0, 0)),
                  pl.BlockSpec((1, D_MODEL), lambda i: (0, 0))],
        out_specs=pl.BlockSpec((tm, D_MODEL), lambda i: (i, 0)),
        out_shape=jax.ShapeDtypeStruct((s, D_MODEL), F32),
        compiler_params=_cparams("arbitrary"),
        name="out_proj_ln",
    )(ya, ym, yc, w, x, g, b)


def _ffn_kernel(x_ref, wu_ref, wd_ref, g_ref, b_ref, o_ref, xb_ref, acc_ref):
    f = pl.program_id(1)

    @pl.when(f == 0)
    def _():
        xb_ref[...] = x_ref[...].astype(BF16)
        acc_ref[...] = jnp.zeros(acc_ref.shape, F32)

    hdn = jnp.maximum(jnp.dot(xb_ref[...], wu_ref[...], preferred_element_type=F32), 0.0)
    acc_ref[...] += jnp.dot((hdn * hdn).astype(BF16), wd_ref[...], preferred_element_type=F32)

    @pl.when(f == pl.num_programs(1) - 1)
    def _():
        o_ref[...] = _layer_norm(ALPHA * x_ref[...] + acc_ref[...], g_ref[...], b_ref[...])


def _ffn(x, wu, wd, layer, g, b, tm, tf):
    s = x.shape[0]
    return pl.pallas_call(
        _ffn_kernel,
        grid=(s // tm, D_FF // tf),
        in_specs=[pl.BlockSpec((tm, D_MODEL), lambda i, f: (i, 0)),
                  pl.BlockSpec((None, D_MODEL, tf), lambda i, f: (layer, 0, f)),
                  pl.BlockSpec((None, tf, D_MODEL), lambda i, f: (layer, f, 0)),
                  pl.BlockSpec((1, D_MODEL), lambda i, f: (0, 0)),
                  pl.BlockSpec((1, D_MODEL), lambda i, f: (0, 0))],
        out_specs=pl.BlockSpec((tm, D_MODEL), lambda i, f: (i, 0)),
        out_shape=jax.ShapeDtypeStruct((s, D_MODEL), F32),
        scratch_shapes=[pltpu.VMEM((tm, D_MODEL), BF16),
                        pltpu.VMEM((tm, D_MODEL), F32)],
        compiler_params=_cparams("arbitrary", "arbitrary"),
        name="ffn_ln",
    )(x, wu, wd, g, b)


def _seg(w, i):
    return w[..., _OFFS[i]:_OFFS[i] + SIZES[i]]


def _pad_cols(w, width):
    return jnp.pad(w, [(0, 0)] * (w.ndim - 1) + [(0, width - w.shape[-1])])


def _layout_w_in(w_in):
    w_in = w_in.astype(BF16)
    (q_a, k_a, v_a, q_i, k_i, w_i, q_m, k_m, v_m, o_m, i_m, f_m, q_c, k_c, v_c) = [
        _seg(w_in, i) for i in range(len(SIZES))]
    wb = jnp.concatenate([q_a, k_a, v_a, v_m, q_c, k_c, v_c, q_i, k_i, k_i], axis=-1)
    wf = jnp.concatenate([q_m, k_m, o_m, _pad_cols(w_i, LANES), _pad_cols(i_m, LANES),
                          _pad_cols(f_m, LANES)], axis=-1)
    return _pad_cols(wb, PB_WIDTH), _pad_cols(wf, PF_WIDTH)


def _tile(n, pref):
    t = min(n, pref)
    assert n % t == 0, (n, t)
    return t


def kernel(x, w_in, conv_m, b_i, b_f, m_norm_g, lam_q1, lam_k1, lam_q2, lam_k2, c_norm_g, w_out,
           ln1_g, ln1_b, w_up, w_down, ln2_g, ln2_b):
    batch, s, d = x.shape
    assert batch == 1 and d == D_MODEL
    ksel = min(TOPK_MAX, s // 4)
    wb_all, wf_all = _layout_w_in(w_in)
    w_out_b = w_out.astype(BF16)
    w_up_b = w_up.astype(BF16)
    w_down_b = w_down.astype(BF16)
    conv_w = conv_m.reshape(DEPTH, CONV_W, 2 * M_WIDTH)
    b_i_p = _pad_cols(b_i, LANES).reshape(DEPTH, 1, LANES)
    b_f_p = _pad_cols(b_f, LANES).reshape(DEPTH, 1, LANES)
    lam_p = jnp.stack([lam_q1, lam_k1, lam_q2, lam_k2], axis=1)

    tm_proj = _tile(s, 1024)
    t_attn = _tile(s, 512)
    tq_idx = _tile(s, 512)
    tk_idx = _tile(s, 512)
    l_chunk = _tile(s, 256)
    tm_out = _tile(s, 512)
    tm_ffn = _tile(s, 512)

    h = x.reshape(s, d)
    for l in range(DEPTH):
        pb = _matmul(h, wb_all, l, BF16, tm_proj, 768, "proj_bf16")
        pf = _matmul(h, wf_all, l, F32, tm_proj, 768, "proj_f32")
        mask = _dsa_mask(pb, pf, ksel, tq_idx, tk_idx)
        y_a = _dsa_attention(pb, mask, t_attn)
        y_m = _mlstm(pb, pf, conv_w[l], b_i_p[l], b_f_p[l], m_norm_g[l].reshape(1, M_WIDTH), l_chunk)
        lam_init = 0.8 - 0.6 * math.exp(-0.3 * l)
        y_c = _diff_attention(pb, lam_p[l], c_norm_g[l].reshape(1, C_WIDTH), lam_init, t_attn)
        h = _out_proj(y_a, y_m, y_c, w_out_b, l, h, ln1_g[l].reshape(1, d), ln1_b[l].reshape(1, d), tm_out)
        h = _ffn(h, w_up_b, w_down_b, l, ln2_g[l].reshape(1, d), ln2_b[l].reshape(1, d), tm_ffn, 1024)
    return h.reshape(batch, s, d)
```

```python
import functools
import math

import jax
import jax.numpy as jnp
import numpy as np
from jax import lax
from jax.experimental import pallas as pl
from jax.experimental.pallas import tpu as pltpu

F32 = jnp.float32
BF16 = jnp.bfloat16
I32 = jnp.int32

D_MODEL = 2048
DEPTH = 4
HEAD_DIM = 128
A_HEADS = 6
A_WIDTH = A_HEADS * HEAD_DIM
IDX_HEADS = 8
IDX_DIM = 64
TOPK_MAX = 256
M_HEADS = 6
M_WIDTH = M_HEADS * HEAD_DIM
CONV_W = 4
C_HEADS = 4
C_QK = 64
C_WIDTH = C_HEADS * 2 * C_QK
D_FF = 4 * D_MODEL
ALPHA = (2.0 * DEPTH) ** 0.25
EPS = 1e-5

SIZES = (A_WIDTH, A_WIDTH, A_WIDTH, IDX_HEADS * IDX_DIM, IDX_DIM, IDX_HEADS,
         M_WIDTH, M_WIDTH, M_WIDTH, M_WIDTH, M_HEADS, M_HEADS,
         C_WIDTH, C_WIDTH, C_WIDTH)
_OFFS = tuple(int(sum(SIZES[:i])) for i in range(len(SIZES)))

LANES = 128
SUBLANES = 8
VMEM_LIMIT_BYTES = 56 * 1024 * 1024
INT_MIN = -2 ** 31
NEG_INF = float("-inf")
LOG2E = math.log2(math.e)

PB_QA, PB_KA, PB_VA, PB_VM = 0, 768, 1536, 2304
PB_QC, PB_KC, PB_VC, PB_QI = 3072, 3584, 4096, 4608
PB_KK = 5120
PB_WIDTH = 5376
PF_QM, PF_KM, PF_OM = 0, 768, 1536
PF_WI, PF_GI, PF_GF = 2304, 2432, 2560
PF_WIDTH = 3072


def _cparams(*sem):
    return pltpu.CompilerParams(dimension_semantics=sem, vmem_limit_bytes=VMEM_LIMIT_BYTES)


def _alibi_slopes(n):
    return [2.0 ** (-8.0 * (h + 1) / n) for h in range(n)]


def _mm_kernel(x_ref, w_ref, o_ref, xb_ref):
    @pl.when(pl.program_id(1) == 0)
    def _():
        xb_ref[...] = x_ref[...].astype(BF16)

    o_ref[...] = jnp.dot(xb_ref[...], w_ref[...], preferred_element_type=F32).astype(o_ref.dtype)


def _matmul(x, w, layer, out_dtype, tm, tn, name):
    m, k = x.shape
    n = w.shape[2]
    return pl.pallas_call(
        _mm_kernel,
        grid=(m // tm, n // tn),
        in_specs=[pl.BlockSpec((tm, k), lambda i, j: (i, 0)),
                  pl.BlockSpec((None, k, tn), lambda i, j: (layer, 0, j))],
        out_specs=pl.BlockSpec((tm, tn), lambda i, j: (i, j)),
        out_shape=jax.ShapeDtypeStruct((m, n), out_dtype),
        scratch_shapes=[pltpu.VMEM((tm, k), BF16)],
        compiler_params=_cparams("arbitrary", "arbitrary"),
        name=name,
    )(x, w)


def _order_key(x):
    bits = lax.bitcast_convert_type(x, I32)
    key = jnp.where(bits < 0, bits ^ jnp.int32(0x7FFFFFFF), bits)
    return jnp.where(bits == jnp.int32(INT_MIN), 0, key)


def _idx_kernel(qi_ref, kk_ref, wi_ref, mask_ref, q8_ref, w8_ref, keys_ref, gm_ref,
                *, tq, tk, nk, ksel, jbits):
    i = pl.program_id(0)
    nck = ((i + 1) * tq + tk - 1) // tk
    classes = gm_ref.shape[0]
    acc_rows = 32

    lane = lax.broadcasted_iota(I32, (tq, LANES), 1)
    lo_half = jnp.where(lane < IDX_DIM, 1.0, 0.0).astype(F32)
    hi_half = 1.0 - lo_half
    for h in range(IDX_HEADS):
        qp = qi_ref[:, (h // 2) * LANES:(h // 2 + 1) * LANES].astype(F32)
        q8_ref[h * tq:(h + 1) * tq, :] = (qp * (lo_half if h % 2 == 0 else hi_half)).astype(BF16)
    w8_ref[...] = wi_ref[...].T[0:SUBLANES, :]
    gm_ref[...] = jnp.full(gm_ref.shape, NEG_INF, F32)

    q_idx = i * tq + lax.broadcasted_iota(I32, (tk, tq), 1)
    k_off = lax.broadcasted_iota(I32, (tk, tq), 0)

    def score_chunk(c, carry):
        kc = kk_ref[pl.ds(pl.multiple_of(c * tk, tk), tk), :]
        y = lax.dot_general(kc, q8_ref[...], (((1,), (1,)), ((), ())), preferred_element_type=F32)
        sc = jnp.maximum(y[:, 0:tq], 0.0) * w8_ref[0:1, :]
        for h in range(1, IDX_HEADS):
            sc = sc + jnp.maximum(y[:, h * tq:(h + 1) * tq], 0.0) * w8_ref[h:h + 1, :]
        causal = c * tk + k_off <= q_idx
        keys_ref[c] = jnp.where(causal, _order_key(sc), jnp.int32(INT_MIN))
        scm = jnp.where(causal, sc, NEG_INF)
        for cls in range(classes):
            g = gm_ref[cls]
            for r in range(cls * SUBLANES, tk, classes * SUBLANES):
                g = jnp.maximum(g, scm[r:r + SUBLANES, :])
            gm_ref[cls] = g
        return carry

    lax.fori_loop(0, nck, score_chunk, 0)

    def count(pred):
        def body(c, acc):
            for r in range(0, tk, acc_rows):
                acc = jnp.where(pred(keys_ref[c, r:r + acc_rows, :], c * tk + r), acc + 1.0, acc)
            return acc
        acc = lax.fori_loop(0, nck, body, jnp.zeros((acc_rows, tq), F32))
        return jnp.sum(acc, axis=0, keepdims=True)

    g_min = gm_ref[0]
    g_max = gm_ref[0]
    for cls in range(1, classes):
        g_min = jnp.minimum(g_min, gm_ref[cls])
        g_max = jnp.maximum(g_max, gm_ref[cls])
    lo_f = jnp.min(g_min, axis=0, keepdims=True)
    hi_f = jnp.max(g_max, axis=0, keepdims=True)
    n_valid = i * tq + lax.broadcasted_iota(I32, (1, tq), 1) + 1
    few = n_valid <= ksel
    lo0 = jnp.where(jnp.logical_or(few, lo_f == NEG_INF), jnp.int32(INT_MIN), _order_key(lo_f))
    hi0 = jnp.where(few, jnp.int32(INT_MIN), _order_key(hi_f))

    def unresolved(st):
        lo, hi = st
        return jnp.max(jnp.where(lo < hi, 1.0, 0.0)) > 0.0

    def bisect(st):
        lo, hi = st
        mid = (lo >> 1) + (hi >> 1) + ((lo | hi) & 1)
        cnt = count(lambda k, r0: k >= mid)
        ge = cnt >= ksel
        exact = cnt == ksel
        return (jnp.where(ge, mid, lo), jnp.where(exact, mid, jnp.where(ge, hi, mid - 1)))

    thr, _ = lax.while_loop(unresolved, bisect, (lo0, hi0))
    n_gt = count(lambda k, r0: k > thr)
    n_ge = count(lambda k, r0: k >= thr)
    need = ksel - n_gt
    excess = jnp.logical_and(thr > jnp.int32(INT_MIN), n_ge > ksel)
    any_excess = jnp.max(jnp.where(excess, 1.0, 0.0)) > 0.0
    sub = lax.broadcasted_iota(I32, (acc_rows, tq), 0)

    def tie_search():
        def jbody(b, jc):
            cand = jc + lax.shift_left(jnp.int32(1), jbits - 1 - b)
            cnt = count(lambda k, r0: jnp.logical_and(k == thr, r0 + sub < cand))
            return jnp.where(cnt <= need, cand, jc)
        return lax.fori_loop(0, jbits, jbody, jnp.zeros((1, tq), I32))

    def write_mask(select):
        for c in range(nk):
            @pl.when(c < nck)
            def _(c=c):
                sel = jnp.where(select(keys_ref[c], c), 1.0, 0.0).astype(F32)
                mask_ref[:, c * tk:(c + 1) * tk] = sel.T.astype(BF16)

            @pl.when(c >= nck)
            def _(c=c):
                mask_ref[:, c * tk:(c + 1) * tk] = jnp.zeros((tq, tk), BF16)

    @pl.when(any_excess)
    def _():
        jcut = tie_search()

        def select(key, c):
            tie = jnp.logical_and(key == thr, c * tk + k_off < jcut)
            return jnp.logical_and(key > jnp.int32(INT_MIN), jnp.logical_or(key > thr, tie))
        write_mask(select)

    @pl.when(jnp.logical_not(any_excess))
    def _():
        floor = jnp.maximum(thr, jnp.int32(INT_MIN + 1))
        write_mask(lambda key, c: key >= floor)


def _dsa_mask(pb, pf, ksel, tq, tk):
    s = pb.shape[0]
    nk = s // tk
    jbits = int(s).bit_length()
    classes = min(32, tk // SUBLANES)
    assert ksel <= SUBLANES * classes, "the threshold bracket relies on that many disjoint key groups"
    kern = functools.partial(_idx_kernel, tq=tq, tk=tk, nk=nk, ksel=ksel, jbits=jbits)
    return pl.pallas_call(
        kern,
        grid=(s // tq,),
        in_specs=[pl.BlockSpec((tq, IDX_HEADS * IDX_DIM), lambda i: (i, PB_QI // (IDX_HEADS * IDX_DIM))),
                  pl.BlockSpec((s, LANES), lambda i: (0, PB_KK // LANES)),
                  pl.BlockSpec((tq, LANES), lambda i: (i, PF_WI // LANES))],
        out_specs=pl.BlockSpec((tq, s), lambda i: (i, 0)),
        out_shape=jax.ShapeDtypeStruct((s, s), BF16),
        scratch_shapes=[pltpu.VMEM((IDX_HEADS * tq, LANES), BF16),
                        pltpu.VMEM((SUBLANES, tq), F32),
                        pltpu.VMEM((nk, tk, tq), I32),
                        pltpu.VMEM((classes, SUBLANES, tq), F32)],
        compiler_params=_cparams("arbitrary"),
        name="dsa_index_mask",
    )(pb, pb, pf)


def _causal_steps(n):
    qi = [i for i in range(n) for _ in range(i + 1)]
    kj = [j for i in range(n) for j in range(i + 1)]
    return jnp.asarray(qi, I32), jnp.asarray(kj, I32)


def _bf16_terms(x):
    out = []
    r = np.float32(x)
    for _ in range(3):
        t = np.float32(np.asarray(r, dtype=jnp.bfloat16))
        out.append(float(t))
        r = np.float32(r - t)
    return out


def _alibi_q_ext(slope, rows):
    s1, s2, s3 = _bf16_terms(slope * LOG2E)
    lane = lax.broadcasted_iota(I32, (rows, LANES), 1)
    ext = jnp.where(lane < 3, s1, jnp.where(lane < 6, s2, jnp.where(lane < 9, s3, 0.0)))
    return ext.astype(BF16)


def _alibi_k_ext(first_pos, rows):
    pos = (first_pos + lax.broadcasted_iota(I32, (rows, LANES), 0)).astype(F32)
    p1, p2, p3 = _split3(pos)
    lane = lax.broadcasted_iota(I32, (rows, LANES), 1)
    sel = lane % 3
    ext = jnp.where(sel == 0, p1.astype(F32), jnp.where(sel == 1, p2.astype(F32), p3.astype(F32)))
    return jnp.where(lane < 9, ext, 0.0).astype(BF16)


def _dsa_attn_kernel(qi_ref, kj_ref, q_ref, k_ref, v_ref, mask_ref, o_ref, qa_ref, mb_ref, s_ref, m_ref,
                     acc_ref, *, tq, tk):
    t = pl.program_id(0)
    i = qi_ref[t]
    j = kj_ref[t]
    slopes = _alibi_slopes(A_HEADS)
    reps = tk // LANES

    @pl.when(j == 0)
    def _():
        m_ref[...] = jnp.full(m_ref.shape, NEG_INF, F32)
        acc_ref[...] = jnp.zeros(acc_ref.shape, F32)
        for h in range(A_HEADS):
            hs = slice(h * HEAD_DIM, (h + 1) * HEAD_DIM)
            qh = (q_ref[:, hs].astype(F32) * (HEAD_DIM ** -0.5 * LOG2E)).astype(BF16)
            qa_ref[h] = jnp.concatenate([qh, _alibi_q_ext(slopes[h], tq)], axis=1)

    mf = mask_ref[...].astype(F32)
    mb_ref[...] = jnp.where(mf > 0.0, mf - 1.0, NEG_INF)
    k_ext = _alibi_k_ext(j * tk - i * tq, tk)
    ones = jnp.ones((tk, LANES), BF16)

    def scores(h):
        hs = slice(h * HEAD_DIM, (h + 1) * HEAD_DIM)
        ka = jnp.concatenate([k_ref[:, hs], k_ext], axis=1)
        s = lax.dot_general(qa_ref[h], ka, (((1,), (1,)), ((), ())), preferred_element_type=F32)
        s = s + mb_ref[...]
        s_ref[h % 2] = s
        m_prev = m_ref[h]
        m_cur = jnp.maximum(m_prev, jnp.max(s, axis=1, keepdims=True))
        m_ref[h] = m_cur
        m_safe = jnp.where(m_cur == NEG_INF, 0.0, m_cur)
        return m_safe, jnp.exp2(m_prev - m_safe)

    def accumulate(h, m_safe, a):
        hs = slice(h * HEAD_DIM, (h + 1) * HEAD_DIM)
        p = jnp.exp2(s_ref[h % 2] - jnp.tile(m_safe, (1, reps)))
        va = jnp.concatenate([v_ref[:, hs], ones], axis=1)
        acc_ref[h] = jnp.tile(a, (1, 2)) * acc_ref[h] + jnp.dot(p.astype(BF16), va,
                                                                 preferred_element_type=F32)

    stats = scores(0)
    for h in range(A_HEADS):
        nxt = scores(h + 1) if h + 1 < A_HEADS else None
        accumulate(h, *stats)
        stats = nxt

    @pl.when(j == i)
    def _():
        for h in range(A_HEADS):
            hs = slice(h * HEAD_DIM, (h + 1) * HEAD_DIM)
            o_ref[:, hs] = (acc_ref[h, :, 0:HEAD_DIM] / acc_ref[h, :, HEAD_DIM:2 * HEAD_DIM]).astype(o_ref.dtype)


def _diff_attn_kernel(qi_ref, kj_ref, q_ref, k_ref, v_ref, lam_ref, g_ref, o_ref, qa_ref, s_ref, m_ref,
                      acc_ref, *, tq, tk, lam_init):
    t = pl.program_id(0)
    i = qi_ref[t]
    j = kj_ref[t]
    slopes = _alibi_slopes(C_HEADS)
    reps = tk // LANES

    @pl.when(j == 0)
    def _():
        m_ref[...] = jnp.full(m_ref.shape, NEG_INF, F32)
        acc_ref[...] = jnp.zeros(acc_ref.shape, F32)
        lane = lax.broadcasted_iota(I32, (tq, LANES), 1)
        for h in range(C_HEADS):
            qf = q_ref[:, h * LANES:(h + 1) * LANES].astype(F32) * (C_QK ** -0.5 * LOG2E)
            ext = _alibi_q_ext(slopes[h], tq)
            qa_ref[2 * h] = jnp.concatenate([jnp.where(lane < C_QK, qf, 0.0).astype(BF16), ext], axis=1)
            qa_ref[2 * h + 1] = jnp.concatenate([jnp.where(lane < C_QK, 0.0, qf).astype(BF16), ext], axis=1)

    def step(diagonal):
        k_ext = _alibi_k_ext(j * tk - i * tq, tk)
        ones = jnp.ones((tk, LANES), BF16)
        if diagonal:
            causal = lax.broadcasted_iota(I32, (tq, tk), 1) <= lax.broadcasted_iota(I32, (tq, tk), 0)

        def scores(u):
            h = u // 2
            ka = jnp.concatenate([k_ref[:, h * LANES:(h + 1) * LANES], k_ext], axis=1)
            s = lax.dot_general(qa_ref[u], ka, (((1,), (1,)), ((), ())), preferred_element_type=F32)
            if diagonal:
                s = jnp.where(causal, s, NEG_INF)
            s_ref[u % 2] = s
            m_prev = m_ref[u]
            m_cur = jnp.maximum(m_prev, jnp.max(s, axis=1, keepdims=True))
            m_ref[u] = m_cur
            return m_cur, jnp.exp2(m_prev - m_cur)

        def accumulate(u, m_cur, a):
            h = u // 2
            p = jnp.exp2(s_ref[u % 2] - jnp.tile(m_cur, (1, reps)))
            va = jnp.concatenate([v_ref[:, h * LANES:(h + 1) * LANES], ones], axis=1)
            acc_ref[u] = jnp.tile(a, (1, 2)) * acc_ref[u] + jnp.dot(p.astype(BF16), va,
                                                                     preferred_element_type=F32)

        stats = scores(0)
        for u in range(2 * C_HEADS):
            nxt = scores(u + 1) if u + 1 < 2 * C_HEADS else None
            accumulate(u, *stats)
            stats = nxt

    @pl.when(j < i)
    def _():
        step(False)

    @pl.when(j == i)
    def _():
        step(True)
        lp = lam_ref[...]
        lam = (jnp.exp(jnp.sum(lp[0:1] * lp[1:2], axis=1, keepdims=True))
               - jnp.exp(jnp.sum(lp[2:3] * lp[3:4], axis=1, keepdims=True)) + lam_init)
        for h in range(C_HEADS):
            hs = slice(h * LANES, (h + 1) * LANES)
            o = (acc_ref[2 * h, :, 0:LANES] / acc_ref[2 * h, :, LANES:2 * LANES]
                 - lam * (acc_ref[2 * h + 1, :, 0:LANES] / acc_ref[2 * h + 1, :, LANES:2 * LANES]))
            ms = jnp.mean(o * o, axis=1, keepdims=True)
            o_ref[:, hs] = (o * lax.rsqrt(ms + EPS) * g_ref[:, hs] * (1.0 - lam_init)).astype(o_ref.dtype)


_N_DSA_SCRATCH = 5


def _attn_pair_kernel(qi_ref, kj_ref, qa_ref, ka_ref, va_ref, mask_ref, qc_ref, kc_ref, vc_ref, lam_ref, g_ref,
                      oa_ref, oc_ref, *scratch, tq, tk, lam_init):
    _dsa_attn_kernel(qi_ref, kj_ref, qa_ref, ka_ref, va_ref, mask_ref, oa_ref, *scratch[:_N_DSA_SCRATCH],
                     tq=tq, tk=tk)
    _diff_attn_kernel(qi_ref, kj_ref, qc_ref, kc_ref, vc_ref, lam_ref, g_ref, oc_ref,
                      *scratch[_N_DSA_SCRATCH:], tq=tq, tk=tk, lam_init=lam_init)


def _attention_pair(pb, mask, lam_params, g, lam_init, tq):
    s = pb.shape[0]
    kern = functools.partial(_attn_pair_kernel, tq=tq, tk=tq, lam_init=lam_init)
    qi, kj = _causal_steps(s // tq)
    q_blk = lambda width, col: pl.BlockSpec((tq, width), lambda t, qi, kj: (qi[t], col // width))
    kv_blk = lambda width, col: pl.BlockSpec((tq, width), lambda t, qi, kj: (kj[t], col // width))
    grid_spec = pltpu.PrefetchScalarGridSpec(
        num_scalar_prefetch=2,
        grid=(int(qi.shape[0]),),
        in_specs=[q_blk(A_WIDTH, PB_QA), kv_blk(A_WIDTH, PB_KA), kv_blk(A_WIDTH, PB_VA),
                  pl.BlockSpec((tq, tq), lambda t, qi, kj: (qi[t], kj[t])),
                  q_blk(C_WIDTH, PB_QC), kv_blk(C_WIDTH, PB_KC), kv_blk(C_WIDTH, PB_VC),
                  pl.BlockSpec((4, C_QK), lambda t, qi, kj: (0, 0)),
                  pl.BlockSpec((1, C_WIDTH), lambda t, qi, kj: (0, 0))],
        out_specs=[pl.BlockSpec((tq, A_WIDTH), lambda t, qi, kj: (qi[t], 0)),
                   pl.BlockSpec((tq, C_WIDTH), lambda t, qi, kj: (qi[t], 0))],
        scratch_shapes=[pltpu.VMEM((A_HEADS, tq, 2 * HEAD_DIM), BF16),
                        pltpu.VMEM((tq, tq), F32),
                        pltpu.VMEM((2, tq, tq), F32),
                        pltpu.VMEM((A_HEADS, tq, LANES), F32),
                        pltpu.VMEM((A_HEADS, tq, 2 * HEAD_DIM), F32),
                        pltpu.VMEM((2 * C_HEADS, tq, 2 * LANES), BF16),
                        pltpu.VMEM((2, tq, tq), F32),
                        pltpu.VMEM((2 * C_HEADS, tq, LANES), F32),
                        pltpu.VMEM((2 * C_HEADS, tq, 2 * LANES), F32)])
    return pl.pallas_call(
        kern,
        grid_spec=grid_spec,
        out_shape=[jax.ShapeDtypeStruct((s, A_WIDTH), BF16), jax.ShapeDtypeStruct((s, C_WIDTH), BF16)],
        compiler_params=_cparams("arbitrary"),
        name="attention_pair",
    )(qi, kj, pb, pb, pb, mask, pb, pb, pb, lam_params, g)


def _split3(x):
    x1 = x.astype(BF16)
    r1 = x - x1.astype(F32)
    x2 = r1.astype(BF16)
    x3 = (r1 - x2.astype(F32)).astype(BF16)
    return x1, x2, x3


def _mlstm_kernel(q_ref, k_ref, qh_ref, kh_ref, v_ref, o_ref, gi_ref, gf_ref, cw_ref, bi_ref, bf_ref,
                  g_ref, y_ref, xq_ref, xk_ref, c_ref, n_ref, m_ref, *, L):
    c = pl.program_id(0)

    @pl.when(c == 0)
    def _():
        c_ref[...] = jnp.zeros(c_ref.shape, F32)
        n_ref[...] = jnp.zeros(n_ref.shape, F32)
        m_ref[...] = jnp.zeros(m_ref.shape, F32)

    first = (c > 0).astype(F32)
    xq_ref[0:SUBLANES, :] = qh_ref[...] * first
    xq_ref[SUBLANES:SUBLANES + L, :] = q_ref[...]
    xk_ref[0:SUBLANES, :] = kh_ref[...] * first
    xk_ref[SUBLANES:SUBLANES + L, :] = k_ref[...]
    qc = jnp.zeros((L, M_WIDTH), F32)
    kc = jnp.zeros((L, M_WIDTH), F32)
    for t in range(CONV_W):
        off = SUBLANES - (CONV_W - 1) + t
        qc = qc + xq_ref[off:off + L, :] * cw_ref[t:t + 1, 0:M_WIDTH]
        kc = kc + xk_ref[off:off + L, :] * cw_ref[t:t + 1, M_WIDTH:2 * M_WIDTH]
    qc = qc * jax.nn.sigmoid(qc)
    kc = kc * jax.nn.sigmoid(kc) * (HEAD_DIM ** -0.5)

    li = gi_ref[...] + bi_ref[...]
    fp = gf_ref[...] + bf_ref[...]
    lf = jnp.minimum(fp, 0.0) - jnp.log(1.0 + jnp.exp(-jnp.abs(fp)))
    r_i = lax.broadcasted_iota(I32, (L, L), 0)
    c_i = lax.broadcasted_iota(I32, (L, L), 1)
    tril = c_i <= r_i
    tril_b = jnp.where(tril, 1.0, 0.0).astype(BF16)
    a = jnp.zeros((L, LANES), F32)
    for part in _split3(lf):
        a = a + jnp.dot(tril_b, part, preferred_element_type=F32)
    b = li - a
    g_tot = a[L - 1:L, :]
    b_t = b.T

    m_all = m_ref[...]
    m_new_all = m_all
    for h in range(M_HEADS):
        hs = slice(h * HEAD_DIM, (h + 1) * HEAD_DIM)
        qh = qc[:, hs]
        kh = kc[:, hs]
        vh = v_ref[:, hs]
        qb = qh.astype(BF16)
        kb = kh.astype(BF16)
        m_prev = m_all[:, h:h + 1]
        d = jnp.where(tril, b_t[h:h + 1, :], NEG_INF)
        mm = jnp.maximum(m_prev, jnp.max(d, axis=1, keepdims=True))
        w_intra = jnp.exp(d - mm)
        w_inter = jnp.exp(m_prev - mm)
        qk = lax.dot_general(qb, kb, (((1,), (1,)), ((), ())), preferred_element_type=F32) * w_intra
        c_h = c_ref[h]
        num = (jnp.dot(qk.astype(BF16), vh, preferred_element_type=F32)
               + w_inter * jnp.dot(qb, c_h.astype(BF16), preferred_element_type=F32))
        den = (jnp.sum(qk, axis=1, keepdims=True)
               + w_inter * jnp.sum(qh * n_ref[h:h + 1, :], axis=1, keepdims=True))
        m_row = a[:, h:h + 1] + mm
        hh = num / jnp.maximum(jnp.abs(den), jnp.exp(-m_row))
        mm_last = mm[L - 1:L, :]
        ws = jnp.exp(b[:, h:h + 1] - mm_last)
        wc = jnp.exp(m_prev - mm_last)
        c_ref[h] = wc * c_h + lax.dot_general(kb, (ws * vh.astype(F32)).astype(BF16),
                                              (((0,), (0,)), ((), ())), preferred_element_type=F32)
        n_ref[h:h + 1, :] = wc * n_ref[h:h + 1, :] + jnp.sum(ws * kh, axis=0, keepdims=True)
        lane = lax.broadcasted_iota(I32, (1, LANES), 1)
        m_new_all = jnp.where(lane == h, g_tot + mm_last, m_new_all)
        ms = jnp.mean(hh * hh, axis=1, keepdims=True)
        hn = hh * lax.rsqrt(ms + EPS) * g_ref[:, hs]
        y_ref[:, hs] = (jax.nn.sigmoid(o_ref[:, hs]) * hn).astype(y_ref.dtype)
    m_ref[...] = m_new_all


def _mlstm(pb, pf, conv_w, b_i, b_f, g, L):
    s = pb.shape[0]
    kern = functools.partial(_mlstm_kernel, L=L)
    halo = lambda c: jnp.maximum(c * (L // SUBLANES) - 1, 0)
    return pl.pallas_call(
        kern,
        grid=(s // L,),
        in_specs=[pl.BlockSpec((L, M_WIDTH), lambda c: (c, PF_QM // M_WIDTH)),
                  pl.BlockSpec((L, M_WIDTH), lambda c: (c, PF_KM // M_WIDTH)),
                  pl.BlockSpec((SUBLANES, M_WIDTH), lambda c: (halo(c), PF_QM // M_WIDTH)),
                  pl.BlockSpec((SUBLANES, M_WIDTH), lambda c: (halo(c), PF_KM // M_WIDTH)),
                  pl.BlockSpec((L, M_WIDTH), lambda c: (c, PB_VM // M_WIDTH)),
                  pl.BlockSpec((L, M_WIDTH), lambda c: (c, PF_OM // M_WIDTH)),
                  pl.BlockSpec((L, LANES), lambda c: (c, PF_GI // LANES)),
                  pl.BlockSpec((L, LANES), lambda c: (c, PF_GF // LANES)),
                  pl.BlockSpec((CONV_W, 2 * M_WIDTH), lambda c: (0, 0)),
                  pl.BlockSpec((1, LANES), lambda c: (0, 0)),
                  pl.BlockSpec((1, LANES), lambda c: (0, 0)),
                  pl.BlockSpec((1, M_WIDTH), lambda c: (0, 0))],
        out_specs=pl.BlockSpec((L, M_WIDTH), lambda c: (c, 0)),
        out_shape=jax.ShapeDtypeStruct((s, M_WIDTH), BF16),
        scratch_shapes=[pltpu.VMEM((L + SUBLANES, M_WIDTH), F32),
                        pltpu.VMEM((L + SUBLANES, M_WIDTH), F32),
                        pltpu.VMEM((M_HEADS, HEAD_DIM, HEAD_DIM), F32),
                        pltpu.VMEM((SUBLANES, HEAD_DIM), F32),
                        pltpu.VMEM((1, LANES), F32)],
        compiler_params=_cparams("arbitrary"),
        name="mlstm",
    )(pf, pf, pf, pf, pb, pf, pf, pf, conv_w, b_i, b_f, g)


def _layer_norm(z, g, b):
    mu = jnp.mean(z, axis=-1, keepdims=True)
    zc = z - mu
    var = jnp.mean(zc * zc, axis=-1, keepdims=True)
    return zc * lax.rsqrt(var + EPS) * g + b


def _out_kernel(ya_ref, ym_ref, yc_ref, w_ref, x_ref, g_ref, b_ref, o_ref):
    acc = jnp.dot(ya_ref[...], w_ref[0:A_WIDTH, :], preferred_element_type=F32)
    acc = acc + jnp.dot(ym_ref[...], w_ref[A_WIDTH:A_WIDTH + M_WIDTH, :], preferred_element_type=F32)
    acc = acc + jnp.dot(yc_ref[...], w_ref[A_WIDTH + M_WIDTH:D_MODEL, :], preferred_element_type=F32)
    o_ref[...] = _layer_norm(ALPHA * x_ref[...] + acc, g_ref[...], b_ref[...])


def _out_proj(ya, ym, yc, w, layer, x, g, b, tm):
    s = x.shape[0]
    return pl.pallas_call(
        _out_kernel,
        grid=(s // tm,),
        in_specs=[pl.BlockSpec((tm, A_WIDTH), lambda i: (i, 0)),
                  pl.BlockSpec((tm, M_WIDTH), lambda i: (i, 0)),
                  pl.BlockSpec((tm, C_WIDTH), lambda i: (i, 0)),
                  pl.BlockSpec((None, D_MODEL, D_MODEL), lambda i: (layer, 0, 0)),
                  pl.BlockSpec((tm, D_MODEL), lambda i: (i, 0)),
                  pl.BlockSpec((1, D_MODEL), lambda i: (0, 0)),
                  pl.BlockSpec((1, D_MODEL), lambda i: (0, 0))],
        out_specs=pl.BlockSpec((tm, D_MODEL), lambda i: (i, 0)),
        out_shape=jax.ShapeDtypeStruct((s, D_MODEL), F32),
        compiler_params=_cparams("arbitrary"),
        name="out_proj_ln",
    )(ya, ym, yc, w, x, g, b)


def _ffn_kernel(x_ref, wu_ref, wd_ref, g_ref, b_ref, o_ref, xb_ref, acc_ref):
    f = pl.program_id(1)

    @pl.when(f == 0)
    def _():
        xb_ref[...] = x_ref[...].astype(BF16)
        acc_ref[...] = jnp.zeros(acc_ref.shape, F32)

    hdn = jnp.maximum(jnp.dot(xb_ref[...], wu_ref[...], preferred_element_type=F32), 0.0)
    acc_ref[...] += jnp.dot((hdn * hdn).astype(BF16), wd_ref[...], preferred_element_type=F32)

    @pl.when(f == pl.num_programs(1) - 1)
    def _():
        o_ref[...] = _layer_norm(ALPHA * x_ref[...] + acc_ref[...], g_ref[...], b_ref[...])


def _ffn(x, wu, wd, layer, g, b, tm, tf):
    s = x.shape[0]
    return pl.pallas_call(
        _ffn_kernel,
        grid=(s // tm, D_FF // tf),
        in_specs=[pl.BlockSpec((tm, D_MODEL), lambda i, f: (i, 0)),
                  pl.BlockSpec((None, D_MODEL, tf), lambda i, f: (layer, 0, f)),
                  pl.BlockSpec((None, tf, D_MODEL), lambda i, f: (layer, f, 0)),
                  pl.BlockSpec((1, D_MODEL), lambda i, f: (0, 0)),
                  pl.BlockSpec((1, D_MODEL), lambda i, f: (0, 0))],
        out_specs=pl.BlockSpec((tm, D_MODEL), lambda i, f: (i, 0)),
        out_shape=jax.ShapeDtypeStruct((s, D_MODEL), F32),
        scratch_shapes=[pltpu.VMEM((tm, D_MODEL), BF16),
                        pltpu.VMEM((tm, D_MODEL), F32)],
        compiler_params=_cparams("arbitrary", "arbitrary"),
        name="ffn_ln",
    )(x, wu, wd, g, b)


def _seg(w, i):
    return w[..., _OFFS[i]:_OFFS[i] + SIZES[i]]


def _pad_cols(w, width):
    return jnp.pad(w, [(0, 0)] * (w.ndim - 1) + [(0, width - w.shape[-1])])


def _layout_w_in(w_in):
    w_in = w_in.astype(BF16)
    (q_a, k_a, v_a, q_i, k_i, w_i, q_m, k_m, v_m, o_m, i_m, f_m, q_c, k_c, v_c) = [
        _seg(w_in, i) for i in range(len(SIZES))]
    wb = jnp.concatenate([q_a, k_a, v_a, v_m, q_c, k_c, v_c, q_i, k_i, k_i], axis=-1)
    wf = jnp.concatenate([q_m, k_m, o_m, _pad_cols(w_i, LANES), _pad_cols(i_m, LANES),
                          _pad_cols(f_m, LANES)], axis=-1)
    return _pad_cols(wb, PB_WIDTH), _pad_cols(wf, PF_WIDTH)


def _tile(n, pref):
    t = min(n, pref)
    assert n % t == 0, (n, t)
    return t


def kernel(x, w_in, conv_m, b_i, b_f, m_norm_g, lam_q1, lam_k1, lam_q2, lam_k2, c_norm_g, w_out,
           ln1_g, ln1_b, w_up, w_down, ln2_g, ln2_b):
    batch, s, d = x.shape
    assert batch == 1 and d == D_MODEL
    ksel = min(TOPK_MAX, s // 4)
    wb_all, wf_all = _layout_w_in(w_in)
    w_out_b = w_out.astype(BF16)
    w_up_b = w_up.astype(BF16)
    w_down_b = w_down.astype(BF16)
    conv_w = conv_m.reshape(DEPTH, CONV_W, 2 * M_WIDTH)
    b_i_p = _pad_cols(b_i, LANES).reshape(DEPTH, 1, LANES)
    b_f_p = _pad_cols(b_f, LANES).reshape(DEPTH, 1, LANES)
    lam_p = jnp.stack([lam_q1, lam_k1, lam_q2, lam_k2], axis=1)

    tm_proj = _tile(s, 1024)
    t_attn = _tile(s, 512)
    tq_idx = _tile(s, 256)
    tk_idx = _tile(s, 512)
    l_chunk = _tile(s, 256)
    tm_out = _tile(s, 512)
    tm_ffn = _tile(s, 512)

    h = x.reshape(s, d)
    for l in range(DEPTH):
        pb = _matmul(h, wb_all, l, BF16, tm_proj, 768, "proj_bf16")
        pf = _matmul(h, wf_all, l, F32, tm_proj, 768, "proj_f32")
        mask = _dsa_mask(pb, pf, ksel, tq_idx, tk_idx)
        lam_init = 0.8 - 0.6 * math.exp(-0.3 * l)
        y_a, y_c = _attention_pair(pb, mask, lam_p[l], c_norm_g[l].reshape(1, C_WIDTH), lam_init, t_attn)
        y_m = _mlstm(pb, pf, conv_w[l], b_i_p[l], b_f_p[l], m_norm_g[l].reshape(1, M_WIDTH), l_chunk)
        h = _out_proj(y_a, y_m, y_c, w_out_b, l, h, ln1_g[l].reshape(1, d), ln1_b[l].reshape(1, d), tm_out)
        h = _ffn(h, w_up_b, w_down_b, l, ln2_g[l].reshape(1, d), ln2_b[l].reshape(1, d), tm_ffn, 1024)
    return h.reshape(batch, s, d)
```

```python
import functools
import math

import jax
import jax.numpy as jnp
import numpy as np
from jax import lax
from jax.experimental import pallas as pl
from jax.experimental.pallas import tpu as pltpu

F32 = jnp.float32
BF16 = jnp.bfloat16
I32 = jnp.int32

D_MODEL = 2048
DEPTH = 4
HEAD_DIM = 128
A_HEADS = 6
A_WIDTH = A_HEADS * HEAD_DIM
IDX_HEADS = 8
IDX_DIM = 64
TOPK_MAX = 256
M_HEADS = 6
M_WIDTH = M_HEADS * HEAD_DIM
CONV_W = 4
C_HEADS = 4
C_QK = 64
C_WIDTH = C_HEADS * 2 * C_QK
D_FF = 4 * D_MODEL
ALPHA = (2.0 * DEPTH) ** 0.25
EPS = 1e-5

SIZES = (A_WIDTH, A_WIDTH, A_WIDTH, IDX_HEADS * IDX_DIM, IDX_DIM, IDX_HEADS,
         M_WIDTH, M_WIDTH, M_WIDTH, M_WIDTH, M_HEADS, M_HEADS,
         C_WIDTH, C_WIDTH, C_WIDTH)
_OFFS = tuple(int(sum(SIZES[:i])) for i in range(len(SIZES)))

LANES = 128
SUBLANES = 8
VMEM_LIMIT_BYTES = 56 * 1024 * 1024
INT_MIN = -2 ** 31
NEG_INF = float("-inf")
LOG2E = math.log2(math.e)

PB_QA, PB_KA, PB_VA, PB_VM = 0, 768, 1536, 2304
PB_QC, PB_KC, PB_VC, PB_QI = 3072, 3584, 4096, 4608
PB_KK = 5120
PB_WIDTH = 5376
PF_QM, PF_KM, PF_OM = 0, 768, 1536
PF_WI, PF_GI, PF_GF = 2304, 2432, 2560
PF_WIDTH = 3072


def _cparams(*sem):
    return pltpu.CompilerParams(dimension_semantics=sem, vmem_limit_bytes=VMEM_LIMIT_BYTES)


def _alibi_slopes(n):
    return [2.0 ** (-8.0 * (h + 1) / n) for h in range(n)]


def _mm_kernel(x_ref, w_ref, o_ref, xb_ref):
    @pl.when(pl.program_id(1) == 0)
    def _():
        xb_ref[...] = x_ref[...].astype(BF16)

    o_ref[...] = jnp.dot(xb_ref[...], w_ref[...], preferred_element_type=F32).astype(o_ref.dtype)


def _matmul(x, w, layer, out_dtype, tm, tn, name):
    m, k = x.shape
    n = w.shape[2]
    return pl.pallas_call(
        _mm_kernel,
        grid=(m // tm, n // tn),
        in_specs=[pl.BlockSpec((tm, k), lambda i, j: (i, 0)),
                  pl.BlockSpec((None, k, tn), lambda i, j: (layer, 0, j))],
        out_specs=pl.BlockSpec((tm, tn), lambda i, j: (i, j)),
        out_shape=jax.ShapeDtypeStruct((m, n), out_dtype),
        scratch_shapes=[pltpu.VMEM((tm, k), BF16)],
        compiler_params=_cparams("arbitrary", "arbitrary"),
        name=name,
    )(x, w)


def _order_key(x):
    bits = lax.bitcast_convert_type(x, I32)
    key = jnp.where(bits < 0, bits ^ jnp.int32(0x7FFFFFFF), bits)
    return jnp.where(bits == jnp.int32(INT_MIN), 0, key)


def _idx_kernel(qi_ref, kk_ref, wi_ref, mask_ref, q8_ref, w8_ref, keys_ref, gm_ref,
                *, tq, tk, nk, ksel, jbits):
    i = pl.program_id(0)
    nck = ((i + 1) * tq + tk - 1) // tk
    classes = gm_ref.shape[0]
    acc_rows = 32

    lane = lax.broadcasted_iota(I32, (tq, LANES), 1)
    lo_half = jnp.where(lane < IDX_DIM, 1.0, 0.0).astype(F32)
    hi_half = 1.0 - lo_half
    for h in range(IDX_HEADS):
        qp = qi_ref[:, (h // 2) * LANES:(h // 2 + 1) * LANES].astype(F32)
        q8_ref[h * tq:(h + 1) * tq, :] = (qp * (lo_half if h % 2 == 0 else hi_half)).astype(BF16)
    w8_ref[...] = wi_ref[...].T[0:SUBLANES, :]
    gm_ref[...] = jnp.full(gm_ref.shape, NEG_INF, F32)

    q_idx = i * tq + lax.broadcasted_iota(I32, (tk, tq), 1)
    k_off = lax.broadcasted_iota(I32, (tk, tq), 0)

    def score_chunk(c, carry):
        kc = kk_ref[pl.ds(pl.multiple_of(c * tk, tk), tk), :]
        y = lax.dot_general(kc, q8_ref[...], (((1,), (1,)), ((), ())), preferred_element_type=F32)
        sc = jnp.maximum(y[:, 0:tq], 0.0) * w8_ref[0:1, :]
        for h in range(1, IDX_HEADS):
            sc = sc + jnp.maximum(y[:, h * tq:(h + 1) * tq], 0.0) * w8_ref[h:h + 1, :]
        causal = c * tk + k_off <= q_idx
        keys_ref[c] = jnp.where(causal, _order_key(sc), jnp.int32(INT_MIN))
        scm = jnp.where(causal, sc, NEG_INF)
        for cls in range(classes):
            g = gm_ref[cls]
            for r in range(cls * SUBLANES, tk, classes * SUBLANES):
                g = jnp.maximum(g, scm[r:r + SUBLANES, :])
            gm_ref[cls] = g
        return carry

    lax.fori_loop(0, nck, score_chunk, 0)

    def count(pred):
        def body(c, acc):
            for r in range(0, tk, acc_rows):
                acc = jnp.where(pred(keys_ref[c, r:r + acc_rows, :], c * tk + r), acc + 1.0, acc)
            return acc
        acc = lax.fori_loop(0, nck, body, jnp.zeros((acc_rows, tq), F32))
        return jnp.sum(acc, axis=0, keepdims=True)

    g_min = gm_ref[0]
    g_max = gm_ref[0]
    for cls in range(1, classes):
        g_min = jnp.minimum(g_min, gm_ref[cls])
        g_max = jnp.maximum(g_max, gm_ref[cls])
    lo_f = jnp.min(g_min, axis=0, keepdims=True)
    hi_f = jnp.max(g_max, axis=0, keepdims=True)
    n_valid = i * tq + lax.broadcasted_iota(I32, (1, tq), 1) + 1
    few = n_valid <= ksel
    lo0 = jnp.where(jnp.logical_or(few, lo_f == NEG_INF), jnp.int32(INT_MIN), _order_key(lo_f))
    hi0 = jnp.where(few, jnp.int32(INT_MIN), _order_key(hi_f))

    def unresolved(st):
        lo, hi = st
        return jnp.max(jnp.where(lo < hi, 1.0, 0.0)) > 0.0

    def bisect(st):
        lo, hi = st
        mid = (lo >> 1) + (hi >> 1) + ((lo | hi) & 1)
        cnt = count(lambda k, r0: k >= mid)
        ge = cnt >= ksel
        exact = cnt == ksel
        return (jnp.where(ge, mid, lo), jnp.where(exact, mid, jnp.where(ge, hi, mid - 1)))

    thr, _ = lax.while_loop(unresolved, bisect, (lo0, hi0))
    n_gt = count(lambda k, r0: k > thr)
    n_ge = count(lambda k, r0: k >= thr)
    need = ksel - n_gt
    excess = jnp.logical_and(thr > jnp.int32(INT_MIN), n_ge > ksel)
    any_excess = jnp.max(jnp.where(excess, 1.0, 0.0)) > 0.0
    sub = lax.broadcasted_iota(I32, (acc_rows, tq), 0)

    def tie_search():
        def jbody(b, jc):
            cand = jc + lax.shift_left(jnp.int32(1), jbits - 1 - b)
            cnt = count(lambda k, r0: jnp.logical_and(k == thr, r0 + sub < cand))
            return jnp.where(cnt <= need, cand, jc)
        return lax.fori_loop(0, jbits, jbody, jnp.zeros((1, tq), I32))

    def write_mask(select):
        for c in range(nk):
            @pl.when(c < nck)
            def _(c=c):
                sel = jnp.where(select(keys_ref[c], c), 1.0, 0.0).astype(F32)
                mask_ref[:, c * tk:(c + 1) * tk] = sel.T.astype(BF16)

            @pl.when(c >= nck)
            def _(c=c):
                mask_ref[:, c * tk:(c + 1) * tk] = jnp.zeros((tq, tk), BF16)

    @pl.when(any_excess)
    def _():
        jcut = tie_search()

        def select(key, c):
            tie = jnp.logical_and(key == thr, c * tk + k_off < jcut)
            return jnp.logical_and(key > jnp.int32(INT_MIN), jnp.logical_or(key > thr, tie))
        write_mask(select)

    @pl.when(jnp.logical_not(any_excess))
    def _():
        floor = jnp.maximum(thr, jnp.int32(INT_MIN + 1))
        write_mask(lambda key, c: key >= floor)


def _dsa_mask(pb, pf, ksel, tq, tk):
    s = pb.shape[0]
    nk = s // tk
    jbits = int(s).bit_length()
    classes = min(32, tk // SUBLANES)
    assert ksel <= SUBLANES * classes, "the threshold bracket relies on that many disjoint key groups"
    kern = functools.partial(_idx_kernel, tq=tq, tk=tk, nk=nk, ksel=ksel, jbits=jbits)
    return pl.pallas_call(
        kern,
        grid=(s // tq,),
        in_specs=[pl.BlockSpec((tq, IDX_HEADS * IDX_DIM), lambda i: (i, PB_QI // (IDX_HEADS * IDX_DIM))),
                  pl.BlockSpec((s, LANES), lambda i: (0, PB_KK // LANES)),
                  pl.BlockSpec((tq, LANES), lambda i: (i, PF_WI // LANES))],
        out_specs=pl.BlockSpec((tq, s), lambda i: (i, 0)),
        out_shape=jax.ShapeDtypeStruct((s, s), BF16),
        scratch_shapes=[pltpu.VMEM((IDX_HEADS * tq, LANES), BF16),
                        pltpu.VMEM((SUBLANES, tq), F32),
                        pltpu.VMEM((nk, tk, tq), I32),
                        pltpu.VMEM((classes, SUBLANES, tq), F32)],
        compiler_params=_cparams("arbitrary"),
        name="dsa_index_mask",
    )(pb, pb, pf)


def _causal_steps(n):
    qi = [i for i in range(n) for _ in range(i + 1)]
    kj = [j for i in range(n) for j in range(i + 1)]
    return jnp.asarray(qi, I32), jnp.asarray(kj, I32)


def _bf16_terms(x):
    out = []
    r = np.float32(x)
    for _ in range(3):
        t = np.float32(np.asarray(r, dtype=jnp.bfloat16))
        out.append(float(t))
        r = np.float32(r - t)
    return out


def _alibi_q_ext(slope, rows):
    s1, s2, s3 = _bf16_terms(slope * LOG2E)
    lane = lax.broadcasted_iota(I32, (rows, LANES), 1)
    ext = jnp.where(lane < 3, s1, jnp.where(lane < 6, s2, jnp.where(lane < 9, s3, 0.0)))
    return ext.astype(BF16)


def _alibi_k_ext(first_pos, rows):
    pos = (first_pos + lax.broadcasted_iota(I32, (rows, LANES), 0)).astype(F32)
    p1, p2, p3 = _split3(pos)
    lane = lax.broadcasted_iota(I32, (rows, LANES), 1)
    sel = lane % 3
    ext = jnp.where(sel == 0, p1.astype(F32), jnp.where(sel == 1, p2.astype(F32), p3.astype(F32)))
    return jnp.where(lane < 9, ext, 0.0).astype(BF16)


def _dsa_attn_kernel(qi_ref, kj_ref, q_ref, k_ref, v_ref, mask_ref, o_ref, qa_ref, mb_ref, s_ref, m_ref,
                     acc_ref, *, tq, tk):
    t = pl.program_id(0)
    i = qi_ref[t]
    j = kj_ref[t]
    slopes = _alibi_slopes(A_HEADS)
    reps = tk // LANES

    @pl.when(j == 0)
    def _():
        m_ref[...] = jnp.full(m_ref.shape, NEG_INF, F32)
        acc_ref[...] = jnp.zeros(acc_ref.shape, F32)
        for h in range(A_HEADS):
            hs = slice(h * HEAD_DIM, (h + 1) * HEAD_DIM)
            qh = (q_ref[:, hs].astype(F32) * (HEAD_DIM ** -0.5 * LOG2E)).astype(BF16)
            qa_ref[h] = jnp.concatenate([qh, _alibi_q_ext(slopes[h], tq)], axis=1)

    mf = mask_ref[...].astype(F32)
    mb_ref[...] = jnp.where(mf > 0.0, mf - 1.0, NEG_INF)
    k_ext = _alibi_k_ext(j * tk - i * tq, tk)
    ones = jnp.ones((tk, LANES), BF16)

    def scores(h):
        hs = slice(h * HEAD_DIM, (h + 1) * HEAD_DIM)
        ka = jnp.concatenate([k_ref[:, hs], k_ext], axis=1)
        s = lax.dot_general(qa_ref[h], ka, (((1,), (1,)), ((), ())), preferred_element_type=F32)
        s = s + mb_ref[...]
        s_ref[h % 2] = s
        m_prev = m_ref[h]
        m_cur = jnp.maximum(m_prev, jnp.max(s, axis=1, keepdims=True))
        m_ref[h] = m_cur
        m_safe = jnp.where(m_cur == NEG_INF, 0.0, m_cur)
        return m_safe, jnp.exp2(m_prev - m_safe)

    def accumulate(h, m_safe, a):
        hs = slice(h * HEAD_DIM, (h + 1) * HEAD_DIM)
        p = jnp.exp2(s_ref[h % 2] - jnp.tile(m_safe, (1, reps)))
        va = jnp.concatenate([v_ref[:, hs], ones], axis=1)
        acc_ref[h] = jnp.tile(a, (1, 2)) * acc_ref[h] + jnp.dot(p.astype(BF16), va,
                                                                 preferred_element_type=F32)

    stats = scores(0)
    for h in range(A_HEADS):
        nxt = scores(h + 1) if h + 1 < A_HEADS else None
        accumulate(h, *stats)
        stats = nxt

    @pl.when(j == i)
    def _():
        for h in range(A_HEADS):
            hs = slice(h * HEAD_DIM, (h + 1) * HEAD_DIM)
            o_ref[:, hs] = (acc_ref[h, :, 0:HEAD_DIM] / acc_ref[h, :, HEAD_DIM:2 * HEAD_DIM]).astype(o_ref.dtype)


def _diff_attn_kernel(qi_ref, kj_ref, q_ref, k_ref, v_ref, lam_ref, g_ref, o_ref, qa_ref, s_ref, m_ref,
                      acc_ref, *, tq, tk, lam_init):
    t = pl.program_id(0)
    i = qi_ref[t]
    j = kj_ref[t]
    slopes = _alibi_slopes(C_HEADS)
    reps = tk // LANES

    @pl.when(j == 0)
    def _():
        m_ref[...] = jnp.full(m_ref.shape, NEG_INF, F32)
        acc_ref[...] = jnp.zeros(acc_ref.shape, F32)
        lane = lax.broadcasted_iota(I32, (tq, LANES), 1)
        for h in range(C_HEADS):
            qf = q_ref[:, h * LANES:(h + 1) * LANES].astype(F32) * (C_QK ** -0.5 * LOG2E)
            ext = _alibi_q_ext(slopes[h], tq)
            qa_ref[2 * h] = jnp.concatenate([jnp.where(lane < C_QK, qf, 0.0).astype(BF16), ext], axis=1)
            qa_ref[2 * h + 1] = jnp.concatenate([jnp.where(lane < C_QK, 0.0, qf).astype(BF16), ext], axis=1)

    def step(diagonal):
        k_ext = _alibi_k_ext(j * tk - i * tq, tk)
        ones = jnp.ones((tk, LANES), BF16)
        if diagonal:
            causal = lax.broadcasted_iota(I32, (tq, tk), 1) <= lax.broadcasted_iota(I32, (tq, tk), 0)

        def scores(u):
            h = u // 2
            ka = jnp.concatenate([k_ref[:, h * LANES:(h + 1) * LANES], k_ext], axis=1)
            s = lax.dot_general(qa_ref[u], ka, (((1,), (1,)), ((), ())), preferred_element_type=F32)
            if diagonal:
                s = jnp.where(causal, s, NEG_INF)
            s_ref[u % 2] = s
            m_prev = m_ref[u]
            m_cur = jnp.maximum(m_prev, jnp.max(s, axis=1, keepdims=True))
            m_ref[u] = m_cur
            return m_cur, jnp.exp2(m_prev - m_cur)

        def accumulate(u, m_cur, a):
            h = u // 2
            p = jnp.exp2(s_ref[u % 2] - jnp.tile(m_cur, (1, reps)))
            va = jnp.concatenate([v_ref[:, h * LANES:(h + 1) * LANES], ones], axis=1)
            acc_ref[u] = jnp.tile(a, (1, 2)) * acc_ref[u] + jnp.dot(p.astype(BF16), va,
                                                                     preferred_element_type=F32)

        stats = scores(0)
        for u in range(2 * C_HEADS):
            nxt = scores(u + 1) if u + 1 < 2 * C_HEADS else None
            accumulate(u, *stats)
            stats = nxt

    @pl.when(j < i)
    def _():
        step(False)

    @pl.when(j == i)
    def _():
        step(True)
        lp = lam_ref[...]
        lam = (jnp.exp(jnp.sum(lp[0:1] * lp[1:2], axis=1, keepdims=True))
               - jnp.exp(jnp.sum(lp[2:3] * lp[3:4], axis=1, keepdims=True)) + lam_init)
        for h in range(C_HEADS):
            hs = slice(h * LANES, (h + 1) * LANES)
            o = (acc_ref[2 * h, :, 0:LANES] / acc_ref[2 * h, :, LANES:2 * LANES]
                 - lam * (acc_ref[2 * h + 1, :, 0:LANES] / acc_ref[2 * h + 1, :, LANES:2 * LANES]))
            ms = jnp.mean(o * o, axis=1, keepdims=True)
            o_ref[:, hs] = (o * lax.rsqrt(ms + EPS) * g_ref[:, hs] * (1.0 - lam_init)).astype(o_ref.dtype)


_N_DSA_SCRATCH = 5


def _attn_pair_kernel(qi_ref, kj_ref, qa_ref, ka_ref, va_ref, mask_ref, qc_ref, kc_ref, vc_ref, lam_ref, g_ref,
                      oa_ref, oc_ref, *scratch, tq, tk, lam_init):
    _dsa_attn_kernel(qi_ref, kj_ref, qa_ref, ka_ref, va_ref, mask_ref, oa_ref, *scratch[:_N_DSA_SCRATCH],
                     tq=tq, tk=tk)
    _diff_attn_kernel(qi_ref, kj_ref, qc_ref, kc_ref, vc_ref, lam_ref, g_ref, oc_ref,
                      *scratch[_N_DSA_SCRATCH:], tq=tq, tk=tk, lam_init=lam_init)


def _attention_pair(pb, mask, lam_params, g, lam_init, tq):
    s = pb.shape[0]
    kern = functools.partial(_attn_pair_kernel, tq=tq, tk=tq, lam_init=lam_init)
    qi, kj = _causal_steps(s // tq)
    q_blk = lambda width, col: pl.BlockSpec((tq, width), lambda t, qi, kj: (qi[t], col // width))
    kv_blk = lambda width, col: pl.BlockSpec((tq, width), lambda t, qi, kj: (kj[t], col // width))
    grid_spec = pltpu.PrefetchScalarGridSpec(
        num_scalar_prefetch=2,
        grid=(int(qi.shape[0]),),
        in_specs=[q_blk(A_WIDTH, PB_QA), kv_blk(A_WIDTH, PB_KA), kv_blk(A_WIDTH, PB_VA),
                  pl.BlockSpec((tq, tq), lambda t, qi, kj: (qi[t], kj[t])),
                  q_blk(C_WIDTH, PB_QC), kv_blk(C_WIDTH, PB_KC), kv_blk(C_WIDTH, PB_VC),
                  pl.BlockSpec((4, C_QK), lambda t, qi, kj: (0, 0)),
                  pl.BlockSpec((1, C_WIDTH), lambda t, qi, kj: (0, 0))],
        out_specs=[pl.BlockSpec((tq, A_WIDTH), lambda t, qi, kj: (qi[t], 0)),
                   pl.BlockSpec((tq, C_WIDTH), lambda t, qi, kj: (qi[t], 0))],
        scratch_shapes=[pltpu.VMEM((A_HEADS, tq, 2 * HEAD_DIM), BF16),
                        pltpu.VMEM((tq, tq), F32),
                        pltpu.VMEM((2, tq, tq), F32),
                        pltpu.VMEM((A_HEADS, tq, LANES), F32),
                        pltpu.VMEM((A_HEADS, tq, 2 * HEAD_DIM), F32),
                        pltpu.VMEM((2 * C_HEADS, tq, 2 * LANES), BF16),
                        pltpu.VMEM((2, tq, tq), F32),
                        pltpu.VMEM((2 * C_HEADS, tq, LANES), F32),
                        pltpu.VMEM((2 * C_HEADS, tq, 2 * LANES), F32)])
    return pl.pallas_call(
        kern,
        grid_spec=grid_spec,
        out_shape=[jax.ShapeDtypeStruct((s, A_WIDTH), BF16), jax.ShapeDtypeStruct((s, C_WIDTH), BF16)],
        compiler_params=_cparams("arbitrary"),
        name="attention_pair",
    )(qi, kj, pb, pb, pb, mask, pb, pb, pb, lam_params, g)


def _split3(x):
    x1 = x.astype(BF16)
    r1 = x - x1.astype(F32)
    x2 = r1.astype(BF16)
    x3 = (r1 - x2.astype(F32)).astype(BF16)
    return x1, x2, x3


def _mlstm_kernel(q_ref, k_ref, qh_ref, kh_ref, v_ref, o_ref, gi_ref, gf_ref, cw_ref, bi_ref, bf_ref,
                  g_ref, y_ref, xq_ref, xk_ref, c_ref, n_ref, m_ref, *, L):
    c = pl.program_id(0)

    @pl.when(c == 0)
    def _():
        c_ref[...] = jnp.zeros(c_ref.shape, F32)
        n_ref[...] = jnp.zeros(n_ref.shape, F32)
        m_ref[...] = jnp.zeros(m_ref.shape, F32)

    first = (c > 0).astype(F32)
    xq_ref[0:SUBLANES, :] = qh_ref[...] * first
    xq_ref[SUBLANES:SUBLANES + L, :] = q_ref[...]
    xk_ref[0:SUBLANES, :] = kh_ref[...] * first
    xk_ref[SUBLANES:SUBLANES + L, :] = k_ref[...]
    qc = jnp.zeros((L, M_WIDTH), F32)
    kc = jnp.zeros((L, M_WIDTH), F32)
    for t in range(CONV_W):
        off = SUBLANES - (CONV_W - 1) + t
        qc = qc + xq_ref[off:off + L, :] * cw_ref[t:t + 1, 0:M_WIDTH]
        kc = kc + xk_ref[off:off + L, :] * cw_ref[t:t + 1, M_WIDTH:2 * M_WIDTH]
    qc = qc * jax.nn.sigmoid(qc)
    kc = kc * jax.nn.sigmoid(kc) * (HEAD_DIM ** -0.5)

    li = gi_ref[...] + bi_ref[...]
    fp = gf_ref[...] + bf_ref[...]
    lf = jnp.minimum(fp, 0.0) - jnp.log(1.0 + jnp.exp(-jnp.abs(fp)))
    r_i = lax.broadcasted_iota(I32, (L, L), 0)
    c_i = lax.broadcasted_iota(I32, (L, L), 1)
    tril = c_i <= r_i
    tril_b = jnp.where(tril, 1.0, 0.0).astype(BF16)
    a = jnp.zeros((L, LANES), F32)
    for part in _split3(lf):
        a = a + jnp.dot(tril_b, part, preferred_element_type=F32)
    b = li - a
    g_tot = a[L - 1:L, :]
    b_t = b.T

    m_all = m_ref[...]
    m_new_all = m_all
    for h in range(M_HEADS):
        hs = slice(h * HEAD_DIM, (h + 1) * HEAD_DIM)
        qh = qc[:, hs]
        kh = kc[:, hs]
        vh = v_ref[:, hs]
        qb = qh.astype(BF16)
        kb = kh.astype(BF16)
        m_prev = m_all[:, h:h + 1]
        d = jnp.where(tril, b_t[h:h + 1, :], NEG_INF)
        mm = jnp.maximum(m_prev, jnp.max(d, axis=1, keepdims=True))
        w_intra = jnp.exp(d - mm)
        w_inter = jnp.exp(m_prev - mm)
        qk = lax.dot_general(qb, kb, (((1,), (1,)), ((), ())), preferred_element_type=F32) * w_intra
        c_h = c_ref[h]
        num = (jnp.dot(qk.astype(BF16), vh, preferred_element_type=F32)
               + w_inter * jnp.dot(qb, c_h.astype(BF16), preferred_element_type=F32))
        den = (jnp.sum(qk, axis=1, keepdims=True)
               + w_inter * jnp.sum(qh * n_ref[h:h + 1, :], axis=1, keepdims=True))
        m_row = a[:, h:h + 1] + mm
        hh = num / jnp.maximum(jnp.abs(den), jnp.exp(-m_row))
        mm_last = mm[L - 1:L, :]
        ws = jnp.exp(b[:, h:h + 1] - mm_last)
        wc = jnp.exp(m_prev - mm_last)
        c_ref[h] = wc * c_h + lax.dot_general(kb, (ws * vh.astype(F32)).astype(BF16),
                                              (((0,), (0,)), ((), ())), preferred_element_type=F32)
        n_ref[h:h + 1, :] = wc * n_ref[h:h + 1, :] + jnp.sum(ws * kh, axis=0, keepdims=True)
        lane = lax.broadcasted_iota(I32, (1, LANES), 1)
        m_new_all = jnp.where(lane == h, g_tot + mm_last, m_new_all)
        ms = jnp.mean(hh * hh, axis=1, keepdims=True)
        hn = hh * lax.rsqrt(ms + EPS) * g_ref[:, hs]
        y_ref[:, hs] = (jax.nn.sigmoid(o_ref[:, hs]) * hn).astype(y_ref.dtype)
    m_ref[...] = m_new_all


def _mlstm(pb, pf, conv_w, b_i, b_f, g, L):
    s = pb.shape[0]
    kern = functools.partial(_mlstm_kernel, L=L)
    halo = lambda c: jnp.maximum(c * (L // SUBLANES) - 1, 0)
    return pl.pallas_call(
        kern,
        grid=(s // L,),
        in_specs=[pl.BlockSpec((L, M_WIDTH), lambda c: (c, PF_QM // M_WIDTH)),
                  pl.BlockSpec((L, M_WIDTH), lambda c: (c, PF_KM // M_WIDTH)),
                  pl.BlockSpec((SUBLANES, M_WIDTH), lambda c: (halo(c), PF_QM // M_WIDTH)),
                  pl.BlockSpec((SUBLANES, M_WIDTH), lambda c: (halo(c), PF_KM // M_WIDTH)),
                  pl.BlockSpec((L, M_WIDTH), lambda c: (c, PB_VM // M_WIDTH)),
                  pl.BlockSpec((L, M_WIDTH), lambda c: (c, PF_OM // M_WIDTH)),
                  pl.BlockSpec((L, LANES), lambda c: (c, PF_GI // LANES)),
                  pl.BlockSpec((L, LANES), lambda c: (c, PF_GF // LANES)),
                  pl.BlockSpec((CONV_W, 2 * M_WIDTH), lambda c: (0, 0)),
                  pl.BlockSpec((1, LANES), lambda c: (0, 0)),
                  pl.BlockSpec((1, LANES), lambda c: (0, 0)),
                  pl.BlockSpec((1, M_WIDTH), lambda c: (0, 0))],
        out_specs=pl.BlockSpec((L, M_WIDTH), lambda c: (c, 0)),
        out_shape=jax.ShapeDtypeStruct((s, M_WIDTH), BF16),
        scratch_shapes=[pltpu.VMEM((L + SUBLANES, M_WIDTH), F32),
                        pltpu.VMEM((L + SUBLANES, M_WIDTH), F32),
                        pltpu.VMEM((M_HEADS, HEAD_DIM, HEAD_DIM), F32),
                        pltpu.VMEM((SUBLANES, HEAD_DIM), F32),
                        pltpu.VMEM((1, LANES), F32)],
        compiler_params=_cparams("arbitrary"),
        name="mlstm",
    )(pf, pf, pf, pf, pb, pf, pf, pf, conv_w, b_i, b_f, g)


def _layer_norm(z, g, b):
    mu = jnp.mean(z, axis=-1, keepdims=True)
    zc = z - mu
    var = jnp.mean(zc * zc, axis=-1, keepdims=True)
    return zc * lax.rsqrt(var + EPS) * g + b


def _out_kernel(ya_ref, ym_ref, yc_ref, w_ref, x_ref, g_ref, b_ref, o_ref):
    acc = jnp.dot(ya_ref[...], w_ref[0:A_WIDTH, :], preferred_element_type=F32)
    acc = acc + jnp.dot(ym_ref[...], w_ref[A_WIDTH:A_WIDTH + M_WIDTH, :], preferred_element_type=F32)
    acc = acc + jnp.dot(yc_ref[...], w_ref[A_WIDTH + M_WIDTH:D_MODEL, :], preferred_element_type=F32)
    o_ref[...] = _layer_norm(ALPHA * x_ref[...] + acc, g_ref[...], b_ref[...])


def _out_proj(ya, ym, yc, w, layer, x, g, b, tm):
    s = x.shape[0]
    return pl.pallas_call(
        _out_kernel,
        grid=(s // tm,),
        in_specs=[pl.BlockSpec((tm, A_WIDTH), lambda i: (i, 0)),
                  pl.BlockSpec((tm, M_WIDTH), lambda i: (i, 0)),
                  pl.BlockSpec((tm, C_WIDTH), lambda i: (i, 0)),
                  pl.BlockSpec((None, D_MODEL, D_MODEL), lambda i: (layer, 0, 0)),
                  pl.BlockSpec((tm, D_MODEL), lambda i: (i, 0)),
                  pl.BlockSpec((1, D_MODEL), lambda i: (0, 0)),
                  pl.BlockSpec((1, D_MODEL), lambda i: (0, 0))],
        out_specs=pl.BlockSpec((tm, D_MODEL), lambda i: (i, 0)),
        out_shape=jax.ShapeDtypeStruct((s, D_MODEL), F32),
        compiler_params=_cparams("arbitrary"),
        name="out_proj_ln",
    )(ya, ym, yc, w, x, g, b)


def _ffn_kernel(x_ref, wu_ref, wd_ref, g_ref, b_ref, o_ref, xb_ref, acc_ref):
    f = pl.program_id(1)

    @pl.when(f == 0)
    def _():
        xb_ref[...] = x_ref[...].astype(BF16)
        acc_ref[...] = jnp.zeros(acc_ref.shape, F32)

    hdn = jnp.maximum(jnp.dot(xb_ref[...], wu_ref[...], preferred_element_type=F32), 0.0)
    acc_ref[...] += jnp.dot((hdn * hdn).astype(BF16), wd_ref[...], preferred_element_type=F32)

    @pl.when(f == pl.num_programs(1) - 1)
    def _():
        o_ref[...] = _layer_norm(ALPHA * x_ref[...] + acc_ref[...], g_ref[...], b_ref[...])


def _ffn(x, wu, wd, layer, g, b, tm, tf):
    s = x.shape[0]
    return pl.pallas_call(
        _ffn_kernel,
        grid=(s // tm, D_FF // tf),
        in_specs=[pl.BlockSpec((tm, D_MODEL), lambda i, f: (i, 0)),
                  pl.BlockSpec((None, D_MODEL, tf), lambda i, f: (layer, 0, f)),
                  pl.BlockSpec((None, tf, D_MODEL), lambda i, f: (layer, f, 0)),
                  pl.BlockSpec((1, D_MODEL), lambda i, f: (0, 0)),
                  pl.BlockSpec((1, D_MODEL), lambda i, f: (0, 0))],
        out_specs=pl.BlockSpec((tm, D_MODEL), lambda i, f: (i, 0)),
        out_shape=jax.ShapeDtypeStruct((s, D_MODEL), F32),
        scratch_shapes=[pltpu.VMEM((tm, D_MODEL), BF16),
                        pltpu.VMEM((tm, D_MODEL), F32)],
        compiler_params=_cparams("arbitrary", "arbitrary"),
        name="ffn_ln",
    )(x, wu, wd, g, b)


def _seg(w, i):
    return w[..., _OFFS[i]:_OFFS[i] + SIZES[i]]


def _pad_cols(w, width):
    return jnp.pad(w, [(0, 0)] * (w.ndim - 1) + [(0, width - w.shape[-1])])


def _layout_w_in(w_in):
    w_in = w_in.astype(BF16)
    (q_a, k_a, v_a, q_i, k_i, w_i, q_m, k_m, v_m, o_m, i_m, f_m, q_c, k_c, v_c) = [
        _seg(w_in, i) for i in range(len(SIZES))]
    wb = jnp.concatenate([q_a, k_a, v_a, v_m, q_c, k_c, v_c, q_i, k_i, k_i], axis=-1)
    wf = jnp.concatenate([q_m, k_m, o_m, _pad_cols(w_i, LANES), _pad_cols(i_m, LANES),
                          _pad_cols(f_m, LANES)], axis=-1)
    return _pad_cols(wb, PB_WIDTH), _pad_cols(wf, PF_WIDTH)


def _tile(n, pref):
    t = min(n, pref)
    assert n % t == 0, (n, t)
    return t


def kernel(x, w_in, conv_m, b_i, b_f, m_norm_g, lam_q1, lam_k1, lam_q2, lam_k2, c_norm_g, w_out,
           ln1_g, ln1_b, w_up, w_down, ln2_g, ln2_b):
    batch, s, d = x.shape
    assert batch == 1 and d == D_MODEL
    ksel = min(TOPK_MAX, s // 4)
    wb_all, wf_all = _layout_w_in(w_in)
    w_out_b = w_out.astype(BF16)
    w_up_b = w_up.astype(BF16)
    w_down_b = w_down.astype(BF16)
    conv_w = conv_m.reshape(DEPTH, CONV_W, 2 * M_WIDTH)
    b_i_p = _pad_cols(b_i, LANES).reshape(DEPTH, 1, LANES)
    b_f_p = _pad_cols(b_f, LANES).reshape(DEPTH, 1, LANES)
    lam_p = jnp.stack([lam_q1, lam_k1, lam_q2, lam_k2], axis=1)

    tm_proj = _tile(s, 1024)
    t_attn = _tile(s, 512)
    tq_idx = _tile(s, 256)
    tk_idx = _tile(s, 512)
    l_chunk = _tile(s, 256)
    tm_out = _tile(s, 512)
    tm_ffn = _tile(s, 512)

    h = x.reshape(s, d)
    for l in range(DEPTH):
        pb = _matmul(h, wb_all, l, BF16, tm_proj, PB_WIDTH // 3, "proj_bf16")
        pf = _matmul(h, wf_all, l, F32, tm_proj, PF_WIDTH // 2, "proj_f32")
        mask = _dsa_mask(pb, pf, ksel, tq_idx, tk_idx)
        lam_init = 0.8 - 0.6 * math.exp(-0.3 * l)
        y_a, y_c = _attention_pair(pb, mask, lam_p[l], c_norm_g[l].reshape(1, C_WIDTH), lam_init, t_attn)
        y_m = _mlstm(pb, pf, conv_w[l], b_i_p[l], b_f_p[l], m_norm_g[l].reshape(1, M_WIDTH), l_chunk)
        h = _out_proj(y_a, y_m, y_c, w_out_b, l, h, ln1_g[l].reshape(1, d), ln1_b[l].reshape(1, d), tm_out)
        h = _ffn(h, w_up_b, w_down_b, l, ln2_g[l].reshape(1, d), ln2_b[l].reshape(1, d), tm_ffn, 1024)
    return h.reshape(batch, s, d)
```

```python
import functools
import math

import jax
import jax.numpy as jnp
import numpy as np
from jax import lax
from jax.experimental import pallas as pl
from jax.experimental.pallas import tpu as pltpu

F32 = jnp.float32
BF16 = jnp.bfloat16
I32 = jnp.int32

D_MODEL = 2048
DEPTH = 4
HEAD_DIM = 128
A_HEADS = 6
A_WIDTH = A_HEADS * HEAD_DIM
IDX_HEADS = 8
IDX_DIM = 64
TOPK_MAX = 256
M_HEADS = 6
M_WIDTH = M_HEADS * HEAD_DIM
CONV_W = 4
C_HEADS = 4
C_QK = 64
C_WIDTH = C_HEADS * 2 * C_QK
D_FF = 4 * D_MODEL
ALPHA = (2.0 * DEPTH) ** 0.25
EPS = 1e-5

SIZES = (A_WIDTH, A_WIDTH, A_WIDTH, IDX_HEADS * IDX_DIM, IDX_DIM, IDX_HEADS,
         M_WIDTH, M_WIDTH, M_WIDTH, M_WIDTH, M_HEADS, M_HEADS,
         C_WIDTH, C_WIDTH, C_WIDTH)
_OFFS = tuple(int(sum(SIZES[:i])) for i in range(len(SIZES)))

LANES = 128
SUBLANES = 8
VMEM_LIMIT_BYTES = 56 * 1024 * 1024
INT_MIN = -2 ** 31
NEG_INF = float("-inf")
LOG2E = math.log2(math.e)

PB_QA, PB_KA, PB_VA, PB_VM = 0, 768, 1536, 2304
PB_QC, PB_KC, PB_VC, PB_QI = 3072, 3584, 4096, 4608
PB_KK = 5120
PB_WIDTH = 5376
PF_QM, PF_KM, PF_OM = 0, 768, 1536
PF_WI, PF_GI, PF_GF = 2304, 2432, 2560
PF_WIDTH = 3072


def _cparams(*sem):
    return pltpu.CompilerParams(dimension_semantics=sem, vmem_limit_bytes=VMEM_LIMIT_BYTES)


def _alibi_slopes(n):
    return [2.0 ** (-8.0 * (h + 1) / n) for h in range(n)]


def _mm_kernel(x_ref, w_ref, o_ref, xb_ref):
    @pl.when(pl.program_id(1) == 0)
    def _():
        xb_ref[...] = x_ref[...].astype(BF16)

    o_ref[...] = jnp.dot(xb_ref[...], w_ref[...], preferred_element_type=F32).astype(o_ref.dtype)


def _matmul(x, w, layer, out_dtype, tm, tn, name):
    m, k = x.shape
    n = w.shape[2]
    return pl.pallas_call(
        _mm_kernel,
        grid=(m // tm, n // tn),
        in_specs=[pl.BlockSpec((tm, k), lambda i, j: (i, 0)),
                  pl.BlockSpec((None, k, tn), lambda i, j: (layer, 0, j))],
        out_specs=pl.BlockSpec((tm, tn), lambda i, j: (i, j)),
        out_shape=jax.ShapeDtypeStruct((m, n), out_dtype),
        scratch_shapes=[pltpu.VMEM((tm, k), BF16)],
        compiler_params=_cparams("arbitrary", "arbitrary"),
        name=name,
    )(x, w)


def _order_key(x):
    bits = lax.bitcast_convert_type(x, I32)
    key = jnp.where(bits < 0, bits ^ jnp.int32(0x7FFFFFFF), bits)
    return jnp.where(bits == jnp.int32(INT_MIN), 0, key)


def _idx_kernel(qi_ref, kk_ref, wi_ref, mask_ref, q8_ref, w8_ref, keys_ref, gm_ref,
                *, tq, tk, nk, ksel, jbits):
    i = pl.program_id(0)
    nck = ((i + 1) * tq + tk - 1) // tk
    classes = gm_ref.shape[0]
    acc_rows = 32

    lane = lax.broadcasted_iota(I32, (tq, LANES), 1)
    lo_half = jnp.where(lane < IDX_DIM, 1.0, 0.0).astype(F32)
    hi_half = 1.0 - lo_half
    for h in range(IDX_HEADS):
        qp = qi_ref[:, (h // 2) * LANES:(h // 2 + 1) * LANES].astype(F32)
        q8_ref[h * tq:(h + 1) * tq, :] = (qp * (lo_half if h % 2 == 0 else hi_half)).astype(BF16)
    w8_ref[...] = wi_ref[...].T[0:SUBLANES, :]
    gm_ref[...] = jnp.full(gm_ref.shape, NEG_INF, F32)

    q_idx = i * tq + lax.broadcasted_iota(I32, (tk, tq), 1)
    k_off = lax.broadcasted_iota(I32, (tk, tq), 0)

    def score_chunk(c, carry, masked):
        kc = kk_ref[pl.ds(pl.multiple_of(c * tk, tk), tk), :]
        y = lax.dot_general(kc, q8_ref[...], (((1,), (1,)), ((), ())), preferred_element_type=F32)
        sc = jnp.maximum(y[:, 0:tq], 0.0) * w8_ref[0:1, :]
        for h in range(1, IDX_HEADS):
            sc = sc + jnp.maximum(y[:, h * tq:(h + 1) * tq], 0.0) * w8_ref[h:h + 1, :]
        key = _order_key(sc)
        if masked:
            causal = c * tk + k_off <= q_idx
            key = jnp.where(causal, key, jnp.int32(INT_MIN))
            sc = jnp.where(causal, sc, NEG_INF)
        keys_ref[c] = key
        for cls in range(classes):
            g = gm_ref[cls]
            for r in range(cls * SUBLANES, tk, classes * SUBLANES):
                g = jnp.maximum(g, sc[r:r + SUBLANES, :])
            gm_ref[cls] = g
        return carry

    n_full = (i * tq) // tk
    lax.fori_loop(0, n_full, functools.partial(score_chunk, masked=False), 0)
    lax.fori_loop(n_full, nck, functools.partial(score_chunk, masked=True), 0)

    def count(pred):
        def body(c, acc):
            for r in range(0, tk, acc_rows):
                acc = jnp.where(pred(keys_ref[c, r:r + acc_rows, :], c * tk + r), acc + 1.0, acc)
            return acc
        acc = lax.fori_loop(0, nck, body, jnp.zeros((acc_rows, tq), F32))
        return jnp.sum(acc, axis=0, keepdims=True)

    g_min = gm_ref[0]
    g_max = gm_ref[0]
    for cls in range(1, classes):
        g_min = jnp.minimum(g_min, gm_ref[cls])
        g_max = jnp.maximum(g_max, gm_ref[cls])
    lo_f = jnp.min(g_min, axis=0, keepdims=True)
    hi_f = jnp.max(g_max, axis=0, keepdims=True)
    n_valid = i * tq + lax.broadcasted_iota(I32, (1, tq), 1) + 1
    few = n_valid <= ksel
    lo0 = jnp.where(jnp.logical_or(few, lo_f == NEG_INF), jnp.int32(INT_MIN), _order_key(lo_f))
    hi0 = jnp.where(few, jnp.int32(INT_MIN), _order_key(hi_f))

    def unresolved(st):
        lo, hi = st
        return jnp.max(jnp.where(lo < hi, 1.0, 0.0)) > 0.0

    def bisect(st):
        lo, hi = st
        mid = (lo >> 1) + (hi >> 1) + ((lo | hi) & 1)
        cnt = count(lambda k, r0: k >= mid)
        ge = cnt >= ksel
        exact = cnt == ksel
        return (jnp.where(ge, mid, lo), jnp.where(exact, mid, jnp.where(ge, hi, mid - 1)))

    thr, _ = lax.while_loop(unresolved, bisect, (lo0, hi0))
    n_gt = count(lambda k, r0: k > thr)
    n_ge = count(lambda k, r0: k >= thr)
    need = ksel - n_gt
    excess = jnp.logical_and(thr > jnp.int32(INT_MIN), n_ge > ksel)
    any_excess = jnp.max(jnp.where(excess, 1.0, 0.0)) > 0.0
    sub = lax.broadcasted_iota(I32, (acc_rows, tq), 0)

    def tie_search():
        def jbody(b, jc):
            cand = jc + lax.shift_left(jnp.int32(1), jbits - 1 - b)
            cnt = count(lambda k, r0: jnp.logical_and(k == thr, r0 + sub < cand))
            return jnp.where(cnt <= need, cand, jc)
        return lax.fori_loop(0, jbits, jbody, jnp.zeros((1, tq), I32))

    def write_mask(select):
        for c in range(nk):
            @pl.when(c < nck)
            def _(c=c):
                sel = jnp.where(select(keys_ref[c], c), 1.0, 0.0).astype(F32)
                mask_ref[:, c * tk:(c + 1) * tk] = sel.T.astype(BF16)

            @pl.when(c >= nck)
            def _(c=c):
                mask_ref[:, c * tk:(c + 1) * tk] = jnp.zeros((tq, tk), BF16)

    @pl.when(any_excess)
    def _():
        jcut = tie_search()

        def select(key, c):
            tie = jnp.logical_and(key == thr, c * tk + k_off < jcut)
            return jnp.logical_and(key > jnp.int32(INT_MIN), jnp.logical_or(key > thr, tie))
        write_mask(select)

    @pl.when(jnp.logical_not(any_excess))
    def _():
        floor = jnp.maximum(thr, jnp.int32(INT_MIN + 1))
        write_mask(lambda key, c: key >= floor)


def _dsa_mask(pb, pf, ksel, tq, tk):
    s = pb.shape[0]
    nk = s // tk
    jbits = int(s).bit_length()
    classes = min(32, tk // SUBLANES)
    assert ksel <= SUBLANES * classes, "the threshold bracket relies on that many disjoint key groups"
    kern = functools.partial(_idx_kernel, tq=tq, tk=tk, nk=nk, ksel=ksel, jbits=jbits)
    return pl.pallas_call(
        kern,
        grid=(s // tq,),
        in_specs=[pl.BlockSpec((tq, IDX_HEADS * IDX_DIM), lambda i: (i, PB_QI // (IDX_HEADS * IDX_DIM))),
                  pl.BlockSpec((s, LANES), lambda i: (0, PB_KK // LANES)),
                  pl.BlockSpec((tq, LANES), lambda i: (i, PF_WI // LANES))],
        out_specs=pl.BlockSpec((tq, s), lambda i: (i, 0)),
        out_shape=jax.ShapeDtypeStruct((s, s), BF16),
        scratch_shapes=[pltpu.VMEM((IDX_HEADS * tq, LANES), BF16),
                        pltpu.VMEM((SUBLANES, tq), F32),
                        pltpu.VMEM((nk, tk, tq), I32),
                        pltpu.VMEM((classes, SUBLANES, tq), F32)],
        compiler_params=_cparams("arbitrary"),
        name="dsa_index_mask",
    )(pb, pb, pf)


def _causal_steps(n):
    qi = [i for i in range(n) for _ in range(i + 1)]
    kj = [j for i in range(n) for j in range(i + 1)]
    return jnp.asarray(qi, I32), jnp.asarray(kj, I32)


def _bf16_terms(x):
    out = []
    r = np.float32(x)
    for _ in range(3):
        t = np.float32(np.asarray(r, dtype=jnp.bfloat16))
        out.append(float(t))
        r = np.float32(r - t)
    return out


def _alibi_q_ext(slope, rows):
    s1, s2, s3 = _bf16_terms(slope * LOG2E)
    lane = lax.broadcasted_iota(I32, (rows, LANES), 1)
    ext = jnp.where(lane < 3, s1, jnp.where(lane < 6, s2, jnp.where(lane < 9, s3, 0.0)))
    return ext.astype(BF16)


def _alibi_k_ext(first_pos, rows):
    pos = (first_pos + lax.broadcasted_iota(I32, (rows, LANES), 0)).astype(F32)
    p1, p2, p3 = _split3(pos)
    lane = lax.broadcasted_iota(I32, (rows, LANES), 1)
    sel = lane % 3
    ext = jnp.where(sel == 0, p1.astype(F32), jnp.where(sel == 1, p2.astype(F32), p3.astype(F32)))
    return jnp.where(lane < 9, ext, 0.0).astype(BF16)


def _dsa_attn_kernel(qi_ref, kj_ref, q_ref, k_ref, v_ref, mask_ref, o_ref, qa_ref, mb_ref, s_ref, m_ref,
                     acc_ref, *, tq, tk):
    t = pl.program_id(0)
    i = qi_ref[t]
    j = kj_ref[t]
    slopes = _alibi_slopes(A_HEADS)
    reps = tk // LANES

    @pl.when(j == 0)
    def _():
        m_ref[...] = jnp.full(m_ref.shape, NEG_INF, F32)
        acc_ref[...] = jnp.zeros(acc_ref.shape, F32)
        for h in range(A_HEADS):
            hs = slice(h * HEAD_DIM, (h + 1) * HEAD_DIM)
            qh = (q_ref[:, hs].astype(F32) * (HEAD_DIM ** -0.5 * LOG2E)).astype(BF16)
            qa_ref[h] = jnp.concatenate([qh, _alibi_q_ext(slopes[h], tq)], axis=1)

    mf = mask_ref[...].astype(F32)
    mb_ref[...] = jnp.where(mf > 0.0, mf - 1.0, NEG_INF)
    k_ext = _alibi_k_ext(j * tk - i * tq, tk)
    ones = jnp.ones((tk, LANES), BF16)

    def scores(h):
        hs = slice(h * HEAD_DIM, (h + 1) * HEAD_DIM)
        ka = jnp.concatenate([k_ref[:, hs], k_ext], axis=1)
        s = lax.dot_general(qa_ref[h], ka, (((1,), (1,)), ((), ())), preferred_element_type=F32)
        s = s + mb_ref[...]
        s_ref[h % 2] = s
        m_prev = m_ref[h]
        m_cur = jnp.maximum(m_prev, jnp.max(s, axis=1, keepdims=True))
        m_ref[h] = m_cur
        m_safe = jnp.where(m_cur == NEG_INF, 0.0, m_cur)
        return m_safe, jnp.exp2(m_prev - m_safe)

    def accumulate(h, m_safe, a):
        hs = slice(h * HEAD_DIM, (h + 1) * HEAD_DIM)
        p = jnp.exp2(s_ref[h % 2] - jnp.tile(m_safe, (1, reps)))
        va = jnp.concatenate([v_ref[:, hs], ones], axis=1)
        acc_ref[h] = jnp.tile(a, (1, 2)) * acc_ref[h] + jnp.dot(p.astype(BF16), va,
                                                                 preferred_element_type=F32)

    stats = scores(0)
    for h in range(A_HEADS):
        nxt = scores(h + 1) if h + 1 < A_HEADS else None
        accumulate(h, *stats)
        stats = nxt

    @pl.when(j == i)
    def _():
        for h in range(A_HEADS):
            hs = slice(h * HEAD_DIM, (h + 1) * HEAD_DIM)
            o_ref[:, hs] = (acc_ref[h, :, 0:HEAD_DIM] / acc_ref[h, :, HEAD_DIM:2 * HEAD_DIM]).astype(o_ref.dtype)


def _diff_attn_kernel(qi_ref, kj_ref, q_ref, k_ref, v_ref, lam_ref, g_ref, o_ref, qa_ref, s_ref, m_ref,
                      acc_ref, *, tq, tk, lam_init):
    t = pl.program_id(0)
    i = qi_ref[t]
    j = kj_ref[t]
    slopes = _alibi_slopes(C_HEADS)
    reps = tk // LANES

    @pl.when(j == 0)
    def _():
        m_ref[...] = jnp.full(m_ref.shape, NEG_INF, F32)
        acc_ref[...] = jnp.zeros(acc_ref.shape, F32)
        lane = lax.broadcasted_iota(I32, (tq, LANES), 1)
        for h in range(C_HEADS):
            qf = q_ref[:, h * LANES:(h + 1) * LANES].astype(F32) * (C_QK ** -0.5 * LOG2E)
            ext = _alibi_q_ext(slopes[h], tq)
            qa_ref[2 * h] = jnp.concatenate([jnp.where(lane < C_QK, qf, 0.0).astype(BF16), ext], axis=1)
            qa_ref[2 * h + 1] = jnp.concatenate([jnp.where(lane < C_QK, 0.0, qf).astype(BF16), ext], axis=1)

    def step(diagonal):
        k_ext = _alibi_k_ext(j * tk - i * tq, tk)
        ones = jnp.ones((tk, LANES), BF16)
        if diagonal:
            causal = lax.broadcasted_iota(I32, (tq, tk), 1) <= lax.broadcasted_iota(I32, (tq, tk), 0)

        def scores(u):
            h = u // 2
            ka = jnp.concatenate([k_ref[:, h * LANES:(h + 1) * LANES], k_ext], axis=1)
            s = lax.dot_general(qa_ref[u], ka, (((1,), (1,)), ((), ())), preferred_element_type=F32)
            if diagonal:
                s = jnp.where(causal, s, NEG_INF)
            s_ref[u % 2] = s
            m_prev = m_ref[u]
            m_cur = jnp.maximum(m_prev, jnp.max(s, axis=1, keepdims=True))
            m_ref[u] = m_cur
            return m_cur, jnp.exp2(m_prev - m_cur)

        def accumulate(u, m_cur, a):
            h = u // 2
            p = jnp.exp2(s_ref[u % 2] - jnp.tile(m_cur, (1, reps)))
            va = jnp.concatenate([v_ref[:, h * LANES:(h + 1) * LANES], ones], axis=1)
            acc_ref[u] = jnp.tile(a, (1, 2)) * acc_ref[u] + jnp.dot(p.astype(BF16), va,
                                                                     preferred_element_type=F32)

        stats = scores(0)
        for u in range(2 * C_HEADS):
            nxt = scores(u + 1) if u + 1 < 2 * C_HEADS else None
            accumulate(u, *stats)
            stats = nxt

    @pl.when(j < i)
    def _():
        step(False)

    @pl.when(j == i)
    def _():
        step(True)
        lp = lam_ref[...]
        lam = (jnp.exp(jnp.sum(lp[0:1] * lp[1:2], axis=1, keepdims=True))
               - jnp.exp(jnp.sum(lp[2:3] * lp[3:4], axis=1, keepdims=True)) + lam_init)
        for h in range(C_HEADS):
            hs = slice(h * LANES, (h + 1) * LANES)
            o = (acc_ref[2 * h, :, 0:LANES] / acc_ref[2 * h, :, LANES:2 * LANES]
                 - lam * (acc_ref[2 * h + 1, :, 0:LANES] / acc_ref[2 * h + 1, :, LANES:2 * LANES]))
            ms = jnp.mean(o * o, axis=1, keepdims=True)
            o_ref[:, hs] = (o * lax.rsqrt(ms + EPS) * g_ref[:, hs] * (1.0 - lam_init)).astype(o_ref.dtype)


_N_DSA_SCRATCH = 5


def _attn_pair_kernel(qi_ref, kj_ref, qa_ref, ka_ref, va_ref, mask_ref, qc_ref, kc_ref, vc_ref, lam_ref, g_ref,
                      oa_ref, oc_ref, *scratch, tq, tk, lam_init):
    _dsa_attn_kernel(qi_ref, kj_ref, qa_ref, ka_ref, va_ref, mask_ref, oa_ref, *scratch[:_N_DSA_SCRATCH],
                     tq=tq, tk=tk)
    _diff_attn_kernel(qi_ref, kj_ref, qc_ref, kc_ref, vc_ref, lam_ref, g_ref, oc_ref,
                      *scratch[_N_DSA_SCRATCH:], tq=tq, tk=tk, lam_init=lam_init)


def _attention_pair(pb, mask, lam_params, g, lam_init, tq):
    s = pb.shape[0]
    kern = functools.partial(_attn_pair_kernel, tq=tq, tk=tq, lam_init=lam_init)
    qi, kj = _causal_steps(s // tq)
    q_blk = lambda width, col: pl.BlockSpec((tq, width), lambda t, qi, kj: (qi[t], col // width))
    kv_blk = lambda width, col: pl.BlockSpec((tq, width), lambda t, qi, kj: (kj[t], col // width))
    grid_spec = pltpu.PrefetchScalarGridSpec(
        num_scalar_prefetch=2,
        grid=(int(qi.shape[0]),),
        in_specs=[q_blk(A_WIDTH, PB_QA), kv_blk(A_WIDTH, PB_KA), kv_blk(A_WIDTH, PB_VA),
                  pl.BlockSpec((tq, tq), lambda t, qi, kj: (qi[t], kj[t])),
                  q_blk(C_WIDTH, PB_QC), kv_blk(C_WIDTH, PB_KC), kv_blk(C_WIDTH, PB_VC),
                  pl.BlockSpec((4, C_QK), lambda t, qi, kj: (0, 0)),
                  pl.BlockSpec((1, C_WIDTH), lambda t, qi, kj: (0, 0))],
        out_specs=[pl.BlockSpec((tq, A_WIDTH), lambda t, qi, kj: (qi[t], 0)),
                   pl.BlockSpec((tq, C_WIDTH), lambda t, qi, kj: (qi[t], 0))],
        scratch_shapes=[pltpu.VMEM((A_HEADS, tq, 2 * HEAD_DIM), BF16),
                        pltpu.VMEM((tq, tq), F32),
                        pltpu.VMEM((2, tq, tq), F32),
                        pltpu.VMEM((A_HEADS, tq, LANES), F32),
                        pltpu.VMEM((A_HEADS, tq, 2 * HEAD_DIM), F32),
                        pltpu.VMEM((2 * C_HEADS, tq, 2 * LANES), BF16),
                        pltpu.VMEM((2, tq, tq), F32),
                        pltpu.VMEM((2 * C_HEADS, tq, LANES), F32),
                        pltpu.VMEM((2 * C_HEADS, tq, 2 * LANES), F32)])
    return pl.pallas_call(
        kern,
        grid_spec=grid_spec,
        out_shape=[jax.ShapeDtypeStruct((s, A_WIDTH), BF16), jax.ShapeDtypeStruct((s, C_WIDTH), BF16)],
        compiler_params=_cparams("arbitrary"),
        name="attention_pair",
    )(qi, kj, pb, pb, pb, mask, pb, pb, pb, lam_params, g)


def _split3(x):
    x1 = x.astype(BF16)
    r1 = x - x1.astype(F32)
    x2 = r1.astype(BF16)
    x3 = (r1 - x2.astype(F32)).astype(BF16)
    return x1, x2, x3


def _mlstm_kernel(q_ref, k_ref, qh_ref, kh_ref, v_ref, o_ref, gi_ref, gf_ref, cw_ref, bi_ref, bf_ref,
                  g_ref, y_ref, xq_ref, xk_ref, c_ref, n_ref, m_ref, *, L):
    c = pl.program_id(0)

    @pl.when(c == 0)
    def _():
        c_ref[...] = jnp.zeros(c_ref.shape, F32)
        n_ref[...] = jnp.zeros(n_ref.shape, F32)
        m_ref[...] = jnp.zeros(m_ref.shape, F32)

    first = (c > 0).astype(F32)
    xq_ref[0:SUBLANES, :] = qh_ref[...] * first
    xq_ref[SUBLANES:SUBLANES + L, :] = q_ref[...]
    xk_ref[0:SUBLANES, :] = kh_ref[...] * first
    xk_ref[SUBLANES:SUBLANES + L, :] = k_ref[...]
    qc = jnp.zeros((L, M_WIDTH), F32)
    kc = jnp.zeros((L, M_WIDTH), F32)
    for t in range(CONV_W):
        off = SUBLANES - (CONV_W - 1) + t
        qc = qc + xq_ref[off:off + L, :] * cw_ref[t:t + 1, 0:M_WIDTH]
        kc = kc + xk_ref[off:off + L, :] * cw_ref[t:t + 1, M_WIDTH:2 * M_WIDTH]
    qc = qc * jax.nn.sigmoid(qc)
    kc = kc * jax.nn.sigmoid(kc) * (HEAD_DIM ** -0.5)

    li = gi_ref[...] + bi_ref[...]
    fp = gf_ref[...] + bf_ref[...]
    lf = jnp.minimum(fp, 0.0) - jnp.log(1.0 + jnp.exp(-jnp.abs(fp)))
    r_i = lax.broadcasted_iota(I32, (L, L), 0)
    c_i = lax.broadcasted_iota(I32, (L, L), 1)
    tril = c_i <= r_i
    tril_b = jnp.where(tril, 1.0, 0.0).astype(BF16)
    a = jnp.zeros((L, LANES), F32)
    for part in _split3(lf):
        a = a + jnp.dot(tril_b, part, preferred_element_type=F32)
    b = li - a
    g_tot = a[L - 1:L, :]
    b_t = b.T

    m_all = m_ref[...]
    m_new_all = m_all
    for h in range(M_HEADS):
        hs = slice(h * HEAD_DIM, (h + 1) * HEAD_DIM)
        qh = qc[:, hs]
        kh = kc[:, hs]
        vh = v_ref[:, hs]
        qb = qh.astype(BF16)
        kb = kh.astype(BF16)
        m_prev = m_all[:, h:h + 1]
        d = jnp.where(tril, b_t[h:h + 1, :], NEG_INF)
        mm = jnp.maximum(m_prev, jnp.max(d, axis=1, keepdims=True))
        w_intra = jnp.exp(d - mm)
        w_inter = jnp.exp(m_prev - mm)
        qk = lax.dot_general(qb, kb, (((1,), (1,)), ((), ())), preferred_element_type=F32) * w_intra
        c_h = c_ref[h]
        num = (jnp.dot(qk.astype(BF16), vh, preferred_element_type=F32)
               + w_inter * jnp.dot(qb, c_h.astype(BF16), preferred_element_type=F32))
        den = (jnp.sum(qk, axis=1, keepdims=True)
               + w_inter * jnp.sum(qh * n_ref[h:h + 1, :], axis=1, keepdims=True))
        m_row = a[:, h:h + 1] + mm
        hh = num / jnp.maximum(jnp.abs(den), jnp.exp(-m_row))
        mm_last = mm[L - 1:L, :]
        ws = jnp.exp(b[:, h:h + 1] - mm_last)
        wc = jnp.exp(m_prev - mm_last)
        c_ref[h] = wc * c_h + lax.dot_general(kb, (ws * vh.astype(F32)).astype(BF16),
                                              (((0,), (0,)), ((), ())), preferred_element_type=F32)
        n_ref[h:h + 1, :] = wc * n_ref[h:h + 1, :] + jnp.sum(ws * kh, axis=0, keepdims=True)
        lane = lax.broadcasted_iota(I32, (1, LANES), 1)
        m_new_all = jnp.where(lane == h, g_tot + mm_last, m_new_all)
        ms = jnp.mean(hh * hh, axis=1, keepdims=True)
        hn = hh * lax.rsqrt(ms + EPS) * g_ref[:, hs]
        y_ref[:, hs] = (jax.nn.sigmoid(o_ref[:, hs]) * hn).astype(y_ref.dtype)
    m_ref[...] = m_new_all


def _mlstm(pb, pf, conv_w, b_i, b_f, g, L):
    s = pb.shape[0]
    kern = functools.partial(_mlstm_kernel, L=L)
    halo = lambda c: jnp.maximum(c * (L // SUBLANES) - 1, 0)
    return pl.pallas_call(
        kern,
        grid=(s // L,),
        in_specs=[pl.BlockSpec((L, M_WIDTH), lambda c: (c, PF_QM // M_WIDTH)),
                  pl.BlockSpec((L, M_WIDTH), lambda c: (c, PF_KM // M_WIDTH)),
                  pl.BlockSpec((SUBLANES, M_WIDTH), lambda c: (halo(c), PF_QM // M_WIDTH)),
                  pl.BlockSpec((SUBLANES, M_WIDTH), lambda c: (halo(c), PF_KM // M_WIDTH)),
                  pl.BlockSpec((L, M_WIDTH), lambda c: (c, PB_VM // M_WIDTH)),
                  pl.BlockSpec((L, M_WIDTH), lambda c: (c, PF_OM // M_WIDTH)),
                  pl.BlockSpec((L, LANES), lambda c: (c, PF_GI // LANES)),
                  pl.BlockSpec((L, LANES), lambda c: (c, PF_GF // LANES)),
                  pl.BlockSpec((CONV_W, 2 * M_WIDTH), lambda c: (0, 0)),
                  pl.BlockSpec((1, LANES), lambda c: (0, 0)),
                  pl.BlockSpec((1, LANES), lambda c: (0, 0)),
                  pl.BlockSpec((1, M_WIDTH), lambda c: (0, 0))],
        out_specs=pl.BlockSpec((L, M_WIDTH), lambda c: (c, 0)),
        out_shape=jax.ShapeDtypeStruct((s, M_WIDTH), BF16),
        scratch_shapes=[pltpu.VMEM((L + SUBLANES, M_WIDTH), F32),
                        pltpu.VMEM((L + SUBLANES, M_WIDTH), F32),
                        pltpu.VMEM((M_HEADS, HEAD_DIM, HEAD_DIM), F32),
                        pltpu.VMEM((SUBLANES, HEAD_DIM), F32),
                        pltpu.VMEM((1, LANES), F32)],
        compiler_params=_cparams("arbitrary"),
        name="mlstm",
    )(pf, pf, pf, pf, pb, pf, pf, pf, conv_w, b_i, b_f, g)


def _layer_norm(z, g, b):
    mu = jnp.mean(z, axis=-1, keepdims=True)
    zc = z - mu
    var = jnp.mean(zc * zc, axis=-1, keepdims=True)
    return zc * lax.rsqrt(var + EPS) * g + b


def _out_kernel(ya_ref, ym_ref, yc_ref, w_ref, x_ref, g_ref, b_ref, o_ref):
    acc = jnp.dot(ya_ref[...], w_ref[0:A_WIDTH, :], preferred_element_type=F32)
    acc = acc + jnp.dot(ym_ref[...], w_ref[A_WIDTH:A_WIDTH + M_WIDTH, :], preferred_element_type=F32)
    acc = acc + jnp.dot(yc_ref[...], w_ref[A_WIDTH + M_WIDTH:D_MODEL, :], preferred_element_type=F32)
    o_ref[...] = _layer_norm(ALPHA * x_ref[...] + acc, g_ref[...], b_ref[...])


def _out_proj(ya, ym, yc, w, layer, x, g, b, tm):
    s = x.shape[0]
    return pl.pallas_call(
        _out_kernel,
        grid=(s // tm,),
        in_specs=[pl.BlockSpec((tm, A_WIDTH), lambda i: (i, 0)),
                  pl.BlockSpec((tm, M_WIDTH), lambda i: (i, 0)),
                  pl.BlockSpec((tm, C_WIDTH), lambda i: (i, 0)),
                  pl.BlockSpec((None, D_MODEL, D_MODEL), lambda i: (layer, 0, 0)),
                  pl.BlockSpec((tm, D_MODEL), lambda i: (i, 0)),
                  pl.BlockSpec((1, D_MODEL), lambda i: (0, 0)),
                  pl.BlockSpec((1, D_MODEL), lambda i: (0, 0))],
        out_specs=pl.BlockSpec((tm, D_MODEL), lambda i: (i, 0)),
        out_shape=jax.ShapeDtypeStruct((s, D_MODEL), F32),
        compiler_params=_cparams("arbitrary"),
        name="out_proj_ln",
    )(ya, ym, yc, w, x, g, b)


def _ffn_kernel(x_ref, wu_ref, wd_ref, g_ref, b_ref, o_ref, xb_ref, acc_ref):
    f = pl.program_id(1)

    @pl.when(f == 0)
    def _():
        xb_ref[...] = x_ref[...].astype(BF16)
        acc_ref[...] = jnp.zeros(acc_ref.shape, F32)

    hdn = jnp.maximum(jnp.dot(xb_ref[...], wu_ref[...], preferred_element_type=F32), 0.0)
    acc_ref[...] += jnp.dot((hdn * hdn).astype(BF16), wd_ref[...], preferred_element_type=F32)

    @pl.when(f == pl.num_programs(1) - 1)
    def _():
        o_ref[...] = _layer_norm(ALPHA * x_ref[...] + acc_ref[...], g_ref[...], b_ref[...])


def _ffn(x, wu, wd, layer, g, b, tm, tf):
    s = x.shape[0]
    return pl.pallas_call(
        _ffn_kernel,
        grid=(s // tm, D_FF // tf),
        in_specs=[pl.BlockSpec((tm, D_MODEL), lambda i, f: (i, 0)),
                  pl.BlockSpec((None, D_MODEL, tf), lambda i, f: (layer, 0, f)),
                  pl.BlockSpec((None, tf, D_MODEL), lambda i, f: (layer, f, 0)),
                  pl.BlockSpec((1, D_MODEL), lambda i, f: (0, 0)),
                  pl.BlockSpec((1, D_MODEL), lambda i, f: (0, 0))],
        out_specs=pl.BlockSpec((tm, D_MODEL), lambda i, f: (i, 0)),
        out_shape=jax.ShapeDtypeStruct((s, D_MODEL), F32),
        scratch_shapes=[pltpu.VMEM((tm, D_MODEL), BF16),
                        pltpu.VMEM((tm, D_MODEL), F32)],
        compiler_params=_cparams("arbitrary", "arbitrary"),
        name="ffn_ln",
    )(x, wu, wd, g, b)


def _seg(w, i):
    return w[..., _OFFS[i]:_OFFS[i] + SIZES[i]]


def _pad_cols(w, width):
    return jnp.pad(w, [(0, 0)] * (w.ndim - 1) + [(0, width - w.shape[-1])])


def _layout_w_in(w_in):
    w_in = w_in.astype(BF16)
    (q_a, k_a, v_a, q_i, k_i, w_i, q_m, k_m, v_m, o_m, i_m, f_m, q_c, k_c, v_c) = [
        _seg(w_in, i) for i in range(len(SIZES))]
    wb = jnp.concatenate([q_a, k_a, v_a, v_m, q_c, k_c, v_c, q_i, k_i, k_i], axis=-1)
    wf = jnp.concatenate([q_m, k_m, o_m, _pad_cols(w_i, LANES), _pad_cols(i_m, LANES),
                          _pad_cols(f_m, LANES)], axis=-1)
    return _pad_cols(wb, PB_WIDTH), _pad_cols(wf, PF_WIDTH)


def _tile(n, pref):
    t = min(n, pref)
    assert n % t == 0, (n, t)
    return t


def kernel(x, w_in, conv_m, b_i, b_f, m_norm_g, lam_q1, lam_k1, lam_q2, lam_k2, c_norm_g, w_out,
           ln1_g, ln1_b, w_up, w_down, ln2_g, ln2_b):
    batch, s, d = x.shape
    assert batch == 1 and d == D_MODEL
    ksel = min(TOPK_MAX, s // 4)
    wb_all, wf_all = _layout_w_in(w_in)
    w_out_b = w_out.astype(BF16)
    w_up_b = w_up.astype(BF16)
    w_down_b = w_down.astype(BF16)
    conv_w = conv_m.reshape(DEPTH, CONV_W, 2 * M_WIDTH)
    b_i_p = _pad_cols(b_i, LANES).reshape(DEPTH, 1, LANES)
    b_f_p = _pad_cols(b_f, LANES).reshape(DEPTH, 1, LANES)
    lam_p = jnp.stack([lam_q1, lam_k1, lam_q2, lam_k2], axis=1)

    tm_proj = _tile(s, 1024)
    t_attn = _tile(s, 512)
    tq_idx = _tile(s, 256)
    tk_idx = _tile(s, 512)
    l_chunk = _tile(s, 256)
    tm_out = _tile(s, 512)
    tm_ffn = _tile(s, 512)

    h = x.reshape(s, d)
    for l in range(DEPTH):
        pb = _matmul(h, wb_all, l, BF16, tm_proj, PB_WIDTH // 3, "proj_bf16")
        pf = _matmul(h, wf_all, l, F32, tm_proj, PF_WIDTH // 2, "proj_f32")
        mask = _dsa_mask(pb, pf, ksel, tq_idx, tk_idx)
        lam_init = 0.8 - 0.6 * math.exp(-0.3 * l)
        y_a, y_c = _attention_pair(pb, mask, lam_p[l], c_norm_g[l].reshape(1, C_WIDTH), lam_init, t_attn)
        y_m = _mlstm(pb, pf, conv_w[l], b_i_p[l], b_f_p[l], m_norm_g[l].reshape(1, M_WIDTH), l_chunk)
        h = _out_proj(y_a, y_m, y_c, w_out_b, l, h, ln1_g[l].reshape(1, d), ln1_b[l].reshape(1, d), tm_out)
        h = _ffn(h, w_up_b, w_down_b, l, ln2_g[l].reshape(1, d), ln2_b[l].reshape(1, d), tm_ffn, 1024)
    return h.reshape(batch, s, d)
```

```python
import functools
import math

import jax
import jax.numpy as jnp
import numpy as np
from jax import lax
from jax.experimental import pallas as pl
from jax.experimental.pallas import tpu as pltpu

F32 = jnp.float32
BF16 = jnp.bfloat16
I32 = jnp.int32

D_MODEL = 2048
DEPTH = 4
HEAD_DIM = 128
A_HEADS = 6
A_WIDTH = A_HEADS * HEAD_DIM
IDX_HEADS = 8
IDX_DIM = 64
TOPK_MAX = 256
M_HEADS = 6
M_WIDTH = M_HEADS * HEAD_DIM
CONV_W = 4
C_HEADS = 4
C_QK = 64
C_WIDTH = C_HEADS * 2 * C_QK
D_FF = 4 * D_MODEL
ALPHA = (2.0 * DEPTH) ** 0.25
EPS = 1e-5

SIZES = (A_WIDTH, A_WIDTH, A_WIDTH, IDX_HEADS * IDX_DIM, IDX_DIM, IDX_HEADS,
         M_WIDTH, M_WIDTH, M_WIDTH, M_WIDTH, M_HEADS, M_HEADS,
         C_WIDTH, C_WIDTH, C_WIDTH)
_OFFS = tuple(int(sum(SIZES[:i])) for i in range(len(SIZES)))

LANES = 128
SUBLANES = 8
VMEM_LIMIT_BYTES = 56 * 1024 * 1024
INT_MIN = -2 ** 31
NEG_INF = float("-inf")
LOG2E = math.log2(math.e)

PB_QA, PB_KA, PB_VA, PB_VM = 0, 768, 1536, 2304
PB_QC, PB_KC, PB_VC, PB_QI = 3072, 3584, 4096, 4608
PB_KK = 5120
PB_WIDTH = 5376
PF_QM, PF_KM, PF_OM = 0, 768, 1536
PF_WI, PF_GI, PF_GF = 2304, 2432, 2560
PF_WIDTH = 3072


def _cparams(*sem):
    return pltpu.CompilerParams(dimension_semantics=sem, vmem_limit_bytes=VMEM_LIMIT_BYTES)


def _alibi_slopes(n):
    return [2.0 ** (-8.0 * (h + 1) / n) for h in range(n)]


def _mm_kernel(x_ref, w_ref, o_ref, xb_ref):
    @pl.when(pl.program_id(1) == 0)
    def _():
        xb_ref[...] = x_ref[...].astype(BF16)

    o_ref[...] = jnp.dot(xb_ref[...], w_ref[...], preferred_element_type=F32).astype(o_ref.dtype)


def _matmul(x, w, layer, out_dtype, tm, tn, name):
    m, k = x.shape
    n = w.shape[2]
    return pl.pallas_call(
        _mm_kernel,
        grid=(m // tm, n // tn),
        in_specs=[pl.BlockSpec((tm, k), lambda i, j: (i, 0)),
                  pl.BlockSpec((None, k, tn), lambda i, j: (layer, 0, j))],
        out_specs=pl.BlockSpec((tm, tn), lambda i, j: (i, j)),
        out_shape=jax.ShapeDtypeStruct((m, n), out_dtype),
        scratch_shapes=[pltpu.VMEM((tm, k), BF16)],
        compiler_params=_cparams("arbitrary", "arbitrary"),
        name=name,
    )(x, w)


def _order_key(x):
    bits = lax.bitcast_convert_type(x, I32)
    key = jnp.where(bits < 0, bits ^ jnp.int32(0x7FFFFFFF), bits)
    return jnp.where(bits == jnp.int32(INT_MIN), 0, key)


def _idx_kernel(qi_ref, kk_ref, wi_ref, mask_ref, q8_ref, w8_ref, keys_ref, gm_ref,
                *, tq, tk, nk, ksel, jbits):
    i = pl.program_id(0)
    nck = ((i + 1) * tq + tk - 1) // tk
    classes = gm_ref.shape[0]
    acc_rows = 32

    lane = lax.broadcasted_iota(I32, (tq, LANES), 1)
    lo_half = jnp.where(lane < IDX_DIM, 1.0, 0.0).astype(F32)
    hi_half = 1.0 - lo_half
    for h in range(IDX_HEADS):
        qp = qi_ref[:, (h // 2) * LANES:(h // 2 + 1) * LANES].astype(F32)
        q8_ref[h * tq:(h + 1) * tq, :] = (qp * (lo_half if h % 2 == 0 else hi_half)).astype(BF16)
    w8_ref[...] = wi_ref[...].T[0:SUBLANES, :]
    gm_ref[...] = jnp.full(gm_ref.shape, NEG_INF, F32)

    q_idx = i * tq + lax.broadcasted_iota(I32, (tk, tq), 1)
    k_off = lax.broadcasted_iota(I32, (tk, tq), 0)

    def score_chunk(c, carry, masked):
        kc = kk_ref[pl.ds(pl.multiple_of(c * tk, tk), tk), :]
        y = lax.dot_general(kc, q8_ref[...], (((1,), (1,)), ((), ())), preferred_element_type=F32)
        sc = jnp.maximum(y[:, 0:tq], 0.0) * w8_ref[0:1, :]
        for h in range(1, IDX_HEADS):
            sc = sc + jnp.maximum(y[:, h * tq:(h + 1) * tq], 0.0) * w8_ref[h:h + 1, :]
        key = _order_key(sc)
        if masked:
            causal = c * tk + k_off <= q_idx
            key = jnp.where(causal, key, jnp.int32(INT_MIN))
            sc = jnp.where(causal, sc, NEG_INF)
        keys_ref[c] = key
        for cls in range(classes):
            g = gm_ref[cls]
            for r in range(cls * SUBLANES, tk, classes * SUBLANES):
                g = jnp.maximum(g, sc[r:r + SUBLANES, :])
            gm_ref[cls] = g
        return carry

    n_full = (i * tq) // tk
    lax.fori_loop(0, n_full, functools.partial(score_chunk, masked=False), 0)
    lax.fori_loop(n_full, nck, functools.partial(score_chunk, masked=True), 0)

    def count(pred):
        def body(c, acc):
            for r in range(0, tk, acc_rows):
                acc = jnp.where(pred(keys_ref[c, r:r + acc_rows, :], c * tk + r), acc + 1.0, acc)
            return acc
        acc = lax.fori_loop(0, nck, body, jnp.zeros((acc_rows, tq), F32))
        return jnp.sum(acc, axis=0, keepdims=True)

    g_min = gm_ref[0]
    g_max = gm_ref[0]
    for cls in range(1, classes):
        g_min = jnp.minimum(g_min, gm_ref[cls])
        g_max = jnp.maximum(g_max, gm_ref[cls])
    lo_f = jnp.min(g_min, axis=0, keepdims=True)
    hi_f = jnp.max(g_max, axis=0, keepdims=True)
    n_valid = i * tq + lax.broadcasted_iota(I32, (1, tq), 1) + 1
    few = n_valid <= ksel
    lo0 = jnp.where(jnp.logical_or(few, lo_f == NEG_INF), jnp.int32(INT_MIN), _order_key(lo_f))
    hi0 = jnp.where(few, jnp.int32(INT_MIN), _order_key(hi_f))

    def unresolved(st):
        lo, hi, _ = st
        return jnp.max(jnp.where(lo < hi, 1.0, 0.0)) > 0.0

    def bisect(st):
        lo, hi, n_lo = st
        mid = (lo >> 1) + (hi >> 1) + ((lo | hi) & 1)
        cnt = count(lambda k, r0: k >= mid)
        ge = cnt >= ksel
        exact = cnt == ksel
        return (jnp.where(ge, mid, lo), jnp.where(exact, mid, jnp.where(ge, hi, mid - 1)),
                jnp.where(ge, cnt, n_lo))

    unmeasured = jnp.full((1, tq), float(2 ** 30), F32)
    thr, _, n_ge = lax.while_loop(unresolved, bisect, (lo0, hi0, unmeasured))
    excess = jnp.logical_and(thr > jnp.int32(INT_MIN), n_ge > ksel)
    any_excess = jnp.max(jnp.where(excess, 1.0, 0.0)) > 0.0
    sub = lax.broadcasted_iota(I32, (acc_rows, tq), 0)

    def tie_search():
        need = ksel - count(lambda k, r0: k > thr)

        def jbody(b, jc):
            cand = jc + lax.shift_left(jnp.int32(1), jbits - 1 - b)
            cnt = count(lambda k, r0: jnp.logical_and(k == thr, r0 + sub < cand))
            return jnp.where(cnt <= need, cand, jc)
        return lax.fori_loop(0, jbits, jbody, jnp.zeros((1, tq), I32))

    def write_mask(select):
        for c in range(nk):
            @pl.when(c < nck)
            def _(c=c):
                sel = jnp.where(select(keys_ref[c], c), 1.0, 0.0).astype(F32)
                mask_ref[:, c * tk:(c + 1) * tk] = sel.T.astype(BF16)

            @pl.when(c >= nck)
            def _(c=c):
                mask_ref[:, c * tk:(c + 1) * tk] = jnp.zeros((tq, tk), BF16)

    @pl.when(any_excess)
    def _():
        jcut = tie_search()

        def select(key, c):
            tie = jnp.logical_and(key == thr, c * tk + k_off < jcut)
            return jnp.logical_and(key > jnp.int32(INT_MIN), jnp.logical_or(key > thr, tie))
        write_mask(select)

    @pl.when(jnp.logical_not(any_excess))
    def _():
        floor = jnp.maximum(thr, jnp.int32(INT_MIN + 1))
        write_mask(lambda key, c: key >= floor)


def _dsa_mask(pb, pf, ksel, tq, tk):
    s = pb.shape[0]
    nk = s // tk
    jbits = int(s).bit_length()
    classes = min(32, tk // SUBLANES)
    assert ksel <= SUBLANES * classes, "the threshold bracket relies on that many disjoint key groups"
    kern = functools.partial(_idx_kernel, tq=tq, tk=tk, nk=nk, ksel=ksel, jbits=jbits)
    return pl.pallas_call(
        kern,
        grid=(s // tq,),
        in_specs=[pl.BlockSpec((tq, IDX_HEADS * IDX_DIM), lambda i: (i, PB_QI // (IDX_HEADS * IDX_DIM))),
                  pl.BlockSpec((s, LANES), lambda i: (0, PB_KK // LANES)),
                  pl.BlockSpec((tq, LANES), lambda i: (i, PF_WI // LANES))],
        out_specs=pl.BlockSpec((tq, s), lambda i: (i, 0)),
        out_shape=jax.ShapeDtypeStruct((s, s), BF16),
        scratch_shapes=[pltpu.VMEM((IDX_HEADS * tq, LANES), BF16),
                        pltpu.VMEM((SUBLANES, tq), F32),
                        pltpu.VMEM((nk, tk, tq), I32),
                        pltpu.VMEM((classes, SUBLANES, tq), F32)],
        compiler_params=_cparams("arbitrary"),
        name="dsa_index_mask",
    )(pb, pb, pf)


def _causal_steps(n):
    qi = [i for i in range(n) for _ in range(i + 1)]
    kj = [j for i in range(n) for j in range(i + 1)]
    return jnp.asarray(qi, I32), jnp.asarray(kj, I32)


def _bf16_terms(x):
    out = []
    r = np.float32(x)
    for _ in range(3):
        t = np.float32(np.asarray(r, dtype=jnp.bfloat16))
        out.append(float(t))
        r = np.float32(r - t)
    return out


def _alibi_q_ext(slope, rows):
    s1, s2, s3 = _bf16_terms(slope * LOG2E)
    lane = lax.broadcasted_iota(I32, (rows, LANES), 1)
    ext = jnp.where(lane < 3, s1, jnp.where(lane < 6, s2, jnp.where(lane < 9, s3, 0.0)))
    return ext.astype(BF16)


def _alibi_k_ext(first_pos, rows):
    pos = (first_pos + lax.broadcasted_iota(I32, (rows, LANES), 0)).astype(F32)
    p1, p2, p3 = _split3(pos)
    lane = lax.broadcasted_iota(I32, (rows, LANES), 1)
    sel = lane % 3
    ext = jnp.where(sel == 0, p1.astype(F32), jnp.where(sel == 1, p2.astype(F32), p3.astype(F32)))
    return jnp.where(lane < 9, ext, 0.0).astype(BF16)


def _dsa_attn_kernel(qi_ref, kj_ref, q_ref, k_ref, v_ref, mask_ref, o_ref, qa_ref, mb_ref, s_ref, m_ref,
                     acc_ref, *, tq, tk):
    t = pl.program_id(0)
    i = qi_ref[t]
    j = kj_ref[t]
    slopes = _alibi_slopes(A_HEADS)
    reps = tk // LANES

    @pl.when(j == 0)
    def _():
        m_ref[...] = jnp.full(m_ref.shape, NEG_INF, F32)
        acc_ref[...] = jnp.zeros(acc_ref.shape, F32)
        for h in range(A_HEADS):
            hs = slice(h * HEAD_DIM, (h + 1) * HEAD_DIM)
            qh = (q_ref[:, hs].astype(F32) * (HEAD_DIM ** -0.5 * LOG2E)).astype(BF16)
            qa_ref[h] = jnp.concatenate([qh, _alibi_q_ext(slopes[h], tq)], axis=1)

    mf = mask_ref[...].astype(F32)
    mb_ref[...] = jnp.where(mf > 0.0, mf - 1.0, NEG_INF)
    k_ext = _alibi_k_ext(j * tk - i * tq, tk)
    ones = jnp.ones((tk, LANES), BF16)

    def scores(h):
        hs = slice(h * HEAD_DIM, (h + 1) * HEAD_DIM)
        ka = jnp.concatenate([k_ref[:, hs], k_ext], axis=1)
        s = lax.dot_general(qa_ref[h], ka, (((1,), (1,)), ((), ())), preferred_element_type=F32)
        s = s + mb_ref[...]
        s_ref[h % 2] = s
        m_prev = m_ref[h]
        m_cur = jnp.maximum(m_prev, jnp.max(s, axis=1, keepdims=True))
        m_ref[h] = m_cur
        m_safe = jnp.where(m_cur == NEG_INF, 0.0, m_cur)
        return m_safe, jnp.exp2(m_prev - m_safe)

    def accumulate(h, m_safe, a):
        hs = slice(h * HEAD_DIM, (h + 1) * HEAD_DIM)
        p = jnp.exp2(s_ref[h % 2] - jnp.tile(m_safe, (1, reps)))
        va = jnp.concatenate([v_ref[:, hs], ones], axis=1)
        acc_ref[h] = jnp.tile(a, (1, 2)) * acc_ref[h] + jnp.dot(p.astype(BF16), va,
                                                                 preferred_element_type=F32)

    stats = scores(0)
    for h in range(A_HEADS):
        nxt = scores(h + 1) if h + 1 < A_HEADS else None
        accumulate(h, *stats)
        stats = nxt

    @pl.when(j == i)
    def _():
        for h in range(A_HEADS):
            hs = slice(h * HEAD_DIM, (h + 1) * HEAD_DIM)
            o_ref[:, hs] = (acc_ref[h, :, 0:HEAD_DIM] / acc_ref[h, :, HEAD_DIM:2 * HEAD_DIM]).astype(o_ref.dtype)


def _diff_attn_kernel(qi_ref, kj_ref, q_ref, k_ref, v_ref, lam_ref, g_ref, o_ref, qa_ref, s_ref, m_ref,
                      acc_ref, *, tq, tk, lam_init):
    t = pl.program_id(0)
    i = qi_ref[t]
    j = kj_ref[t]
    slopes = _alibi_slopes(C_HEADS)
    reps = tk // LANES

    @pl.when(j == 0)
    def _():
        m_ref[...] = jnp.full(m_ref.shape, NEG_INF, F32)
        acc_ref[...] = jnp.zeros(acc_ref.shape, F32)
        lane = lax.broadcasted_iota(I32, (tq, LANES), 1)
        for h in range(C_HEADS):
            qf = q_ref[:, h * LANES:(h + 1) * LANES].astype(F32) * (C_QK ** -0.5 * LOG2E)
            ext = _alibi_q_ext(slopes[h], tq)
            qa_ref[2 * h] = jnp.concatenate([jnp.where(lane < C_QK, qf, 0.0).astype(BF16), ext], axis=1)
            qa_ref[2 * h + 1] = jnp.concatenate([jnp.where(lane < C_QK, 0.0, qf).astype(BF16), ext], axis=1)

    def step(diagonal):
        k_ext = _alibi_k_ext(j * tk - i * tq, tk)
        ones = jnp.ones((tk, LANES), BF16)
        if diagonal:
            causal = lax.broadcasted_iota(I32, (tq, tk), 1) <= lax.broadcasted_iota(I32, (tq, tk), 0)

        def scores(u):
            h = u // 2
            ka = jnp.concatenate([k_ref[:, h * LANES:(h + 1) * LANES], k_ext], axis=1)
            s = lax.dot_general(qa_ref[u], ka, (((1,), (1,)), ((), ())), preferred_element_type=F32)
            if diagonal:
                s = jnp.where(causal, s, NEG_INF)
            s_ref[u % 2] = s
            m_prev = m_ref[u]
            m_cur = jnp.maximum(m_prev, jnp.max(s, axis=1, keepdims=True))
            m_ref[u] = m_cur
            return m_cur, jnp.exp2(m_prev - m_cur)

        def accumulate(u, m_cur, a):
            h = u // 2
            p = jnp.exp2(s_ref[u % 2] - jnp.tile(m_cur, (1, reps)))
            va = jnp.concatenate([v_ref[:, h * LANES:(h + 1) * LANES], ones], axis=1)
            acc_ref[u] = jnp.tile(a, (1, 2)) * acc_ref[u] + jnp.dot(p.astype(BF16), va,
                                                                     preferred_element_type=F32)

        stats = scores(0)
        for u in range(2 * C_HEADS):
            nxt = scores(u + 1) if u + 1 < 2 * C_HEADS else None
            accumulate(u, *stats)
            stats = nxt

    @pl.when(j < i)
    def _():
        step(False)

    @pl.when(j == i)
    def _():
        step(True)
        lp = lam_ref[...]
        lam = (jnp.exp(jnp.sum(lp[0:1] * lp[1:2], axis=1, keepdims=True))
               - jnp.exp(jnp.sum(lp[2:3] * lp[3:4], axis=1, keepdims=True)) + lam_init)
        for h in range(C_HEADS):
            hs = slice(h * LANES, (h + 1) * LANES)
            o = (acc_ref[2 * h, :, 0:LANES] / acc_ref[2 * h, :, LANES:2 * LANES]
                 - lam * (acc_ref[2 * h + 1, :, 0:LANES] / acc_ref[2 * h + 1, :, LANES:2 * LANES]))
            ms = jnp.mean(o * o, axis=1, keepdims=True)
            o_ref[:, hs] = (o * lax.rsqrt(ms + EPS) * g_ref[:, hs] * (1.0 - lam_init)).astype(o_ref.dtype)


_N_DSA_SCRATCH = 5


def _attn_pair_kernel(qi_ref, kj_ref, qa_ref, ka_ref, va_ref, mask_ref, qc_ref, kc_ref, vc_ref, lam_ref, g_ref,
                      oa_ref, oc_ref, *scratch, tq, tk, lam_init):
    _dsa_attn_kernel(qi_ref, kj_ref, qa_ref, ka_ref, va_ref, mask_ref, oa_ref, *scratch[:_N_DSA_SCRATCH],
                     tq=tq, tk=tk)
    _diff_attn_kernel(qi_ref, kj_ref, qc_ref, kc_ref, vc_ref, lam_ref, g_ref, oc_ref,
                      *scratch[_N_DSA_SCRATCH:], tq=tq, tk=tk, lam_init=lam_init)


def _attention_pair(pb, mask, lam_params, g, lam_init, tq):
    s = pb.shape[0]
    kern = functools.partial(_attn_pair_kernel, tq=tq, tk=tq, lam_init=lam_init)
    qi, kj = _causal_steps(s // tq)
    q_blk = lambda width, col: pl.BlockSpec((tq, width), lambda t, qi, kj: (qi[t], col // width))
    kv_blk = lambda width, col: pl.BlockSpec((tq, width), lambda t, qi, kj: (kj[t], col // width))
    grid_spec = pltpu.PrefetchScalarGridSpec(
        num_scalar_prefetch=2,
        grid=(int(qi.shape[0]),),
        in_specs=[q_blk(A_WIDTH, PB_QA), kv_blk(A_WIDTH, PB_KA), kv_blk(A_WIDTH, PB_VA),
                  pl.BlockSpec((tq, tq), lambda t, qi, kj: (qi[t], kj[t])),
                  q_blk(C_WIDTH, PB_QC), kv_blk(C_WIDTH, PB_KC), kv_blk(C_WIDTH, PB_VC),
                  pl.BlockSpec((4, C_QK), lambda t, qi, kj: (0, 0)),
                  pl.BlockSpec((1, C_WIDTH), lambda t, qi, kj: (0, 0))],
        out_specs=[pl.BlockSpec((tq, A_WIDTH), lambda t, qi, kj: (qi[t], 0)),
                   pl.BlockSpec((tq, C_WIDTH), lambda t, qi, kj: (qi[t], 0))],
        scratch_shapes=[pltpu.VMEM((A_HEADS, tq, 2 * HEAD_DIM), BF16),
                        pltpu.VMEM((tq, tq), F32),
                        pltpu.VMEM((2, tq, tq), F32),
                        pltpu.VMEM((A_HEADS, tq, LANES), F32),
                        pltpu.VMEM((A_HEADS, tq, 2 * HEAD_DIM), F32),
                        pltpu.VMEM((2 * C_HEADS, tq, 2 * LANES), BF16),
                        pltpu.VMEM((2, tq, tq), F32),
                        pltpu.VMEM((2 * C_HEADS, tq, LANES), F32),
                        pltpu.VMEM((2 * C_HEADS, tq, 2 * LANES), F32)])
    return pl.pallas_call(
        kern,
        grid_spec=grid_spec,
        out_shape=[jax.ShapeDtypeStruct((s, A_WIDTH), BF16), jax.ShapeDtypeStruct((s, C_WIDTH), BF16)],
        compiler_params=_cparams("arbitrary"),
        name="attention_pair",
    )(qi, kj, pb, pb, pb, mask, pb, pb, pb, lam_params, g)


def _split3(x):
    x1 = x.astype(BF16)
    r1 = x - x1.astype(F32)
    x2 = r1.astype(BF16)
    x3 = (r1 - x2.astype(F32)).astype(BF16)
    return x1, x2, x3


def _mlstm_kernel(q_ref, k_ref, qh_ref, kh_ref, v_ref, o_ref, gi_ref, gf_ref, cw_ref, bi_ref, bf_ref,
                  g_ref, y_ref, xq_ref, xk_ref, c_ref, n_ref, m_ref, *, L):
    c = pl.program_id(0)

    @pl.when(c == 0)
    def _():
        c_ref[...] = jnp.zeros(c_ref.shape, F32)
        n_ref[...] = jnp.zeros(n_ref.shape, F32)
        m_ref[...] = jnp.zeros(m_ref.shape, F32)

    first = (c > 0).astype(F32)
    xq_ref[0:SUBLANES, :] = qh_ref[...] * first
    xq_ref[SUBLANES:SUBLANES + L, :] = q_ref[...]
    xk_ref[0:SUBLANES, :] = kh_ref[...] * first
    xk_ref[SUBLANES:SUBLANES + L, :] = k_ref[...]
    qc = jnp.zeros((L, M_WIDTH), F32)
    kc = jnp.zeros((L, M_WIDTH), F32)
    for t in range(CONV_W):
        off = SUBLANES - (CONV_W - 1) + t
        qc = qc + xq_ref[off:off + L, :] * cw_ref[t:t + 1, 0:M_WIDTH]
        kc = kc + xk_ref[off:off + L, :] * cw_ref[t:t + 1, M_WIDTH:2 * M_WIDTH]
    qc = qc * jax.nn.sigmoid(qc)
    kc = kc * jax.nn.sigmoid(kc) * (HEAD_DIM ** -0.5)

    li = gi_ref[...] + bi_ref[...]
    fp = gf_ref[...] + bf_ref[...]
    lf = jnp.minimum(fp, 0.0) - jnp.log(1.0 + jnp.exp(-jnp.abs(fp)))
    r_i = lax.broadcasted_iota(I32, (L, L), 0)
    c_i = lax.broadcasted_iota(I32, (L, L), 1)
    tril = c_i <= r_i
    tril_b = jnp.where(tril, 1.0, 0.0).astype(BF16)
    a = jnp.zeros((L, LANES), F32)
    for part in _split3(lf):
        a = a + jnp.dot(tril_b, part, preferred_element_type=F32)
    b = li - a
    g_tot = a[L - 1:L, :]
    b_t = b.T

    m_all = m_ref[...]
    m_new_all = m_all
    for h in range(M_HEADS):
        hs = slice(h * HEAD_DIM, (h + 1) * HEAD_DIM)
        qh = qc[:, hs]
        kh = kc[:, hs]
        vh = v_ref[:, hs]
        qb = qh.astype(BF16)
        kb = kh.astype(BF16)
        m_prev = m_all[:, h:h + 1]
        d = jnp.where(tril, b_t[h:h + 1, :], NEG_INF)
        mm = jnp.maximum(m_prev, jnp.max(d, axis=1, keepdims=True))
        w_intra = jnp.exp(d - mm)
        w_inter = jnp.exp(m_prev - mm)
        qk = lax.dot_general(qb, kb, (((1,), (1,)), ((), ())), preferred_element_type=F32) * w_intra
        c_h = c_ref[h]
        num = (jnp.dot(qk.astype(BF16), vh, preferred_element_type=F32)
               + w_inter * jnp.dot(qb, c_h.astype(BF16), preferred_element_type=F32))
        den = (jnp.sum(qk, axis=1, keepdims=True)
               + w_inter * jnp.sum(qh * n_ref[h:h + 1, :], axis=1, keepdims=True))
        m_row = a[:, h:h + 1] + mm
        hh = num / jnp.maximum(jnp.abs(den), jnp.exp(-m_row))
        mm_last = mm[L - 1:L, :]
        ws = jnp.exp(b[:, h:h + 1] - mm_last)
        wc = jnp.exp(m_prev - mm_last)
        c_ref[h] = wc * c_h + lax.dot_general(kb, (ws * vh.astype(F32)).astype(BF16),
                                              (((0,), (0,)), ((), ())), preferred_element_type=F32)
        n_ref[h:h + 1, :] = wc * n_ref[h:h + 1, :] + jnp.sum(ws * kh, axis=0, keepdims=True)
        lane = lax.broadcasted_iota(I32, (1, LANES), 1)
        m_new_all = jnp.where(lane == h, g_tot + mm_last, m_new_all)
        ms = jnp.mean(hh * hh, axis=1, keepdims=True)
        hn = hh * lax.rsqrt(ms + EPS) * g_ref[:, hs]
        y_ref[:, hs] = (jax.nn.sigmoid(o_ref[:, hs]) * hn).astype(y_ref.dtype)
    m_ref[...] = m_new_all


def _mlstm(pb, pf, conv_w, b_i, b_f, g, L):
    s = pb.shape[0]
    kern = functools.partial(_mlstm_kernel, L=L)
    halo = lambda c: jnp.maximum(c * (L // SUBLANES) - 1, 0)
    return pl.pallas_call(
        kern,
        grid=(s // L,),
        in_specs=[pl.BlockSpec((L, M_WIDTH), lambda c: (c, PF_QM // M_WIDTH)),
                  pl.BlockSpec((L, M_WIDTH), lambda c: (c, PF_KM // M_WIDTH)),
                  pl.BlockSpec((SUBLANES, M_WIDTH), lambda c: (halo(c), PF_QM // M_WIDTH)),
                  pl.BlockSpec((SUBLANES, M_WIDTH), lambda c: (halo(c), PF_KM // M_WIDTH)),
                  pl.BlockSpec((L, M_WIDTH), lambda c: (c, PB_VM // M_WIDTH)),
                  pl.BlockSpec((L, M_WIDTH), lambda c: (c, PF_OM // M_WIDTH)),
                  pl.BlockSpec((L, LANES), lambda c: (c, PF_GI // LANES)),
                  pl.BlockSpec((L, LANES), lambda c: (c, PF_GF // LANES)),
                  pl.BlockSpec((CONV_W, 2 * M_WIDTH), lambda c: (0, 0)),
                  pl.BlockSpec((1, LANES), lambda c: (0, 0)),
                  pl.BlockSpec((1, LANES), lambda c: (0, 0)),
                  pl.BlockSpec((1, M_WIDTH), lambda c: (0, 0))],
        out_specs=pl.BlockSpec((L, M_WIDTH), lambda c: (c, 0)),
        out_shape=jax.ShapeDtypeStruct((s, M_WIDTH), BF16),
        scratch_shapes=[pltpu.VMEM((L + SUBLANES, M_WIDTH), F32),
                        pltpu.VMEM((L + SUBLANES, M_WIDTH), F32),
                        pltpu.VMEM((M_HEADS, HEAD_DIM, HEAD_DIM), F32),
                        pltpu.VMEM((SUBLANES, HEAD_DIM), F32),
                        pltpu.VMEM((1, LANES), F32)],
        compiler_params=_cparams("arbitrary"),
        name="mlstm",
    )(pf, pf, pf, pf, pb, pf, pf, pf, conv_w, b_i, b_f, g)


def _layer_norm(z, g, b):
    mu = jnp.mean(z, axis=-1, keepdims=True)
    zc = z - mu
    var = jnp.mean(zc * zc, axis=-1, keepdims=True)
    return zc * lax.rsqrt(var + EPS) * g + b


def _out_kernel(ya_ref, ym_ref, yc_ref, w_ref, x_ref, g_ref, b_ref, o_ref):
    acc = jnp.dot(ya_ref[...], w_ref[0:A_WIDTH, :], preferred_element_type=F32)
    acc = acc + jnp.dot(ym_ref[...], w_ref[A_WIDTH:A_WIDTH + M_WIDTH, :], preferred_element_type=F32)
    acc = acc + jnp.dot(yc_ref[...], w_ref[A_WIDTH + M_WIDTH:D_MODEL, :], preferred_element_type=F32)
    o_ref[...] = _layer_norm(ALPHA * x_ref[...] + acc, g_ref[...], b_ref[...])


def _out_proj(ya, ym, yc, w, layer, x, g, b, tm):
    s = x.shape[0]
    return pl.pallas_call(
        _out_kernel,
        grid=(s // tm,),
        in_specs=[pl.BlockSpec((tm, A_WIDTH), lambda i: (i, 0)),
                  pl.BlockSpec((tm, M_WIDTH), lambda i: (i, 0)),
                  pl.BlockSpec((tm, C_WIDTH), lambda i: (i, 0)),
                  pl.BlockSpec((None, D_MODEL, D_MODEL), lambda i: (layer, 0, 0)),
                  pl.BlockSpec((tm, D_MODEL), lambda i: (i, 0)),
                  pl.BlockSpec((1, D_MODEL), lambda i: (0, 0)),
                  pl.BlockSpec((1, D_MODEL), lambda i: (0, 0))],
        out_specs=pl.BlockSpec((tm, D_MODEL), lambda i: (i, 0)),
        out_shape=jax.ShapeDtypeStruct((s, D_MODEL), F32),
        compiler_params=_cparams("arbitrary"),
        name="out_proj_ln",
    )(ya, ym, yc, w, x, g, b)


def _ffn_kernel(x_ref, wu_ref, wd_ref, g_ref, b_ref, o_ref, xb_ref, acc_ref):
    f = pl.program_id(1)

    @pl.when(f == 0)
    def _():
        xb_ref[...] = x_ref[...].astype(BF16)
        acc_ref[...] = jnp.zeros(acc_ref.shape, F32)

    hdn = jnp.maximum(jnp.dot(xb_ref[...], wu_ref[...], preferred_element_type=F32), 0.0)
    acc_ref[...] += jnp.dot((hdn * hdn).astype(BF16), wd_ref[...], preferred_element_type=F32)

    @pl.when(f == pl.num_programs(1) - 1)
    def _():
        o_ref[...] = _layer_norm(ALPHA * x_ref[...] + acc_ref[...], g_ref[...], b_ref[...])


def _ffn(x, wu, wd, layer, g, b, tm, tf):
    s = x.shape[0]
    return pl.pallas_call(
        _ffn_kernel,
        grid=(s // tm, D_FF // tf),
        in_specs=[pl.BlockSpec((tm, D_MODEL), lambda i, f: (i, 0)),
                  pl.BlockSpec((None, D_MODEL, tf), lambda i, f: (layer, 0, f)),
                  pl.BlockSpec((None, tf, D_MODEL), lambda i, f: (layer, f, 0)),
                  pl.BlockSpec((1, D_MODEL), lambda i, f: (0, 0)),
                  pl.BlockSpec((1, D_MODEL), lambda i, f: (0, 0))],
        out_specs=pl.BlockSpec((tm, D_MODEL), lambda i, f: (i, 0)),
        out_shape=jax.ShapeDtypeStruct((s, D_MODEL), F32),
        scratch_shapes=[pltpu.VMEM((tm, D_MODEL), BF16),
                        pltpu.VMEM((tm, D_MODEL), F32)],
        compiler_params=_cparams("arbitrary", "arbitrary"),
        name="ffn_ln",
    )(x, wu, wd, g, b)


def _seg(w, i):
    return w[..., _OFFS[i]:_OFFS[i] + SIZES[i]]


def _pad_cols(w, width):
    return jnp.pad(w, [(0, 0)] * (w.ndim - 1) + [(0, width - w.shape[-1])])


def _layout_w_in(w_in):
    w_in = w_in.astype(BF16)
    (q_a, k_a, v_a, q_i, k_i, w_i, q_m, k_m, v_m, o_m, i_m, f_m, q_c, k_c, v_c) = [
        _seg(w_in, i) for i in range(len(SIZES))]
    wb = jnp.concatenate([q_a, k_a, v_a, v_m, q_c, k_c, v_c, q_i, k_i, k_i], axis=-1)
    wf = jnp.concatenate([q_m, k_m, o_m, _pad_cols(w_i, LANES), _pad_cols(i_m, LANES),
                          _pad_cols(f_m, LANES)], axis=-1)
    return _pad_cols(wb, PB_WIDTH), _pad_cols(wf, PF_WIDTH)


def _tile(n, pref):
    t = min(n, pref)
    assert n % t == 0, (n, t)
    return t


def kernel(x, w_in, conv_m, b_i, b_f, m_norm_g, lam_q1, lam_k1, lam_q2, lam_k2, c_norm_g, w_out,
           ln1_g, ln1_b, w_up, w_down, ln2_g, ln2_b):
    batch, s, d = x.shape
    assert batch == 1 and d == D_MODEL
    ksel = min(TOPK_MAX, s // 4)
    wb_all, wf_all = _layout_w_in(w_in)
    w_out_b = w_out.astype(BF16)
    w_up_b = w_up.astype(BF16)
    w_down_b = w_down.astype(BF16)
    conv_w = conv_m.reshape(DEPTH, CONV_W, 2 * M_WIDTH)
    b_i_p = _pad_cols(b_i, LANES).reshape(DEPTH, 1, LANES)
    b_f_p = _pad_cols(b_f, LANES).reshape(DEPTH, 1, LANES)
    lam_p = jnp.stack([lam_q1, lam_k1, lam_q2, lam_k2], axis=1)

    tm_proj = _tile(s, 1024)
    t_attn = _tile(s, 512)
    tq_idx = _tile(s, 256)
    tk_idx = _tile(s, 512)
    l_chunk = _tile(s, 256)
    tm_out = _tile(s, 512)
    tm_ffn = _tile(s, 512)

    h = x.reshape(s, d)
    for l in range(DEPTH):
        pb = _matmul(h, wb_all, l, BF16, tm_proj, PB_WIDTH // 3, "proj_bf16")
        pf = _matmul(h, wf_all, l, F32, tm_proj, PF_WIDTH // 2, "proj_f32")
        mask = _dsa_mask(pb, pf, ksel, tq_idx, tk_idx)
        lam_init = 0.8 - 0.6 * math.exp(-0.3 * l)
        y_a, y_c = _attention_pair(pb, mask, lam_p[l], c_norm_g[l].reshape(1, C_WIDTH), lam_init, t_attn)
        y_m = _mlstm(pb, pf, conv_w[l], b_i_p[l], b_f_p[l], m_norm_g[l].reshape(1, M_WIDTH), l_chunk)
        h = _out_proj(y_a, y_m, y_c, w_out_b, l, h, ln1_g[l].reshape(1, d), ln1_b[l].reshape(1, d), tm_out)
        h = _ffn(h, w_up_b, w_down_b, l, ln2_g[l].reshape(1, d), ln2_b[l].reshape(1, d), tm_ffn, 1024)
    return h.reshape(batch, s, d)
```

```python
import functools
import math

import jax
import jax.numpy as jnp
import numpy as np
from jax import lax
from jax.experimental import pallas as pl
from jax.experimental.pallas import tpu as pltpu

F32 = jnp.float32
BF16 = jnp.bfloat16
I32 = jnp.int32

D_MODEL = 2048
DEPTH = 4
HEAD_DIM = 128
A_HEADS = 6
A_WIDTH = A_HEADS * HEAD_DIM
IDX_HEADS = 8
IDX_DIM = 64
TOPK_MAX = 256
M_HEADS = 6
M_WIDTH = M_HEADS * HEAD_DIM
CONV_W = 4
C_HEADS = 4
C_QK = 64
C_WIDTH = C_HEADS * 2 * C_QK
D_FF = 4 * D_MODEL
ALPHA = (2.0 * DEPTH) ** 0.25
EPS = 1e-5

SIZES = (A_WIDTH, A_WIDTH, A_WIDTH, IDX_HEADS * IDX_DIM, IDX_DIM, IDX_HEADS,
         M_WIDTH, M_WIDTH, M_WIDTH, M_WIDTH, M_HEADS, M_HEADS,
         C_WIDTH, C_WIDTH, C_WIDTH)
_OFFS = tuple(int(sum(SIZES[:i])) for i in range(len(SIZES)))

LANES = 128
SUBLANES = 8
VMEM_LIMIT_BYTES = 56 * 1024 * 1024
INT_MIN = -2 ** 31
NEG_INF = float("-inf")
LOG2E = math.log2(math.e)

PB_QA, PB_KA, PB_VA, PB_VM = 0, 768, 1536, 2304
PB_QC, PB_KC, PB_VC, PB_QI = 3072, 3584, 4096, 4608
PB_KK = 5120
PB_WIDTH = 5376
PF_QM, PF_KM, PF_OM = 0, 768, 1536
PF_WI, PF_GI, PF_GF = 2304, 2432, 2560
PF_WIDTH = 3072


def _cparams(*sem):
    return pltpu.CompilerParams(dimension_semantics=sem, vmem_limit_bytes=VMEM_LIMIT_BYTES)


def _alibi_slopes(n):
    return [2.0 ** (-8.0 * (h + 1) / n) for h in range(n)]


def _mm_kernel(x_ref, w_ref, o_ref, xb_ref):
    @pl.when(pl.program_id(1) == 0)
    def _():
        xb_ref[...] = x_ref[...].astype(BF16)

    o_ref[...] = jnp.dot(xb_ref[...], w_ref[...], preferred_element_type=F32).astype(o_ref.dtype)


def _matmul(x, w, layer, out_dtype, tm, tn, name):
    m, k = x.shape
    n = w.shape[2]
    return pl.pallas_call(
        _mm_kernel,
        grid=(m // tm, n // tn),
        in_specs=[pl.BlockSpec((tm, k), lambda i, j: (i, 0)),
                  pl.BlockSpec((None, k, tn), lambda i, j: (layer, 0, j))],
        out_specs=pl.BlockSpec((tm, tn), lambda i, j: (i, j)),
        out_shape=jax.ShapeDtypeStruct((m, n), out_dtype),
        scratch_shapes=[pltpu.VMEM((tm, k), BF16)],
        compiler_params=_cparams("arbitrary", "arbitrary"),
        name=name,
    )(x, w)


def _order_key(x):
    bits = lax.bitcast_convert_type(x, I32)
    key = jnp.where(bits < 0, bits ^ jnp.int32(0x7FFFFFFF), bits)
    return jnp.where(bits == jnp.int32(INT_MIN), 0, key)


def _idx_kernel(qi_ref, kk_ref, wi_ref, mask_ref, q8_ref, w8_ref, keys_ref, gm_ref,
                *, tq, tk, nk, ksel, jbits):
    i = pl.program_id(0)
    nck = ((i + 1) * tq + tk - 1) // tk
    classes = gm_ref.shape[0]
    acc_rows = 32

    lane = lax.broadcasted_iota(I32, (tq, LANES), 1)
    lo_half = jnp.where(lane < IDX_DIM, 1.0, 0.0).astype(F32)
    hi_half = 1.0 - lo_half
    for h in range(IDX_HEADS):
        qp = qi_ref[:, (h // 2) * LANES:(h // 2 + 1) * LANES].astype(F32)
        q8_ref[h * tq:(h + 1) * tq, :] = (qp * (lo_half if h % 2 == 0 else hi_half)).astype(BF16)
    w8_ref[...] = wi_ref[...].T[0:SUBLANES, :]
    gm_ref[...] = jnp.full(gm_ref.shape, NEG_INF, F32)

    q_idx = i * tq + lax.broadcasted_iota(I32, (tk, tq), 1)
    k_off = lax.broadcasted_iota(I32, (tk, tq), 0)

    def score_chunk(c, carry, masked):
        kc = kk_ref[pl.ds(pl.multiple_of(c * tk, tk), tk), :]
        y = lax.dot_general(kc, q8_ref[...], (((1,), (1,)), ((), ())), preferred_element_type=F32)
        sc = jnp.maximum(y[:, 0:tq], 0.0) * w8_ref[0:1, :]
        for h in range(1, IDX_HEADS):
            sc = sc + jnp.maximum(y[:, h * tq:(h + 1) * tq], 0.0) * w8_ref[h:h + 1, :]
        key = _order_key(sc)
        if masked:
            causal = c * tk + k_off <= q_idx
            key = jnp.where(causal, key, jnp.int32(INT_MIN))
            sc = jnp.where(causal, sc, NEG_INF)
        keys_ref[c] = key
        for cls in range(classes):
            g = gm_ref[cls]
            for r in range(cls * SUBLANES, tk, classes * SUBLANES):
                g = jnp.maximum(g, sc[r:r + SUBLANES, :])
            gm_ref[cls] = g
        return carry

    n_full = (i * tq) // tk
    lax.fori_loop(0, n_full, functools.partial(score_chunk, masked=False), 0)
    lax.fori_loop(n_full, nck, functools.partial(score_chunk, masked=True), 0)

    def count(pred):
        def body(c, acc):
            for r in range(0, tk, acc_rows):
                acc = jnp.where(pred(keys_ref[c, r:r + acc_rows, :], c * tk + r), acc + 1.0, acc)
            return acc
        acc = lax.fori_loop(0, nck, body, jnp.zeros((acc_rows, tq), F32))
        return jnp.sum(acc, axis=0, keepdims=True)

    g_min = gm_ref[0]
    g_max = gm_ref[0]
    for cls in range(1, classes):
        g_min = jnp.minimum(g_min, gm_ref[cls])
        g_max = jnp.maximum(g_max, gm_ref[cls])
    lo_f = jnp.min(g_min, axis=0, keepdims=True)
    hi_f = jnp.max(g_max, axis=0, keepdims=True)
    n_valid = i * tq + lax.broadcasted_iota(I32, (1, tq), 1) + 1
    few = n_valid <= ksel
    lo0 = jnp.where(jnp.logical_or(few, lo_f == NEG_INF), jnp.int32(INT_MIN), _order_key(lo_f))
    hi0 = jnp.where(few, jnp.int32(INT_MIN), _order_key(hi_f))

    def unresolved(st):
        lo, hi, _ = st
        return jnp.max(jnp.where(lo < hi, 1.0, 0.0)) > 0.0

    def bisect(st):
        lo, hi, n_lo = st
        mid = (lo >> 1) + (hi >> 1) + ((lo | hi) & 1)
        cnt = count(lambda k, r0: k >= mid)
        ge = cnt >= ksel
        exact = cnt == ksel
        return (jnp.where(ge, mid, lo), jnp.where(exact, mid, jnp.where(ge, hi, mid - 1)),
                jnp.where(ge, cnt, n_lo))

    unmeasured = jnp.full((1, tq), float(2 ** 30), F32)
    thr, _, n_ge = lax.while_loop(unresolved, lambda st: bisect(bisect(st)), (lo0, hi0, unmeasured))
    excess = jnp.logical_and(thr > jnp.int32(INT_MIN), n_ge > ksel)
    any_excess = jnp.max(jnp.where(excess, 1.0, 0.0)) > 0.0
    sub = lax.broadcasted_iota(I32, (acc_rows, tq), 0)

    def tie_search():
        need = ksel - count(lambda k, r0: k > thr)

        def jbody(b, jc):
            cand = jc + lax.shift_left(jnp.int32(1), jbits - 1 - b)
            cnt = count(lambda k, r0: jnp.logical_and(k == thr, r0 + sub < cand))
            return jnp.where(cnt <= need, cand, jc)
        return lax.fori_loop(0, jbits, jbody, jnp.zeros((1, tq), I32))

    def write_mask(select):
        for c in range(nk):
            @pl.when(c < nck)
            def _(c=c):
                sel = jnp.where(select(keys_ref[c], c), 1.0, 0.0).astype(F32)
                mask_ref[:, c * tk:(c + 1) * tk] = sel.T.astype(BF16)

            @pl.when(c >= nck)
            def _(c=c):
                mask_ref[:, c * tk:(c + 1) * tk] = jnp.zeros((tq, tk), BF16)

    @pl.when(any_excess)
    def _():
        jcut = tie_search()

        def select(key, c):
            tie = jnp.logical_and(key == thr, c * tk + k_off < jcut)
            return jnp.logical_and(key > jnp.int32(INT_MIN), jnp.logical_or(key > thr, tie))
        write_mask(select)

    @pl.when(jnp.logical_not(any_excess))
    def _():
        floor = jnp.maximum(thr, jnp.int32(INT_MIN + 1))
        write_mask(lambda key, c: key >= floor)


def _dsa_mask(pb, pf, ksel, tq, tk):
    s = pb.shape[0]
    nk = s // tk
    jbits = int(s).bit_length()
    classes = min(32, tk // SUBLANES)
    assert ksel <= SUBLANES * classes, "the threshold bracket relies on that many disjoint key groups"
    kern = functools.partial(_idx_kernel, tq=tq, tk=tk, nk=nk, ksel=ksel, jbits=jbits)
    return pl.pallas_call(
        kern,
        grid=(s // tq,),
        in_specs=[pl.BlockSpec((tq, IDX_HEADS * IDX_DIM), lambda i: (i, PB_QI // (IDX_HEADS * IDX_DIM))),
                  pl.BlockSpec((s, LANES), lambda i: (0, PB_KK // LANES)),
                  pl.BlockSpec((tq, LANES), lambda i: (i, PF_WI // LANES))],
        out_specs=pl.BlockSpec((tq, s), lambda i: (i, 0)),
        out_shape=jax.ShapeDtypeStruct((s, s), BF16),
        scratch_shapes=[pltpu.VMEM((IDX_HEADS * tq, LANES), BF16),
                        pltpu.VMEM((SUBLANES, tq), F32),
                        pltpu.VMEM((nk, tk, tq), I32),
                        pltpu.VMEM((classes, SUBLANES, tq), F32)],
        compiler_params=_cparams("arbitrary"),
        name="dsa_index_mask",
    )(pb, pb, pf)


def _causal_steps(n):
    qi = [i for i in range(n) for _ in range(i + 1)]
    kj = [j for i in range(n) for j in range(i + 1)]
    return jnp.asarray(qi, I32), jnp.asarray(kj, I32)


def _bf16_terms(x):
    out = []
    r = np.float32(x)
    for _ in range(3):
        t = np.float32(np.asarray(r, dtype=jnp.bfloat16))
        out.append(float(t))
        r = np.float32(r - t)
    return out


def _alibi_q_ext(slope, rows):
    s1, s2, s3 = _bf16_terms(slope * LOG2E)
    lane = lax.broadcasted_iota(I32, (rows, LANES), 1)
    ext = jnp.where(lane < 3, s1, jnp.where(lane < 6, s2, jnp.where(lane < 9, s3, 0.0)))
    return ext.astype(BF16)


def _alibi_k_ext(first_pos, rows):
    pos = (first_pos + lax.broadcasted_iota(I32, (rows, LANES), 0)).astype(F32)
    p1, p2, p3 = _split3(pos)
    lane = lax.broadcasted_iota(I32, (rows, LANES), 1)
    sel = lane % 3
    ext = jnp.where(sel == 0, p1.astype(F32), jnp.where(sel == 1, p2.astype(F32), p3.astype(F32)))
    return jnp.where(lane < 9, ext, 0.0).astype(BF16)


def _dsa_attn_kernel(qi_ref, kj_ref, q_ref, k_ref, v_ref, mask_ref, o_ref, qa_ref, mb_ref, s_ref, m_ref,
                     acc_ref, *, tq, tk):
    t = pl.program_id(0)
    i = qi_ref[t]
    j = kj_ref[t]
    slopes = _alibi_slopes(A_HEADS)
    reps = tk // LANES

    @pl.when(j == 0)
    def _():
        m_ref[...] = jnp.full(m_ref.shape, NEG_INF, F32)
        acc_ref[...] = jnp.zeros(acc_ref.shape, F32)
        for h in range(A_HEADS):
            hs = slice(h * HEAD_DIM, (h + 1) * HEAD_DIM)
            qh = (q_ref[:, hs].astype(F32) * (HEAD_DIM ** -0.5 * LOG2E)).astype(BF16)
            qa_ref[h] = jnp.concatenate([qh, _alibi_q_ext(slopes[h], tq)], axis=1)

    mf = mask_ref[...].astype(F32)
    mb_ref[...] = jnp.where(mf > 0.0, mf - 1.0, NEG_INF)
    k_ext = _alibi_k_ext(j * tk - i * tq, tk)
    ones = jnp.ones((tk, LANES), BF16)

    def scores(h):
        hs = slice(h * HEAD_DIM, (h + 1) * HEAD_DIM)
        ka = jnp.concatenate([k_ref[:, hs], k_ext], axis=1)
        s = lax.dot_general(qa_ref[h], ka, (((1,), (1,)), ((), ())), preferred_element_type=F32)
        s = s + mb_ref[...]
        s_ref[h % 2] = s
        m_prev = m_ref[h]
        m_cur = jnp.maximum(m_prev, jnp.max(s, axis=1, keepdims=True))
        m_ref[h] = m_cur
        m_safe = jnp.where(m_cur == NEG_INF, 0.0, m_cur)
        return m_safe, jnp.exp2(m_prev - m_safe)

    def accumulate(h, m_safe, a):
        hs = slice(h * HEAD_DIM, (h + 1) * HEAD_DIM)
        p = jnp.exp2(s_ref[h % 2] - jnp.tile(m_safe, (1, reps)))
        va = jnp.concatenate([v_ref[:, hs], ones], axis=1)
        acc_ref[h] = jnp.tile(a, (1, 2)) * acc_ref[h] + jnp.dot(p.astype(BF16), va,
                                                                 preferred_element_type=F32)

    stats = scores(0)
    for h in range(A_HEADS):
        nxt = scores(h + 1) if h + 1 < A_HEADS else None
        accumulate(h, *stats)
        stats = nxt

    @pl.when(j == i)
    def _():
        for h in range(A_HEADS):
            hs = slice(h * HEAD_DIM, (h + 1) * HEAD_DIM)
            o_ref[:, hs] = (acc_ref[h, :, 0:HEAD_DIM] / acc_ref[h, :, HEAD_DIM:2 * HEAD_DIM]).astype(o_ref.dtype)


def _diff_attn_kernel(qi_ref, kj_ref, q_ref, k_ref, v_ref, lam_ref, g_ref, o_ref, qa_ref, s_ref, m_ref,
                      acc_ref, *, tq, tk, lam_init):
    t = pl.program_id(0)
    i = qi_ref[t]
    j = kj_ref[t]
    slopes = _alibi_slopes(C_HEADS)
    reps = tk // LANES

    @pl.when(j == 0)
    def _():
        m_ref[...] = jnp.full(m_ref.shape, NEG_INF, F32)
        acc_ref[...] = jnp.zeros(acc_ref.shape, F32)
        lane = lax.broadcasted_iota(I32, (tq, LANES), 1)
        for h in range(C_HEADS):
            qf = q_ref[:, h * LANES:(h + 1) * LANES].astype(F32) * (C_QK ** -0.5 * LOG2E)
            ext = _alibi_q_ext(slopes[h], tq)
            qa_ref[2 * h] = jnp.concatenate([jnp.where(lane < C_QK, qf, 0.0).astype(BF16), ext], axis=1)
            qa_ref[2 * h + 1] = jnp.concatenate([jnp.where(lane < C_QK, 0.0, qf).astype(BF16), ext], axis=1)

    def step(diagonal):
        k_ext = _alibi_k_ext(j * tk - i * tq, tk)
        ones = jnp.ones((tk, LANES), BF16)
        if diagonal:
            causal = lax.broadcasted_iota(I32, (tq, tk), 1) <= lax.broadcasted_iota(I32, (tq, tk), 0)

        def scores(u):
            h = u // 2
            ka = jnp.concatenate([k_ref[:, h * LANES:(h + 1) * LANES], k_ext], axis=1)
            s = lax.dot_general(qa_ref[u], ka, (((1,), (1,)), ((), ())), preferred_element_type=F32)
            if diagonal:
                s = jnp.where(causal, s, NEG_INF)
            s_ref[u % 2] = s
            m_prev = m_ref[u]
            m_cur = jnp.maximum(m_prev, jnp.max(s, axis=1, keepdims=True))
            m_ref[u] = m_cur
            return m_cur, jnp.exp2(m_prev - m_cur)

        def accumulate(u, m_cur, a):
            h = u // 2
            p = jnp.exp2(s_ref[u % 2] - jnp.tile(m_cur, (1, reps)))
            va = jnp.concatenate([v_ref[:, h * LANES:(h + 1) * LANES], ones], axis=1)
            acc_ref[u] = jnp.tile(a, (1, 2)) * acc_ref[u] + jnp.dot(p.astype(BF16), va,
                                                                     preferred_element_type=F32)

        stats = scores(0)
        for u in range(2 * C_HEADS):
            nxt = scores(u + 1) if u + 1 < 2 * C_HEADS else None
            accumulate(u, *stats)
            stats = nxt

    @pl.when(j < i)
    def _():
        step(False)

    @pl.when(j == i)
    def _():
        step(True)
        lp = lam_ref[...]
        lam = (jnp.exp(jnp.sum(lp[0:1] * lp[1:2], axis=1, keepdims=True))
               - jnp.exp(jnp.sum(lp[2:3] * lp[3:4], axis=1, keepdims=True)) + lam_init)
        for h in range(C_HEADS):
            hs = slice(h * LANES, (h + 1) * LANES)
            o = (acc_ref[2 * h, :, 0:LANES] / acc_ref[2 * h, :, LANES:2 * LANES]
                 - lam * (acc_ref[2 * h + 1, :, 0:LANES] / acc_ref[2 * h + 1, :, LANES:2 * LANES]))
            ms = jnp.mean(o * o, axis=1, keepdims=True)
            o_ref[:, hs] = (o * lax.rsqrt(ms + EPS) * g_ref[:, hs] * (1.0 - lam_init)).astype(o_ref.dtype)


_N_DSA_SCRATCH = 5


def _attn_pair_kernel(qi_ref, kj_ref, qa_ref, ka_ref, va_ref, mask_ref, qc_ref, kc_ref, vc_ref, lam_ref, g_ref,
                      oa_ref, oc_ref, *scratch, tq, tk, lam_init):
    _dsa_attn_kernel(qi_ref, kj_ref, qa_ref, ka_ref, va_ref, mask_ref, oa_ref, *scratch[:_N_DSA_SCRATCH],
                     tq=tq, tk=tk)
    _diff_attn_kernel(qi_ref, kj_ref, qc_ref, kc_ref, vc_ref, lam_ref, g_ref, oc_ref,
                      *scratch[_N_DSA_SCRATCH:], tq=tq, tk=tk, lam_init=lam_init)


def _attention_pair(pb, mask, lam_params, g, lam_init, tq):
    s = pb.shape[0]
    kern = functools.partial(_attn_pair_kernel, tq=tq, tk=tq, lam_init=lam_init)
    qi, kj = _causal_steps(s // tq)
    q_blk = lambda width, col: pl.BlockSpec((tq, width), lambda t, qi, kj: (qi[t], col // width))
    kv_blk = lambda width, col: pl.BlockSpec((tq, width), lambda t, qi, kj: (kj[t], col // width))
    grid_spec = pltpu.PrefetchScalarGridSpec(
        num_scalar_prefetch=2,
        grid=(int(qi.shape[0]),),
        in_specs=[q_blk(A_WIDTH, PB_QA), kv_blk(A_WIDTH, PB_KA), kv_blk(A_WIDTH, PB_VA),
                  pl.BlockSpec((tq, tq), lambda t, qi, kj: (qi[t], kj[t])),
                  q_blk(C_WIDTH, PB_QC), kv_blk(C_WIDTH, PB_KC), kv_blk(C_WIDTH, PB_VC),
                  pl.BlockSpec((4, C_QK), lambda t, qi, kj: (0, 0)),
                  pl.BlockSpec((1, C_WIDTH), lambda t, qi, kj: (0, 0))],
        out_specs=[pl.BlockSpec((tq, A_WIDTH), lambda t, qi, kj: (qi[t], 0)),
                   pl.BlockSpec((tq, C_WIDTH), lambda t, qi, kj: (qi[t], 0))],
        scratch_shapes=[pltpu.VMEM((A_HEADS, tq, 2 * HEAD_DIM), BF16),
                        pltpu.VMEM((tq, tq), F32),
                        pltpu.VMEM((2, tq, tq), F32),
                        pltpu.VMEM((A_HEADS, tq, LANES), F32),
                        pltpu.VMEM((A_HEADS, tq, 2 * HEAD_DIM), F32),
                        pltpu.VMEM((2 * C_HEADS, tq, 2 * LANES), BF16),
                        pltpu.VMEM((2, tq, tq), F32),
                        pltpu.VMEM((2 * C_HEADS, tq, LANES), F32),
                        pltpu.VMEM((2 * C_HEADS, tq, 2 * LANES), F32)])
    return pl.pallas_call(
        kern,
        grid_spec=grid_spec,
        out_shape=[jax.ShapeDtypeStruct((s, A_WIDTH), BF16), jax.ShapeDtypeStruct((s, C_WIDTH), BF16)],
        compiler_params=_cparams("arbitrary"),
        name="attention_pair",
    )(qi, kj, pb, pb, pb, mask, pb, pb, pb, lam_params, g)


def _split3(x):
    x1 = x.astype(BF16)
    r1 = x - x1.astype(F32)
    x2 = r1.astype(BF16)
    x3 = (r1 - x2.astype(F32)).astype(BF16)
    return x1, x2, x3


def _mlstm_kernel(q_ref, k_ref, qh_ref, kh_ref, v_ref, o_ref, gi_ref, gf_ref, cw_ref, bi_ref, bf_ref,
                  g_ref, y_ref, xq_ref, xk_ref, c_ref, n_ref, m_ref, *, L):
    c = pl.program_id(0)

    @pl.when(c == 0)
    def _():
        c_ref[...] = jnp.zeros(c_ref.shape, F32)
        n_ref[...] = jnp.zeros(n_ref.shape, F32)
        m_ref[...] = jnp.zeros(m_ref.shape, F32)

    first = (c > 0).astype(F32)
    xq_ref[0:SUBLANES, :] = qh_ref[...] * first
    xq_ref[SUBLANES:SUBLANES + L, :] = q_ref[...]
    xk_ref[0:SUBLANES, :] = kh_ref[...] * first
    xk_ref[SUBLANES:SUBLANES + L, :] = k_ref[...]
    qc = jnp.zeros((L, M_WIDTH), F32)
    kc = jnp.zeros((L, M_WIDTH), F32)
    for t in range(CONV_W):
        off = SUBLANES - (CONV_W - 1) + t
        qc = qc + xq_ref[off:off + L, :] * cw_ref[t:t + 1, 0:M_WIDTH]
        kc = kc + xk_ref[off:off + L, :] * cw_ref[t:t + 1, M_WIDTH:2 * M_WIDTH]
    qc = qc * jax.nn.sigmoid(qc)
    kc = kc * jax.nn.sigmoid(kc) * (HEAD_DIM ** -0.5)

    li = gi_ref[...] + bi_ref[...]
    fp = gf_ref[...] + bf_ref[...]
    lf = jnp.minimum(fp, 0.0) - jnp.log(1.0 + jnp.exp(-jnp.abs(fp)))
    r_i = lax.broadcasted_iota(I32, (L, L), 0)
    c_i = lax.broadcasted_iota(I32, (L, L), 1)
    tril = c_i <= r_i
    tril_b = jnp.where(tril, 1.0, 0.0).astype(BF16)
    a = jnp.zeros((L, LANES), F32)
    for part in _split3(lf):
        a = a + jnp.dot(tril_b, part, preferred_element_type=F32)
    b = li - a
    g_tot = a[L - 1:L, :]
    b_t = b.T

    m_all = m_ref[...]
    m_new_all = m_all
    for h in range(M_HEADS):
        hs = slice(h * HEAD_DIM, (h + 1) * HEAD_DIM)
        qh = qc[:, hs]
        kh = kc[:, hs]
        vh = v_ref[:, hs]
        qb = qh.astype(BF16)
        kb = kh.astype(BF16)
        m_prev = m_all[:, h:h + 1]
        d = jnp.where(tril, b_t[h:h + 1, :], NEG_INF)
        mm = jnp.maximum(m_prev, jnp.max(d, axis=1, keepdims=True))
        w_intra = jnp.exp(d - mm)
        w_inter = jnp.exp(m_prev - mm)
        qk = lax.dot_general(qb, kb, (((1,), (1,)), ((), ())), preferred_element_type=F32) * w_intra
        c_h = c_ref[h]
        num = (jnp.dot(qk.astype(BF16), vh, preferred_element_type=F32)
               + w_inter * jnp.dot(qb, c_h.astype(BF16), preferred_element_type=F32))
        den = (jnp.sum(qk, axis=1, keepdims=True)
               + w_inter * jnp.sum(qh * n_ref[h:h + 1, :], axis=1, keepdims=True))
        m_row = a[:, h:h + 1] + mm
        hh = num / jnp.maximum(jnp.abs(den), jnp.exp(-m_row))
        mm_last = mm[L - 1:L, :]
        ws = jnp.exp(b[:, h:h + 1] - mm_last)
        wc = jnp.exp(m_prev - mm_last)
        c_ref[h] = wc * c_h + lax.dot_general(kb, (ws * vh.astype(F32)).astype(BF16),
                                              (((0,), (0,)), ((), ())), preferred_element_type=F32)
        n_ref[h:h + 1, :] = wc * n_ref[h:h + 1, :] + jnp.sum(ws * kh, axis=0, keepdims=True)
        lane = lax.broadcasted_iota(I32, (1, LANES), 1)
        m_new_all = jnp.where(lane == h, g_tot + mm_last, m_new_all)
        ms = jnp.mean(hh * hh, axis=1, keepdims=True)
        hn = hh * lax.rsqrt(ms + EPS) * g_ref[:, hs]
        y_ref[:, hs] = (jax.nn.sigmoid(o_ref[:, hs]) * hn).astype(y_ref.dtype)
    m_ref[...] = m_new_all


def _mlstm(pb, pf, conv_w, b_i, b_f, g, L):
    s = pb.shape[0]
    kern = functools.partial(_mlstm_kernel, L=L)
    halo = lambda c: jnp.maximum(c * (L // SUBLANES) - 1, 0)
    return pl.pallas_call(
        kern,
        grid=(s // L,),
        in_specs=[pl.BlockSpec((L, M_WIDTH), lambda c: (c, PF_QM // M_WIDTH)),
                  pl.BlockSpec((L, M_WIDTH), lambda c: (c, PF_KM // M_WIDTH)),
                  pl.BlockSpec((SUBLANES, M_WIDTH), lambda c: (halo(c), PF_QM // M_WIDTH)),
                  pl.BlockSpec((SUBLANES, M_WIDTH), lambda c: (halo(c), PF_KM // M_WIDTH)),
                  pl.BlockSpec((L, M_WIDTH), lambda c: (c, PB_VM // M_WIDTH)),
                  pl.BlockSpec((L, M_WIDTH), lambda c: (c, PF_OM // M_WIDTH)),
                  pl.BlockSpec((L, LANES), lambda c: (c, PF_GI // LANES)),
                  pl.BlockSpec((L, LANES), lambda c: (c, PF_GF // LANES)),
                  pl.BlockSpec((CONV_W, 2 * M_WIDTH), lambda c: (0, 0)),
                  pl.BlockSpec((1, LANES), lambda c: (0, 0)),
                  pl.BlockSpec((1, LANES), lambda c: (0, 0)),
                  pl.BlockSpec((1, M_WIDTH), lambda c: (0, 0))],
        out_specs=pl.BlockSpec((L, M_WIDTH), lambda c: (c, 0)),
        out_shape=jax.ShapeDtypeStruct((s, M_WIDTH), BF16),
        scratch_shapes=[pltpu.VMEM((L + SUBLANES, M_WIDTH), F32),
                        pltpu.VMEM((L + SUBLANES, M_WIDTH), F32),
                        pltpu.VMEM((M_HEADS, HEAD_DIM, HEAD_DIM), F32),
                        pltpu.VMEM((SUBLANES, HEAD_DIM), F32),
                        pltpu.VMEM((1, LANES), F32)],
        compiler_params=_cparams("arbitrary"),
        name="mlstm",
    )(pf, pf, pf, pf, pb, pf, pf, pf, conv_w, b_i, b_f, g)


def _layer_norm(z, g, b):
    mu = jnp.mean(z, axis=-1, keepdims=True)
    zc = z - mu
    var = jnp.mean(zc * zc, axis=-1, keepdims=True)
    return zc * lax.rsqrt(var + EPS) * g + b


def _out_kernel(ya_ref, ym_ref, yc_ref, w_ref, x_ref, g_ref, b_ref, o_ref):
    acc = jnp.dot(ya_ref[...], w_ref[0:A_WIDTH, :], preferred_element_type=F32)
    acc = acc + jnp.dot(ym_ref[...], w_ref[A_WIDTH:A_WIDTH + M_WIDTH, :], preferred_element_type=F32)
    acc = acc + jnp.dot(yc_ref[...], w_ref[A_WIDTH + M_WIDTH:D_MODEL, :], preferred_element_type=F32)
    o_ref[...] = _layer_norm(ALPHA * x_ref[...] + acc, g_ref[...], b_ref[...])


def _out_proj(ya, ym, yc, w, layer, x, g, b, tm):
    s = x.shape[0]
    return pl.pallas_call(
        _out_kernel,
        grid=(s // tm,),
        in_specs=[pl.BlockSpec((tm, A_WIDTH), lambda i: (i, 0)),
                  pl.BlockSpec((tm, M_WIDTH), lambda i: (i, 0)),
                  pl.BlockSpec((tm, C_WIDTH), lambda i: (i, 0)),
                  pl.BlockSpec((None, D_MODEL, D_MODEL), lambda i: (layer, 0, 0)),
                  pl.BlockSpec((tm, D_MODEL), lambda i: (i, 0)),
                  pl.BlockSpec((1, D_MODEL), lambda i: (0, 0)),
                  pl.BlockSpec((1, D_MODEL), lambda i: (0, 0))],
        out_specs=pl.BlockSpec((tm, D_MODEL), lambda i: (i, 0)),
        out_shape=jax.ShapeDtypeStruct((s, D_MODEL), F32),
        compiler_params=_cparams("arbitrary"),
        name="out_proj_ln",
    )(ya, ym, yc, w, x, g, b)


def _ffn_kernel(x_ref, wu_ref, wd_ref, g_ref, b_ref, o_ref, xb_ref, acc_ref):
    f = pl.program_id(1)

    @pl.when(f == 0)
    def _():
        xb_ref[...] = x_ref[...].astype(BF16)
        acc_ref[...] = jnp.zeros(acc_ref.shape, F32)

    hdn = jnp.maximum(jnp.dot(xb_ref[...], wu_ref[...], preferred_element_type=F32), 0.0)
    acc_ref[...] += jnp.dot((hdn * hdn).astype(BF16), wd_ref[...], preferred_element_type=F32)

    @pl.when(f == pl.num_programs(1) - 1)
    def _():
        o_ref[...] = _layer_norm(ALPHA * x_ref[...] + acc_ref[...], g_ref[...], b_ref[...])


def _ffn(x, wu, wd, layer, g, b, tm, tf):
    s = x.shape[0]
    return pl.pallas_call(
        _ffn_kernel,
        grid=(s // tm, D_FF // tf),
        in_specs=[pl.BlockSpec((tm, D_MODEL), lambda i, f: (i, 0)),
                  pl.BlockSpec((None, D_MODEL, tf), lambda i, f: (layer, 0, f)),
                  pl.BlockSpec((None, tf, D_MODEL), lambda i, f: (layer, f, 0)),
                  pl.BlockSpec((1, D_MODEL), lambda i, f: (0, 0)),
                  pl.BlockSpec((1, D_MODEL), lambda i, f: (0, 0))],
        out_specs=pl.BlockSpec((tm, D_MODEL), lambda i, f: (i, 0)),
        out_shape=jax.ShapeDtypeStruct((s, D_MODEL), F32),
        scratch_shapes=[pltpu.VMEM((tm, D_MODEL), BF16),
                        pltpu.VMEM((tm, D_MODEL), F32)],
        compiler_params=_cparams("arbitrary", "arbitrary"),
        name="ffn_ln",
    )(x, wu, wd, g, b)


def _seg(w, i):
    return w[..., _OFFS[i]:_OFFS[i] + SIZES[i]]


def _pad_cols(w, width):
    return jnp.pad(w, [(0, 0)] * (w.ndim - 1) + [(0, width - w.shape[-1])])


def _layout_w_in(w_in):
    w_in = w_in.astype(BF16)
    (q_a, k_a, v_a, q_i, k_i, w_i, q_m, k_m, v_m, o_m, i_m, f_m, q_c, k_c, v_c) = [
        _seg(w_in, i) for i in range(len(SIZES))]
    wb = jnp.concatenate([q_a, k_a, v_a, v_m, q_c, k_c, v_c, q_i, k_i, k_i], axis=-1)
    wf = jnp.concatenate([q_m, k_m, o_m, _pad_cols(w_i, LANES), _pad_cols(i_m, LANES),
                          _pad_cols(f_m, LANES)], axis=-1)
    return _pad_cols(wb, PB_WIDTH), _pad_cols(wf, PF_WIDTH)


def _tile(n, pref):
    t = min(n, pref)
    assert n % t == 0, (n, t)
    return t


def kernel(x, w_in, conv_m, b_i, b_f, m_norm_g, lam_q1, lam_k1, lam_q2, lam_k2, c_norm_g, w_out,
           ln1_g, ln1_b, w_up, w_down, ln2_g, ln2_b):
    batch, s, d = x.shape
    assert batch == 1 and d == D_MODEL
    ksel = min(TOPK_MAX, s // 4)
    wb_all, wf_all = _layout_w_in(w_in)
    w_out_b = w_out.astype(BF16)
    w_up_b = w_up.astype(BF16)
    w_down_b = w_down.astype(BF16)
    conv_w = conv_m.reshape(DEPTH, CONV_W, 2 * M_WIDTH)
    b_i_p = _pad_cols(b_i, LANES).reshape(DEPTH, 1, LANES)
    b_f_p = _pad_cols(b_f, LANES).reshape(DEPTH, 1, LANES)
    lam_p = jnp.stack([lam_q1, lam_k1, lam_q2, lam_k2], axis=1)

    tm_proj = _tile(s, 1024)
    t_attn = _tile(s, 512)
    tq_idx = _tile(s, 256)
    tk_idx = _tile(s, 512)
    l_chunk = _tile(s, 256)
    tm_out = _tile(s, 512)
    tm_ffn = _tile(s, 512)

    h = x.reshape(s, d)
    for l in range(DEPTH):
        pb = _matmul(h, wb_all, l, BF16, tm_proj, PB_WIDTH // 3, "proj_bf16")
        pf = _matmul(h, wf_all, l, F32, tm_proj, PF_WIDTH // 2, "proj_f32")
        mask = _dsa_mask(pb, pf, ksel, tq_idx, tk_idx)
        lam_init = 0.8 - 0.6 * math.exp(-0.3 * l)
        y_a, y_c = _attention_pair(pb, mask, lam_p[l], c_norm_g[l].reshape(1, C_WIDTH), lam_init, t_attn)
        y_m = _mlstm(pb, pf, conv_w[l], b_i_p[l], b_f_p[l], m_norm_g[l].reshape(1, M_WIDTH), l_chunk)
        h = _out_proj(y_a, y_m, y_c, w_out_b, l, h, ln1_g[l].reshape(1, d), ln1_b[l].reshape(1, d), tm_out)
        h = _ffn(h, w_up_b, w_down_b, l, ln2_g[l].reshape(1, d), ln2_b[l].reshape(1, d), tm_ffn, 1024)
    return h.reshape(batch, s, d)
```
